```python
import jax, jax.numpy as jnp
from jax import lax
import numpy as np

D_MODEL = 4096
BATCH = 2
SEQ = 4096
DEPTH = 1
DEC_BATCH = 128
DEC_SEQ = 1
PAST_LEN = 2048
PAGE_SIZE = 128

MIX_W = D_MODEL
DN_DK = 128
DN_DV = 128
DN_HEADS = MIX_W // 2 // DN_DK
DN_CONV = 4
DN_CHUNK = 64
FOX_HD = 128
FOX_HEADS = MIX_W // 4 // FOX_HD
FOX_BLOCK = 128
MEM_TOKENS = 256
MEM_HEADS = 4
MEM_HD = MIX_W // 4 // MEM_HEADS
PEER_HEADS = 8
PEER_NKEYS = 128
PEER_EXPERTS = PEER_NKEYS * PEER_NKEYS
PEER_DKEY = 128
PEER_TOPK = 16
PEER_BLOCK = 128
EPS = 1e-6

DN_QK = DN_HEADS * DN_DK
DN_VW = DN_HEADS * DN_DV
FOX_W = FOX_HEADS * FOX_HD
MEM_W = MEM_HEADS * MEM_HD
CONV_CH = 2 * DN_QK + DN_VW
IN_SIZES = (CONV_CH, DN_VW, DN_HEADS, DN_HEADS, FOX_W, FOX_W, FOX_W, FOX_HEADS, MEM_W)
IN_COLS = sum(IN_SIZES)
IN_SPLITS = tuple(int(v) for v in np.cumsum(IN_SIZES)[:-1])
F32 = jnp.float32

kernel_name = 'hymba_deltanet_fox_peer_step'


def rmsnorm(x, g):
    xf = x.astype(F32)
    y = xf * lax.rsqrt(jnp.mean(xf * xf, axis=-1, keepdims=True) + EPS) * g.astype(F32)
    return y.astype(x.dtype)


def l2norm(x):
    xf = x.astype(F32)
    return xf * lax.rsqrt(jnp.sum(xf * xf, axis=-1, keepdims=True) + EPS)


def project_mixer(xn, w_in):
    p = jnp.einsum('btd,dc->btc', xn, w_in)
    return jnp.split(p, IN_SPLITS, axis=-1)


def short_conv(u, buf, w):
    T = u.shape[1]
    full = jnp.concatenate([buf, u], axis=1)
    out = full[:, 0:T] * w[0]
    for i in range(1, DN_CONV):
        out = out + full[:, i:i + T] * w[i]
    return jax.nn.silu(out), full[:, full.shape[1] - (DN_CONV - 1):]


def dn_prep(qkv, b, a, a_log, dt_bias):
    B, T, _ = qkv.shape
    q, k, v = jnp.split(qkv, (DN_QK, 2 * DN_QK), axis=-1)
    q = l2norm(q.reshape(B, T, DN_HEADS, DN_DK)) * (DN_DK ** -0.5)
    k = l2norm(k.reshape(B, T, DN_HEADS, DN_DK))
    v = v.reshape(B, T, DN_HEADS, DN_DV).astype(F32)
    beta = jax.nn.sigmoid(b.astype(F32))
    g = -jnp.exp(a_log.astype(F32)) * jax.nn.softplus(a.astype(F32) + dt_bias.astype(F32))
    return q, k, v, g, beta


def gated_delta_chunked(q, k, v, g, beta, s0):
    B, T, H, _ = q.shape
    C = DN_CHUNK
    N = T // C

    def chunks(t):
        return jnp.moveaxis(t.reshape((B, N, C, H) + t.shape[3:]), 3, 2)

    q, k, v, g, beta = chunks(q), chunks(k), chunks(v), chunks(g), chunks(beta)
    gc = jnp.cumsum(g, axis=-1)
    tri = jnp.tril(jnp.ones((C, C), bool))
    strict = jnp.tril(jnp.ones((C, C), bool), -1)
    diff = gc[..., :, None] - gc[..., None, :]
    decay = jnp.where(tri, jnp.exp(jnp.where(tri, diff, 0.0)), 0.0)
    kk = jnp.einsum('bnhid,bnhjd->bnhij', k, k)
    lower = jnp.where(strict, beta[..., :, None] * kk * decay, 0.0) + jnp.eye(C, dtype=F32)
    rhs = jnp.concatenate([v * beta[..., None], k * (beta * jnp.exp(gc))[..., None]], axis=-1)
    sol = lax.linalg.triangular_solve(lower, rhs, left_side=True, lower=True, unit_diagonal=True)
    u, w = sol[..., :DN_DV], sol[..., DN_DV:]
    qk = jnp.where(tri, jnp.einsum('bnhid,bnhjd->bnhij', q, k) * decay, 0.0)
    q_dec = q * jnp.exp(gc)[..., None]
    k_dec = k * jnp.exp(gc[..., -1:] - gc)[..., None]
    g_last = jnp.exp(gc[..., -1])

    def step(S, xs):
        qk_i, qd_i, kd_i, u_i, w_i, gl_i = xs
        v_new = u_i - jnp.einsum('bhck,bhkv->bhcv', w_i, S)
        o = jnp.einsum('bhck,bhkv->bhcv', qd_i, S) + jnp.einsum('bhij,bhjv->bhiv', qk_i, v_new)
        S = S * gl_i[..., None, None] + jnp.einsum('bhck,bhcv->bhkv', kd_i, v_new)
        return S, o

    xs = tuple(jnp.moveaxis(t, 1, 0) for t in (qk, q_dec, k_dec, u, w, g_last))
    S, o = lax.scan(step, s0, xs)
    o = jnp.moveaxis(jnp.moveaxis(o, 0, 1), 3, 2).reshape(B, T, H, DN_DV)
    return o, S


def gated_delta_recurrent(q, k, v, g, beta, s0):
    def step(S, xs):
        q_t, k_t, v_t, g_t, b_t = xs
        S = S * jnp.exp(g_t)[..., None, None]
        kv = jnp.einsum('bhk,bhkv->bhv', k_t, S)
        S = S + jnp.einsum('bhk,bhv->bhkv', k_t, (v_t - kv) * b_t[..., None])
        return S, jnp.einsum('bhk,bhkv->bhv', q_t, S)

    xs = tuple(jnp.moveaxis(t, 1, 0) for t in (q, k, v, g, beta))
    S, o = lax.scan(step, s0, xs)
    return jnp.moveaxis(o, 0, 1), S


def dn_output(o, z, g):
    B, T = z.shape[:2]
    o = rmsnorm(o, g) * jax.nn.silu(z.reshape(B, T, DN_HEADS, DN_DV).astype(F32))
    return o.reshape(B, T, DN_VW)


def fox_prep(fq, fk, fv, ff, f_bias, qn_g, kn_g):
    B, T, _ = fq.shape
    shp = (B, T, FOX_HEADS, FOX_HD)
    q = rmsnorm(fq.reshape(shp), qn_g)
    k = rmsnorm(fk.reshape(shp), kn_g)
    logf = jax.nn.log_sigmoid(ff.astype(F32) + f_bias.astype(F32))
    return q, k, fv.reshape(shp), logf


def fox_probs(q, k, c_q, c_k, pos_q, pos_k):
    s = jnp.einsum('bqhd,bkhd->bhqk', q, k, preferred_element_type=F32) * (FOX_HD ** -0.5)
    s = s + (jnp.swapaxes(c_q, 1, 2)[..., :, None] - jnp.swapaxes(c_k, 1, 2)[..., None, :])
    s = jnp.where(pos_k[None, :] <= pos_q[:, None], s, -jnp.inf)
    return jax.nn.softmax(s, axis=-1)


def fox_prompt(q, k, v, logf):
    B, T, H, D = q.shape
    c = jnp.cumsum(logf, axis=1)
    nb = T // FOX_BLOCK
    qb = jnp.moveaxis(q.reshape(B, nb, FOX_BLOCK, H, D), 1, 0)
    cb = jnp.moveaxis(c.reshape(B, nb, FOX_BLOCK, H), 1, 0)
    pos_k = jnp.arange(T)

    def block(args):
        i, q_i, c_i = args
        p = fox_probs(q_i, k, c_i, c, i * FOX_BLOCK + jnp.arange(FOX_BLOCK), pos_k)
        return jnp.einsum('bhqk,bkhd->bqhd', p.astype(v.dtype), v)

    o = lax.map(block, (jnp.arange(nb), qb, cb))
    return jnp.moveaxis(o, 0, 1).reshape(B, T, H * D)


def fox_sample(q, k, v, logf, k_past, v_past, logf_past):
    B, T, H, D = q.shape
    P = k_past.shape[1]
    k_all = jnp.concatenate([k_past.astype(k.dtype), k], axis=1)
    v_all = jnp.concatenate([v_past.astype(v.dtype), v], axis=1)
    c = jnp.cumsum(jnp.concatenate([logf_past.astype(F32), logf], axis=1), axis=1)
    p = fox_probs(q, k_all, c[:, P:], c, P + jnp.arange(T), jnp.arange(P + T))
    o = jnp.einsum('bhqk,bkhd->bqhd', p.astype(v_all.dtype), v_all)
    return o.reshape(B, T, H * D)


def mem_kv(mem, ln_g, w_kv, kn_g):
    B, M, _ = mem.shape
    kv = jnp.einsum('bmd,dc->bmc', rmsnorm(mem, ln_g), w_kv)
    k, v = jnp.split(kv, 2, axis=-1)
    k = rmsnorm(k.reshape(B, M, MEM_HEADS, MEM_HD), kn_g)
    return k, v.reshape(B, M, MEM_HEADS, MEM_HD)


def mem_attend(mq, qn_g, mk, mv):
    B, T, _ = mq.shape
    q = rmsnorm(mq.reshape(B, T, MEM_HEADS, MEM_HD), qn_g)
    s = jnp.einsum('bthd,bmhd->bhtm', q, mk.astype(q.dtype), preferred_element_type=F32) * (MEM_HD ** -0.5)
    p = jax.nn.softmax(s, axis=-1)
    mv = mv.astype(q.dtype)
    return jnp.einsum('bhtm,bmhd->bthd', p.astype(mv.dtype), mv).reshape(B, T, MEM_W)


def peer_tokens(xt, w_q, sub_keys, u_tab, v_tab):
    T = xt.shape[0]
    q = jnp.einsum('td,dc->tc', xt, w_q).astype(F32).reshape(T, PEER_HEADS, 2, PEER_DKEY // 2)
    s = jnp.einsum('thpd,hpnd->thpn', q, sub_keys.astype(F32))
    sv, si = lax.top_k(s, PEER_TOPK)
    cand = (sv[:, :, 0, :, None] + sv[:, :, 1, None, :]).reshape(T, PEER_HEADS, PEER_TOPK * PEER_TOPK)
    cv, ci = lax.top_k(cand, PEER_TOPK)
    i1 = jnp.take_along_axis(si[:, :, 0], ci // PEER_TOPK, axis=-1)
    i2 = jnp.take_along_axis(si[:, :, 1], ci % PEER_TOPK, axis=-1)
    idx = (i1 * PEER_NKEYS + i2).reshape(T, PEER_HEADS * PEER_TOPK)
    gate = jax.nn.softmax(cv, axis=-1).reshape(T, PEER_HEADS * PEER_TOPK)
    act = jnp.einsum('ted,td->te', u_tab[idx], xt, preferred_element_type=F32)
    h = gate * jax.nn.gelu(act, approximate=False)
    return jnp.einsum('te,ted->td', h.astype(xt.dtype), v_tab[idx])


def peer_ffn(hn, w_q, sub_keys, u_tab, v_tab):
    B, T, D = hn.shape
    n = B * T
    xt = hn.reshape(n, D)
    if n % PEER_BLOCK == 0:
        out = lax.map(lambda xb: peer_tokens(xb, w_q, sub_keys, u_tab, v_tab),
                      xt.reshape(n // PEER_BLOCK, PEER_BLOCK, D))
    else:
        out = peer_tokens(xt, w_q, sub_keys, u_tab, v_tab)
    return out.reshape(B, T, D)


def mix_and_channel(x, o_dn, o_fox, o_mem, lw):
    mix = jnp.concatenate([o_dn.astype(x.dtype), o_fox.astype(x.dtype), o_mem.astype(x.dtype)], axis=-1)
    h = x + jnp.einsum('btc,cd->btd', mix, lw['w_out'])
    return h + peer_ffn(rmsnorm(h, lw['ln_ffn_g']), lw['peer_w_q'], lw['peer_sub_keys'], lw['peer_u'], lw['peer_v'])


def prompt_layer(x, mem, lw):
    B, T, _ = x.shape
    xn = rmsnorm(x, lw['ln_mix_g'])
    qkv_pre, z, b, a, fq, fk, fv, ff, mq = project_mixer(xn, lw['w_in'])
    qkv, conv_new = short_conv(qkv_pre, jnp.zeros((B, DN_CONV - 1, CONV_CH), x.dtype), lw['conv_w'])
    q, k, v, g, beta = dn_prep(qkv, b, a, lw['dn_a_log'], lw['dn_dt_bias'])
    o_dn, dn_new = gated_delta_chunked(q, k, v, g, beta, jnp.zeros((B, DN_HEADS, DN_DK, DN_DV), F32))
    o_dn = dn_output(o_dn, z, lw['dn_norm_g'])
    fq, fk, fv, logf = fox_prep(fq, fk, fv, ff, lw['fox_f_bias'], lw['fox_qn_g'], lw['fox_kn_g'])
    o_fox = fox_prompt(fq, fk, fv, logf)
    mk, mv = mem_kv(mem, lw['ln_mem_g'], lw['w_mem_kv'], lw['mem_kn_g'])
    o_mem = mem_attend(mq, lw['mem_qn_g'], mk, mv)
    y = mix_and_channel(x, o_dn, o_fox, o_mem, lw)
    return y, fk, fv, logf, dn_new, conv_new, mk, mv


def sample_layer(x, fox_k_pool, fox_v_pool, fox_lf_pool, s_delta, s_conv, mem_k, mem_v, page_table, lw):
    B, T, _ = x.shape
    xn = rmsnorm(x, lw['ln_mix_g'])
    qkv_pre, z, b, a, fq, fk, fv, ff, mq = project_mixer(xn, lw['w_in'])
    qkv, conv_new = short_conv(qkv_pre, s_conv.astype(x.dtype), lw['conv_w'])
    q, k, v, g, beta = dn_prep(qkv, b, a, lw['dn_a_log'], lw['dn_dt_bias'])
    o_dn, dn_new = gated_delta_recurrent(q, k, v, g, beta, s_delta.astype(F32))
    o_dn = dn_output(o_dn, z, lw['dn_norm_g'])
    fq, fk, fv, logf = fox_prep(fq, fk, fv, ff, lw['fox_f_bias'], lw['fox_qn_g'], lw['fox_kn_g'])
    k_past = fox_k_pool[page_table].reshape(B, -1, FOX_HEADS, FOX_HD)
    v_past = fox_v_pool[page_table].reshape(B, -1, FOX_HEADS, FOX_HD)
    lf_past = fox_lf_pool[page_table].reshape(B, -1, FOX_HEADS)
    o_fox = fox_sample(fq, fk, fv, logf, k_past, v_past, lf_past)
    o_mem = mem_attend(mq, lw['mem_qn_g'], mem_k, mem_v)
    y = mix_and_channel(x, o_dn, o_fox, o_mem, lw)
    return y, fk, fv, logf, dn_new, conv_new


def setup_inputs(seed: int = 0) -> dict:
    key = jax.random.key(seed)
    keys = iter(jax.random.split(key, 48))

    def nrm(shape, scale):
        return jax.random.normal(next(keys), shape, jnp.float32) * scale

    def gain(shape):
        return 1.0 + nrm(shape, 0.01)

    n_pages = PAST_LEN // PAGE_SIZE
    n_used = DEC_BATCH * n_pages
    n_pool = n_used + (n_used + 3) // 4
    perm = jax.random.permutation(next(keys), n_pool)
    page_table = perm[:n_used].reshape(DEC_BATCH, n_pages).astype(jnp.int32)
    L = DEPTH
    d_in = D_MODEL ** -0.5
    return {
        'x_prompt': nrm((BATCH, SEQ, D_MODEL), 1.0),
        'x_sample': nrm((DEC_BATCH, DEC_SEQ, D_MODEL), 1.0),
        'cache_fox_k': nrm((L, n_pool, PAGE_SIZE, FOX_HEADS, FOX_HD), 1.0),
        'cache_fox_v': nrm((L, n_pool, PAGE_SIZE, FOX_HEADS, FOX_HD), 1.0),
        'cache_fox_logf': jax.nn.log_sigmoid(3.0 + nrm((L, n_pool, PAGE_SIZE, FOX_HEADS), 1.0)),
        'state_delta': nrm((L, DEC_BATCH, DN_HEADS, DN_DK, DN_DV), 0.1),
        'state_conv': nrm((L, DEC_BATCH, DN_CONV - 1, CONV_CH), 1.0),
        'cache_mem_k': nrm((L, DEC_BATCH, MEM_TOKENS, MEM_HEADS, MEM_HD), 1.0),
        'cache_mem_v': nrm((L, DEC_BATCH, MEM_TOKENS, MEM_HEADS, MEM_HD), 1.0),
        'page_table': page_table,
        'mem_prompt': nrm((BATCH, MEM_TOKENS, D_MODEL), 1.0),
        'ln_mix_g': gain((L, D_MODEL)),
        'w_in': nrm((L, D_MODEL, IN_COLS), d_in),
        'conv_w': nrm((L, DN_CONV, CONV_CH), DN_CONV ** -0.5),
        'dn_a_log': jnp.log(jax.random.uniform(next(keys), (L, DN_HEADS), jnp.float32, 1.0, 16.0)),
        'dn_dt_bias': nrm((L, DN_HEADS), 0.1),
        'dn_norm_g': gain((L, DN_DV)),
        'fox_f_bias': 3.0 + nrm((L, FOX_HEADS), 0.5),
        'fox_qn_g': gain((L, FOX_HD)),
        'fox_kn_g': gain((L, FOX_HD)),
        'ln_mem_g': gain((L, D_MODEL)),
        'w_mem_kv': nrm((L, D_MODEL, 2 * MEM_W), d_in),
        'mem_qn_g': gain((L, MEM_HD)),
        'mem_kn_g': gain((L, MEM_HD)),
        'w_out': nrm((L, MIX_W, D_MODEL), MIX_W ** -0.5),
        'ln_ffn_g': gain((L, D_MODEL)),
        'peer_w_q': nrm((L, D_MODEL, PEER_HEADS * PEER_DKEY), d_in),
        'peer_sub_keys': nrm((L, PEER_HEADS, 2, PEER_NKEYS, PEER_DKEY // 2), (PEER_DKEY // 2) ** -0.5),
        'peer_u': nrm((L, PEER_EXPERTS, D_MODEL), d_in),
        'peer_v': nrm((L, PEER_EXPERTS, D_MODEL), PEER_HEADS ** -0.5),
    }


def reference(x_prompt, x_sample, cache_fox_k, cache_fox_v, cache_fox_logf, state_delta, state_conv,
              cache_mem_k, cache_mem_v, page_table, mem_prompt, ln_mix_g, w_in, conv_w, dn_a_log,
              dn_dt_bias, dn_norm_g, fox_f_bias, fox_qn_g, fox_kn_g, ln_mem_g, w_mem_kv, mem_qn_g,
              mem_kn_g, w_out, ln_ffn_g, peer_w_q, peer_sub_keys, peer_u, peer_v):
    y_p, y_s = x_prompt, x_sample
    p_fk, p_fv, p_lf, p_dn, p_cv, p_mk, p_mv = [], [], [], [], [], [], []
    s_fk, s_fv, s_lf, s_dn, s_cv = [], [], [], [], []
    for l in range(DEPTH):
        lw = {'ln_mix_g': ln_mix_g[l], 'w_in': w_in[l], 'conv_w': conv_w[l], 'dn_a_log': dn_a_log[l],
              'dn_dt_bias': dn_dt_bias[l], 'dn_norm_g': dn_norm_g[l], 'fox_f_bias': fox_f_bias[l],
              'fox_qn_g': fox_qn_g[l], 'fox_kn_g': fox_kn_g[l], 'ln_mem_g': ln_mem_g[l],
              'w_mem_kv': w_mem_kv[l], 'mem_qn_g': mem_qn_g[l], 'mem_kn_g': mem_kn_g[l],
              'w_out': w_out[l], 'ln_ffn_g': ln_ffn_g[l], 'peer_w_q': peer_w_q[l],
              'peer_sub_keys': peer_sub_keys[l], 'peer_u': peer_u[l], 'peer_v': peer_v[l]}
        y_p, fk, fv, lf, dn, cv, mk, mv = prompt_layer(y_p, mem_prompt, lw)
        p_fk.append(fk); p_fv.append(fv); p_lf.append(lf); p_dn.append(dn); p_cv.append(cv)
        p_mk.append(mk); p_mv.append(mv)
        y_s, fk, fv, lf, dn, cv = sample_layer(y_s, cache_fox_k[l], cache_fox_v[l], cache_fox_logf[l],
                                               state_delta[l], state_conv[l], cache_mem_k[l],
                                               cache_mem_v[l], page_table, lw)
        s_fk.append(fk); s_fv.append(fv); s_lf.append(lf); s_dn.append(dn); s_cv.append(cv)
    return (y_p, y_s,
            jnp.stack(p_fk), jnp.stack(p_fv), jnp.stack(p_lf), jnp.stack(p_dn), jnp.stack(p_cv),
            jnp.stack(p_mk), jnp.stack(p_mv),
            jnp.stack(s_fk), jnp.stack(s_fv), jnp.stack(s_lf), jnp.stack(s_dn), jnp.stack(s_cv))
```

```python
import functools
import math

import jax
import jax.numpy as jnp
import numpy as np
from jax import lax
from jax.experimental import pallas as pl
from jax.experimental.pallas import tpu as pltpu

F32 = jnp.float32
BF16 = jnp.bfloat16
HIGHEST = lax.Precision.HIGHEST
EPS = 1e-6
NEG_INF = float("-inf")

D_MODEL = 4096
DN_HEADS = 16
DN_D = 128
DN_CONV = 4
DN_CHUNK = 64
FOX_HEADS = 8
FOX_HD = 128
MEM_TOKENS = 256
MEM_HEADS = 4
MEM_HD = 256
PEER_HEADS = 8
PEER_NKEYS = 128
PEER_DKEY = 128
PEER_TOPK = 16
PAGE_SIZE = 128

DN_QK = DN_HEADS * DN_D
CONV_CH = 3 * DN_QK
FOX_W = FOX_HEADS * FOX_HD
MEM_W = MEM_HEADS * MEM_HD
_OFF_Z = CONV_CH
_OFF_B = _OFF_Z + DN_QK
_OFF_A = _OFF_B + DN_HEADS
_OFF_FQ = _OFF_A + DN_HEADS
_OFF_FK = _OFF_FQ + FOX_W
_OFF_FV = _OFF_FK + FOX_W
_OFF_FF = _OFF_FV + FOX_W
_OFF_MQ = _OFF_FF + FOX_HEADS
_IN_COLS = _OFF_MQ + MEM_W
P_COLS = CONV_CH + DN_QK + 3 * FOX_W + MEM_W
PB_FQ, PB_FK, PB_FV, PB_MQ = 8, 9, 10, 11
LANE = 128
SMALL_COLS = LANE

VMEM_LIMIT_BYTES = 56 * 1024 * 1024


def _cparams(*sem):
    return pltpu.CompilerParams(dimension_semantics=sem, vmem_limit_bytes=VMEM_LIMIT_BYTES)


def _sigmoid(x):
    return 1.0 / (1.0 + jnp.exp(-x))


def _softplus(x):
    return jnp.maximum(x, 0.0) + jnp.log1p(jnp.exp(-jnp.abs(x)))


def _rmsnorm_kernel(x_ref, g_ref, o_ref):
    x = x_ref[...]
    y = x * lax.rsqrt(jnp.mean(x * x, axis=-1, keepdims=True) + EPS) * g_ref[...]
    o_ref[...] = y.astype(o_ref.dtype)


def rmsnorm_rows(x, g, tm, out_dtype=BF16):
    m, k = x.shape
    return pl.pallas_call(
        _rmsnorm_kernel,
        grid=(m // tm,),
        in_specs=[pl.BlockSpec((tm, k), lambda i: (i, 0)), pl.BlockSpec((1, k), lambda i: (0, 0))],
        out_specs=pl.BlockSpec((tm, k), lambda i: (i, 0)),
        out_shape=jax.ShapeDtypeStruct((m, k), out_dtype),
        compiler_params=_cparams("parallel"),
        name="rmsnorm_rows",
    )(x, g.reshape(1, k))


def _head_rmsnorm_kernel(x_ref, g_ref, o_ref, *, heads, hd):
    g = g_ref[...]
    for h in range(heads):
        x = x_ref[:, h * hd:(h + 1) * hd]
        y = x * lax.rsqrt(jnp.mean(x * x, axis=-1, keepdims=True) + EPS) * g
        o_ref[:, h * hd:(h + 1) * hd] = y.astype(o_ref.dtype)


def head_rmsnorm(x, col_block, g, heads, hd, tm, out_dtype=F32):
    m = x.shape[0]
    w = heads * hd
    return pl.pallas_call(
        functools.partial(_head_rmsnorm_kernel, heads=heads, hd=hd),
        grid=(m // tm,),
        in_specs=[pl.BlockSpec((tm, w), lambda i: (i, col_block)), pl.BlockSpec((1, hd), lambda i: (0, 0))],
        out_specs=pl.BlockSpec((tm, w), lambda i: (i, 0)),
        out_shape=jax.ShapeDtypeStruct((m, w), out_dtype),
        compiler_params=_cparams("parallel"),
        name="head_rmsnorm",
    )(x, g.reshape(1, hd))


def _matmul_kernel(*refs, ksizes, has_res):
    n_a = len(ksizes)
    w_ref = refs[n_a]
    o_ref = refs[-1]
    acc = None
    off = 0
    for a_ref, ks in zip(refs[:n_a], ksizes):
        part = jnp.dot(a_ref[...], w_ref[off:off + ks, :], preferred_element_type=F32)
        acc = part if acc is None else acc + part
        off += ks
    if has_res:
        acc = acc + refs[n_a + 1][...]
    o_ref[...] = acc


def matmul(a_list, w, tm, tn, residual=None):
    m = a_list[0].shape[0]
    k, n = w.shape
    ksizes = tuple(a.shape[1] for a in a_list)
    assert sum(ksizes) == k and m % tm == 0 and n % tn == 0
    in_specs = [pl.BlockSpec((tm, ks), lambda i, j: (i, 0)) for ks in ksizes]
    in_specs.append(pl.BlockSpec((k, tn), lambda i, j: (0, j)))
    args = list(a_list) + [w]
    if residual is not None:
        in_specs.append(pl.BlockSpec((tm, tn), lambda i, j: (i, j)))
        args.append(residual)
    return pl.pallas_call(
        functools.partial(_matmul_kernel, ksizes=ksizes, has_res=residual is not None),
        grid=(m // tm, n // tn),
        in_specs=in_specs,
        out_specs=pl.BlockSpec((tm, tn), lambda i, j: (i, j)),
        out_shape=jax.ShapeDtypeStruct((m, n), F32),
        compiler_params=_cparams("parallel", "parallel"),
        name="matmul",
    )(*args)


def _gates_kernel(p_ref, alog_ref, dtb_ref, fb_ref, o_ref):
    x = p_ref[...]
    lane = lax.broadcasted_iota(jnp.int32, x.shape, 1)
    beta = _sigmoid(x)
    g = -jnp.exp(alog_ref[...]) * _softplus(x + dtb_ref[...])
    logf = -_softplus(-(x + fb_ref[...]))
    out = jnp.where(lane < DN_HEADS, beta,
                    jnp.where(lane < 2 * DN_HEADS, g,
                              jnp.where(lane < 2 * DN_HEADS + FOX_HEADS, logf, 0.0)))
    o_ref[...] = out


def gates(p_small, a_log, dt_bias, f_bias, tm):
    m = p_small.shape[0]

    def pad(v, off):
        return jnp.zeros((1, SMALL_COLS), F32).at[0, off:off + v.shape[0]].set(v)

    row = pl.BlockSpec((1, SMALL_COLS), lambda i: (0, 0))
    return pl.pallas_call(
        _gates_kernel,
        grid=(m // tm,),
        in_specs=[pl.BlockSpec((tm, SMALL_COLS), lambda i: (i, 0)), row, row, row],
        out_specs=pl.BlockSpec((tm, SMALL_COLS), lambda i: (i, 0)),
        out_shape=jax.ShapeDtypeStruct((m, SMALL_COLS), F32),
        compiler_params=_cparams("parallel"),
        name="gates",
    )(p_small, pad(a_log, DN_HEADS), pad(dt_bias, DN_HEADS), pad(f_bias, 2 * DN_HEADS))


def _cumsum_kernel(x_ref, o_ref, *, blocks_per_group):
    x = x_ref[...]
    r, n = x.shape
    a = lax.broadcasted_iota(jnp.int32, (n, n), 0)
    b = lax.broadcasted_iota(jnp.int32, (n, n), 1)
    local = jnp.dot(x, (a <= b).astype(F32), precision=HIGHEST, preferred_element_type=F32)
    tot = jnp.broadcast_to(local[:, n - 1:n], (r, n))
    ra = lax.broadcasted_iota(jnp.int32, (r, r), 0)
    rb = lax.broadcasted_iota(jnp.int32, (r, r), 1)
    earlier = jnp.logical_and(rb < ra, (ra // blocks_per_group) == (rb // blocks_per_group)).astype(F32)
    o_ref[...] = local + jnp.dot(earlier, tot, precision=HIGHEST, preferred_element_type=F32)


def cumsum_time(x, blocks_per_group):
    return pl.pallas_call(
        functools.partial(_cumsum_kernel, blocks_per_group=blocks_per_group),
        out_shape=jax.ShapeDtypeStruct(x.shape, F32),
        compiler_params=pltpu.CompilerParams(vmem_limit_bytes=VMEM_LIMIT_BYTES),
        name="cumsum_time",
    )(x)


def _row_to_col(row, n):
    a = lax.broadcasted_iota(jnp.int32, (n, n), 0)
    b = lax.broadcasted_iota(jnp.int32, (n, n), 1)
    return jnp.sum(jnp.where(a == b, row, 0.0), axis=1, keepdims=True)


def _fox_prompt_kernel(q_ref, k_ref, v_ref, c_ref, o_ref, *, tq, tk, scale):
    i = pl.program_id(2)
    q = q_ref[...]
    q0 = pl.multiple_of(i * tq, tq)
    cq = _row_to_col(c_ref[0, :, pl.ds(q0, tq)], tq)
    qpos = q0 + lax.broadcasted_iota(jnp.int32, (tq, 1), 0)
    kiota = lax.broadcasted_iota(jnp.int32, (1, tk), 1)

    def body(j, carry):
        m, l, acc = carry
        k0 = pl.multiple_of(j * tk, tk)
        k = k_ref[pl.ds(k0, tk), :]
        v = v_ref[pl.ds(k0, tk), :]
        s = lax.dot_general(q, k, (((1,), (1,)), ((), ())), preferred_element_type=F32) * scale
        s = s + (cq - c_ref[0, :, pl.ds(k0, tk)])
        s = jnp.where(k0 + kiota <= qpos, s, NEG_INF)
        m_new = jnp.maximum(m, jnp.max(s, axis=1, keepdims=True))
        alpha = jnp.exp(m - m_new)
        p = jnp.exp(s - m_new)
        l = l * alpha + jnp.sum(p, axis=1, keepdims=True)
        acc = acc * alpha + jnp.dot(p.astype(BF16), v, preferred_element_type=F32)
        return m_new, l, acc

    n_kv = (i + 1) * (tq // tk)
    init = (jnp.full((tq, 1), NEG_INF, F32), jnp.zeros((tq, 1), F32), jnp.zeros((tq, FOX_HD), F32))
    _, l, acc = lax.fori_loop(0, n_kv, body, init)
    o_ref[...] = (acc / l).astype(o_ref.dtype)


def fox_prompt(qn, kn, v, c_row, batch, seq, tq, tk):
    nq = seq // tq
    kv_spec = pl.BlockSpec((seq, FOX_HD), lambda b, h, i: (b, h))
    return pl.pallas_call(
        functools.partial(_fox_prompt_kernel, tq=tq, tk=tk, scale=FOX_HD ** -0.5),
        grid=(batch, FOX_HEADS, nq),
        in_specs=[pl.BlockSpec((tq, FOX_HD), lambda b, h, i: (b * nq + i, h)), kv_spec, kv_spec,
                  pl.BlockSpec((1, 1, seq), lambda b, h, i: (b * FOX_HEADS + h, 0, 0))],
        out_specs=pl.BlockSpec((tq, FOX_HD), lambda b, h, i: (b * nq + i, h)),
        out_shape=jax.ShapeDtypeStruct((batch * seq, FOX_W), BF16),
        compiler_params=_cparams("parallel", "parallel", "arbitrary"),
        name="fox_prompt",
    )(qn, kn, v, c_row)


def _dot_hi(a, b):
    return jnp.dot(a, b, precision=HIGHEST, preferred_element_type=F32)


def _dot_nt_hi(a, b):
    return lax.dot_general(a, b, (((1,), (1,)), ((), ())), precision=HIGHEST, preferred_element_type=F32)


def _dn_prompt_kernel(q_ref, k_ref, v_ref, z_ref, wq_ref, wk_ref, wv_ref, g_ref, b_ref, ng_ref,
                      o_ref, s_out_ref, s_ref, xq_ref, xk_ref, xv_ref, *, tc, chunk):
    t = pl.program_id(2)
    nc = tc // chunk
    pad = 8

    @pl.when(t == 0)
    def _():
        s_ref[...] = jnp.zeros_like(s_ref)
        for buf in (xq_ref, xk_ref, xv_ref):
            buf[0:pad, :] = jnp.zeros((pad, DN_D), F32)

    def conv_silu(x_ref, w_ref, buf_ref):
        u = x_ref[...]
        buf_ref[pad:pad + tc, :] = u
        w = w_ref[...]
        out = buf_ref[pad - 3:pad - 3 + tc, :] * w[0:1, :]
        out = out + buf_ref[pad - 2:pad - 2 + tc, :] * w[1:2, :]
        out = out + buf_ref[pad - 1:pad - 1 + tc, :] * w[2:3, :]
        out = out + u * w[3:4, :]
        buf_ref[0:pad, :] = u[tc - pad:tc, :]
        return out * _sigmoid(out)

    def l2norm(x):
        return x * lax.rsqrt(jnp.sum(x * x, axis=-1, keepdims=True) + EPS)

    q_all = l2norm(conv_silu(q_ref, wq_ref, xq_ref)) * (DN_D ** -0.5)
    k_all = l2norm(conv_silu(k_ref, wk_ref, xk_ref))
    v_all = conv_silu(v_ref, wv_ref, xv_ref)

    ra = lax.broadcasted_iota(jnp.int32, (chunk, chunk), 0)
    rb = lax.broadcasted_iota(jnp.int32, (chunk, chunk), 1)
    tri = ra >= rb
    strict = ra > rb
    eye_f = (ra == rb).astype(F32)
    upper_f = (ra <= rb).astype(F32)
    g_all = g_ref[0]
    beta_all = b_ref[0]
    gc_all = _dot_hi(g_all, upper_f)

    s = s_ref[...]
    outs = []
    for c in range(nc):
        sl = slice(c * chunk, (c + 1) * chunk)
        q, k, v = q_all[sl], k_all[sl], v_all[sl]
        gc_row = gc_all[c:c + 1, :]
        gc = _row_to_col(gc_row, chunk)
        beta = _row_to_col(beta_all[c:c + 1, :], chunk)
        decay = jnp.where(tri, jnp.exp(jnp.where(tri, gc - gc_row, 0.0)), 0.0)
        lower = jnp.where(strict, beta * _dot_nt_hi(k, k) * decay, 0.0)
        inv = eye_f - lower
        power = _dot_hi(lower, lower)
        span = 2
        while span < chunk:
            inv = inv + _dot_hi(inv, power)
            span *= 2
            if span < chunk:
                power = _dot_hi(power, power)
        egc = jnp.exp(gc)
        u = _dot_hi(inv, v * beta)
        w = _dot_hi(inv, k * (beta * egc))
        qk = jnp.where(tri, _dot_nt_hi(q, k) * decay, 0.0)
        gc_last = gc_row[:, chunk - 1:chunk]
        k_dec = k * jnp.exp(gc_last - gc)
        v_new = u - _dot_hi(w, s)
        outs.append(_dot_hi(q * egc, s) + _dot_hi(qk, v_new))
        s = s * jnp.exp(gc_last) + _dot_hi(k_dec.T, v_new)
    s_ref[...] = s
    s_out_ref[0] = s

    o = jnp.concatenate(outs, axis=0)
    o = o * lax.rsqrt(jnp.mean(o * o, axis=-1, keepdims=True) + EPS) * ng_ref[...]
    z = z_ref[...]
    o_ref[...] = (o * (z * _sigmoid(z))).astype(o_ref.dtype)


def dn_prompt(p, conv_w, g_chunks, beta_chunks, norm_g, batch, seq, tc):
    nt = seq // tc
    nc = tc // DN_CHUNK
    hb = DN_HEADS

    def col(off):
        return pl.BlockSpec((tc, DN_D), lambda b, h, t: (b * nt + t, off + h))

    def wcol(off):
        return pl.BlockSpec((DN_CONV, DN_D), lambda b, h, t: (0, off + h))

    gspec = pl.BlockSpec((1, nc, DN_CHUNK), lambda b, h, t: (b * DN_HEADS + h, t, 0))
    return pl.pallas_call(
        functools.partial(_dn_prompt_kernel, tc=tc, chunk=DN_CHUNK),
        grid=(batch, DN_HEADS, nt),
        in_specs=[col(0), col(hb), col(2 * hb), col(3 * hb), wcol(0), wcol(hb), wcol(2 * hb), gspec, gspec,
                  pl.BlockSpec((1, DN_D), lambda b, h, t: (0, 0))],
        out_specs=[pl.BlockSpec((tc, DN_D), lambda b, h, t: (b * nt + t, h)),
                   pl.BlockSpec((1, DN_D, DN_D), lambda b, h, t: (b * DN_HEADS + h, 0, 0))],
        out_shape=[jax.ShapeDtypeStruct((batch * seq, DN_QK), BF16),
                   jax.ShapeDtypeStruct((batch * DN_HEADS, DN_D, DN_D), F32)],
        scratch_shapes=[pltpu.VMEM((DN_D, DN_D), F32)] + [pltpu.VMEM((tc + 8, DN_D), F32)] * 3,
        compiler_params=_cparams("parallel", "parallel", "arbitrary"),
        name="dn_prompt",
    )(p, p, p, p, conv_w, conv_w, conv_w, g_chunks, beta_chunks, norm_g.reshape(1, DN_D))


def _dn_sample_kernel(q_ref, k_ref, v_ref, z_ref, cq_ref, ck_ref, cv_ref, wq_ref, wk_ref, wv_ref,
                      g_ref, b_ref, ng_ref, s_ref, o_ref, s_out_ref, *, bb):
    def conv_silu(x_ref, c_ref, w_ref):
        w = w_ref[...]
        out = c_ref[:, 0, :] * w[0:1, :]
        out = out + c_ref[:, 1, :] * w[1:2, :]
        out = out + c_ref[:, 2, :] * w[2:3, :]
        out = out + x_ref[...] * w[3:4, :]
        return out * _sigmoid(out)

    def l2norm(x):
        return x * lax.rsqrt(jnp.sum(x * x, axis=-1, keepdims=True) + EPS)

    q = l2norm(conv_silu(q_ref, cq_ref, wq_ref)) * (DN_D ** -0.5)
    k = l2norm(conv_silu(k_ref, ck_ref, wk_ref))
    v = conv_silu(v_ref, cv_ref, wv_ref)
    q_t = q.T
    k_t = k.T
    decay = jnp.exp(g_ref[0])
    beta = b_ref[0]
    rows = []
    for b in range(bb):
        s = s_ref[b, 0] * decay[b:b + 1, :]
        k_col = k_t[:, b:b + 1]
        kv = jnp.sum(k_col * s, axis=0, keepdims=True)
        s = s + k_col * ((v[b:b + 1, :] - kv) * beta[b:b + 1, :])
        s_out_ref[b, 0] = s
        rows.append(jnp.sum(q_t[:, b:b + 1] * s, axis=0, keepdims=True))
    o = jnp.concatenate(rows, axis=0)
    o = o * lax.rsqrt(jnp.mean(o * o, axis=-1, keepdims=True) + EPS) * ng_ref[...]
    z = z_ref[...]
    o_ref[...] = (o * (z * _sigmoid(z))).astype(o_ref.dtype)


def dn_sample(p, state_conv, conv_w, g_t, beta_t, norm_g, state, bb):
    nb = p.shape[0]
    hb = DN_HEADS

    def col(off):
        return pl.BlockSpec((bb, DN_D), lambda i, h: (i, off + h))

    def ccol(off):
        return pl.BlockSpec((bb, DN_CONV - 1, DN_D), lambda i, h: (i, 0, off + h))

    def wcol(off):
        return pl.BlockSpec((DN_CONV, DN_D), lambda i, h: (0, off + h))

    gspec = pl.BlockSpec((1, bb, 1), lambda i, h: (h, i, 0))
    sspec = pl.BlockSpec((bb, 1, DN_D, DN_D), lambda i, h: (i, h, 0, 0))
    return pl.pallas_call(
        functools.partial(_dn_sample_kernel, bb=bb),
        grid=(nb // bb, DN_HEADS),
        in_specs=[col(0), col(hb), col(2 * hb), col(3 * hb), ccol(0), ccol(hb), ccol(2 * hb),
                  wcol(0), wcol(hb), wcol(2 * hb), gspec, gspec,
                  pl.BlockSpec((1, DN_D), lambda i, h: (0, 0)), sspec],
        out_specs=[pl.BlockSpec((bb, DN_D), lambda i, h: (i, h)), sspec],
        out_shape=[jax.ShapeDtypeStruct((nb, DN_QK), BF16), jax.ShapeDtypeStruct(state.shape, F32)],
        compiler_params=_cparams("parallel", "parallel"),
        name="dn_sample",
    )(p, p, p, p, state_conv, state_conv, state_conv, conv_w, conv_w, conv_w, g_t, beta_t,
      norm_g.reshape(1, DN_D), state)


def _fox_sample_kernel(pt_ref, q_ref, kn_ref, vn_ref, lfn_ref, *refs, pp, scale):
    k_refs = refs[0:pp]
    v_refs = refs[pp:2 * pp]
    lf_refs = refs[2 * pp:3 * pp]
    o_ref = refs[3 * pp]
    m_ref, l_ref, acc_ref, carry_ref = refs[3 * pp + 1:]
    j = pl.program_id(1)
    nh = FOX_HEADS
    hd = FOX_HD
    q = q_ref[0]

    @pl.when(j == 0)
    def _():
        prod = q * kn_ref[0]
        for h in range(nh):
            s_new = jnp.sum(prod[:, h * hd:(h + 1) * hd], axis=1, keepdims=True) * scale
            m_ref[:, h:h + 1] = s_new
        l_ref[...] = jnp.ones_like(l_ref)
        acc_ref[...] = vn_ref[0]
        carry_ref[...] = lfn_ref[0]

    ra = lax.broadcasted_iota(jnp.int32, (PAGE_SIZE, PAGE_SIZE), 0)
    rb = lax.broadcasted_iota(jnp.int32, (PAGE_SIZE, PAGE_SIZE), 1)
    later_f = (rb > ra).astype(F32)
    for i in range(pp):
        lf = lf_refs[i][0]
        carry = carry_ref[...]
        bias = _dot_hi(later_f, lf) + carry
        carry_ref[...] = carry + jnp.sum(lf, axis=0, keepdims=True)
        for h in range(nh):
            cs = slice(h * hd, (h + 1) * hd)
            kh = k_refs[i][0, :, cs]
            s = jnp.sum(kh * q[:, cs], axis=1, keepdims=True) * scale + bias[:, h:h + 1]
            m_old = m_ref[:, h:h + 1]
            m_new = jnp.maximum(m_old, jnp.max(s, axis=0, keepdims=True))
            alpha = jnp.exp(m_old - m_new)
            p = jnp.exp(s - m_new)
            l_ref[:, h:h + 1] = l_ref[:, h:h + 1] * alpha + jnp.sum(p, axis=0, keepdims=True)
            pv = jnp.sum(p * v_refs[i][0, :, cs], axis=0, keepdims=True)
            acc_ref[:, cs] = acc_ref[:, cs] * alpha + pv
            m_ref[:, h:h + 1] = m_new

    @pl.when(j == pl.num_programs(1) - 1)
    def _():
        for h in range(nh):
            cs = slice(h * hd, (h + 1) * hd)
            o_ref[0, :, cs] = (acc_ref[:, cs] / l_ref[:, h:h + 1]).astype(o_ref.dtype)


def fox_sample(q, k_new, v_new, lf_new, k_pool, v_pool, lf_pool, page_table, pp):
    nb, n_pages = page_table.shape
    assert n_pages % pp == 0
    steps = n_pages // pp

    def page_idx(i):
        return lambda b, j, pt: (pt[b, n_pages - 1 - (j * pp + i)], 0, 0)

    tok = lambda b, j, pt: (b, 0, 0)
    in_specs = [pl.BlockSpec((1, 1, FOX_W), tok)] * 3 + [pl.BlockSpec((1, 1, FOX_HEADS), tok)]
    in_specs += [pl.BlockSpec((1, PAGE_SIZE, FOX_W), page_idx(i)) for i in range(pp)]
    in_specs += [pl.BlockSpec((1, PAGE_SIZE, FOX_W), page_idx(i)) for i in range(pp)]
    in_specs += [pl.BlockSpec((1, PAGE_SIZE, FOX_HEADS), page_idx(i)) for i in range(pp)]
    grid_spec = pltpu.PrefetchScalarGridSpec(
        num_scalar_prefetch=1,
        grid=(nb, steps),
        in_specs=in_specs,
        out_specs=pl.BlockSpec((1, 1, FOX_W), tok),
        scratch_shapes=[pltpu.VMEM((1, FOX_HEADS), F32), pltpu.VMEM((1, FOX_HEADS), F32),
                        pltpu.VMEM((1, FOX_W), F32), pltpu.VMEM((1, FOX_HEADS), F32)],
    )
    return pl.pallas_call(
        functools.partial(_fox_sample_kernel, pp=pp, scale=FOX_HD ** -0.5),
        grid_spec=grid_spec,
        out_shape=jax.ShapeDtypeStruct((nb, 1, FOX_W), BF16),
        compiler_params=_cparams("parallel", "arbitrary"),
        name="fox_sample",
    )(page_table, q, k_new, v_new, lf_new, *([k_pool] * pp), *([v_pool] * pp), *([lf_pool] * pp))


def _mem_prompt_kernel(q_ref, k_ref, v_ref, g_ref, o_ref, *, scale):
    g = g_ref[...]
    for h in range(MEM_HEADS):
        cs = slice(h * MEM_HD, (h + 1) * MEM_HD)
        q = q_ref[:, cs]
        qn = (q * lax.rsqrt(jnp.mean(q * q, axis=-1, keepdims=True) + EPS) * g).astype(BF16)
        s = lax.dot_general(qn, k_ref[:, cs], (((1,), (1,)), ((), ())), preferred_element_type=F32) * scale
        p = jnp.exp(s - jnp.max(s, axis=1, keepdims=True))
        p = p / jnp.sum(p, axis=1, keepdims=True)
        o_ref[:, cs] = jnp.dot(p.astype(BF16), v_ref[:, cs], preferred_element_type=F32).astype(o_ref.dtype)


def mem_attend_prompt(p, mk, mv, qn_g, batch, seq, tq):
    nq = seq // tq
    kv = pl.BlockSpec((MEM_TOKENS, MEM_W), lambda b, i: (b, 0))
    return pl.pallas_call(
        functools.partial(_mem_prompt_kernel, scale=MEM_HD ** -0.5),
        grid=(batch, nq),
        in_specs=[pl.BlockSpec((tq, MEM_W), lambda b, i: (b * nq + i, PB_MQ)), kv, kv,
                  pl.BlockSpec((1, MEM_HD), lambda b, i: (0, 0))],
        out_specs=pl.BlockSpec((tq, MEM_W), lambda b, i: (b * nq + i, 0)),
        out_shape=jax.ShapeDtypeStruct((batch * seq, MEM_W), BF16),
        compiler_params=_cparams("parallel", "parallel"),
        name="mem_prompt",
    )(p, mk, mv, qn_g.reshape(1, MEM_HD))


def _mem_sample_kernel(q_ref, k_ref, v_ref, g_ref, o_ref, *, scale):
    g = g_ref[...]
    for h in range(MEM_HEADS):
        cs = slice(h * MEM_HD, (h + 1) * MEM_HD)
        q = q_ref[0, :, cs]
        qn = q * lax.rsqrt(jnp.mean(q * q, axis=-1, keepdims=True) + EPS) * g
        s = jnp.sum(k_ref[0, :, cs] * qn, axis=1, keepdims=True) * scale
        p = jnp.exp(s - jnp.max(s, axis=0, keepdims=True))
        p = p / jnp.sum(p, axis=0, keepdims=True)
        o_ref[0, :, cs] = jnp.sum(p * v_ref[0, :, cs], axis=0, keepdims=True).astype(o_ref.dtype)


def mem_sample(q, mk, mv, qn_g):
    nb = q.shape[0]
    tok = pl.BlockSpec((1, 1, MEM_W), lambda b: (b, 0, 0))
    kv = pl.BlockSpec((1, MEM_TOKENS, MEM_W), lambda b: (b, 0, 0))
    return pl.pallas_call(
        functools.partial(_mem_sample_kernel, scale=MEM_HD ** -0.5),
        grid=(nb,),
        in_specs=[tok, kv, kv, pl.BlockSpec((1, MEM_HD), lambda b: (0, 0))],
        out_specs=tok,
        out_shape=jax.ShapeDtypeStruct((nb, 1, MEM_W), BF16),
        compiler_params=_cparams("parallel"),
        name="mem_sample",
    )(q, mk, mv, qn_g.reshape(1, MEM_HD))


def _top_values(s, count):
    rows = s.shape[0]
    idx = lax.broadcasted_iota(jnp.int32, s.shape, 0)
    vals = []
    for _ in range(count):
        m = jnp.max(s, axis=0, keepdims=True)
        first = jnp.min(jnp.where(s == m, idx, rows), axis=0, keepdims=True)
        s = jnp.where(idx == first, NEG_INF, s)
        vals.append(m)
    return jnp.concatenate(vals, axis=0)


def _peer_stats_kernel(hn_ref, wq_ref, keys_ref, s1_ref, g1_ref, s2_ref, e2_ref, tau_ref, st_ref, top_ref):
    nk = PEER_NKEYS
    kk = PEER_TOPK
    q = jnp.dot(hn_ref[...], wq_ref[...], preferred_element_type=F32)
    st_ref[...] = lax.dot_general(keys_ref[...], q.astype(BF16), (((1,), (1,)), ((), ())),
                                  preferred_element_type=F32)

    def half_body(c, carry):
        r0 = pl.multiple_of(c * nk, nk)
        t0 = pl.multiple_of(c * kk, kk)
        top_ref[pl.ds(t0, kk), :] = _top_values(st_ref[pl.ds(r0, nk), :], kk)
        return carry

    lax.fori_loop(0, 2 * PEER_HEADS, half_body, 0)

    def head_body(h, carry):
        t0 = pl.multiple_of(h * 2 * kk, 2 * kk)
        v1 = top_ref[pl.ds(t0, kk), :]
        v2 = top_ref[pl.ds(t0 + kk, kk), :]
        cand = jnp.concatenate([v1[a:a + 1, :] + v2 for a in range(kk)], axis=0)
        cv = _top_values(cand, kk)
        z = jnp.sum(jnp.exp(cv - cv[0:1, :]), axis=0, keepdims=True)
        tau_ref[pl.ds(h, 1), :] = cv[kk - 1:kk, :]
        r0 = pl.multiple_of(h * 2 * nk, 2 * nk)
        s1 = st_ref[pl.ds(r0, nk), :]
        s2 = st_ref[pl.ds(r0 + nk, nk), :]
        s1_ref[h] = s1
        s2_ref[h] = s2
        g1_ref[h] = jnp.exp(s1 - v1[0:1, :]) / z
        e2_ref[h] = jnp.exp(s2 - v2[0:1, :])
        return carry

    lax.fori_loop(0, PEER_HEADS, head_body, 0)


def peer_stats(hn, w_q, keys_t, mt):
    m = hn.shape[0]
    nrow = PEER_HEADS * 2 * PEER_NKEYS
    hspec = pl.BlockSpec((PEER_HEADS, PEER_NKEYS, mt), lambda i: (0, 0, i))
    hshape = jax.ShapeDtypeStruct((PEER_HEADS, PEER_NKEYS, m), F32)
    return pl.pallas_call(
        _peer_stats_kernel,
        grid=(m // mt,),
        in_specs=[pl.BlockSpec((mt, D_MODEL), lambda i: (i, 0)),
                  pl.BlockSpec(w_q.shape, lambda i: (0, 0)),
                  pl.BlockSpec(keys_t.shape, lambda i: (0, 0))],
        out_specs=[hspec, hspec, hspec, hspec, pl.BlockSpec((PEER_HEADS, mt), lambda i: (0, i))],
        out_shape=[hshape, hshape, hshape, hshape, jax.ShapeDtypeStruct((PEER_HEADS, m), F32)],
        scratch_shapes=[pltpu.VMEM((nrow, mt), F32), pltpu.VMEM((2 * PEER_HEADS * PEER_TOPK, mt), F32)],
        compiler_params=_cparams("parallel"),
        name="peer_stats",
    )(hn, w_q, keys_t)


def _peer_mix_kernel(hn_ref, u_ref, v_ref, s1_ref, g1_ref, s2_ref, e2_ref, tau_ref, o_ref, *, et):
    e = pl.program_id(1)

    @pl.when(e == 0)
    def _():
        o_ref[...] = jnp.zeros_like(o_ref)

    act = lax.dot_general(u_ref[...], hn_ref[...], (((1,), (1,)), ((), ())),
                          preferred_element_type=F32)
    gel = 0.5 * act * (1.0 + lax.erf(act * (0.5 ** 0.5)))
    parts = []
    for a in range(et // PEER_NKEYS):
        w = None
        for h in range(PEER_HEADS):
            s1 = s1_ref[a, h:h + 1, :]
            hit = (s1 + s2_ref[h]) >= tau_ref[h:h + 1, :]
            term = jnp.where(hit, g1_ref[a, h:h + 1, :] * e2_ref[h], 0.0)
            w = term if w is None else w + term
        parts.append((w * gel[a * PEER_NKEYS:(a + 1) * PEER_NKEYS, :]).astype(BF16))
    ht = jnp.concatenate(parts, axis=0)
    o_ref[...] += jnp.dot(ht.T, v_ref[...], preferred_element_type=F32)


def peer_mix(hn, u_tab, v_tab, s1_t, g1_t, s2, e2, tau, mt, et):
    m = hn.shape[0]
    n_exp = u_tab.shape[0]
    a_per = et // PEER_NKEYS
    sel1 = pl.BlockSpec((a_per, PEER_HEADS, mt), lambda i, e: (e, 0, i))
    sel2 = pl.BlockSpec((PEER_HEADS, PEER_NKEYS, mt), lambda i, e: (0, 0, i))
    tab = pl.BlockSpec((et, D_MODEL), lambda i, e: (e, 0))
    return pl.pallas_call(
        functools.partial(_peer_mix_kernel, et=et),
        grid=(m // mt, n_exp // et),
        in_specs=[pl.BlockSpec((mt, D_MODEL), lambda i, e: (i, 0)), tab, tab, sel1, sel1, sel2, sel2,
                  pl.BlockSpec((PEER_HEADS, mt), lambda i, e: (0, i))],
        out_specs=pl.BlockSpec((mt, D_MODEL), lambda i, e: (i, 0)),
        out_shape=jax.ShapeDtypeStruct((m, D_MODEL), F32),
        compiler_params=_cparams("parallel", "arbitrary"),
        name="peer_mix",
    )(hn, u_tab, v_tab, s1_t, g1_t, s2, e2, tau)


def _prep_weights(w_in, w_mem_kv, w_out, peer_w_q, peer_sub_keys, peer_u, peer_v):
    w_big = jnp.concatenate([w_in[:, :_OFF_B], w_in[:, _OFF_FQ:_OFF_FF], w_in[:, _OFF_MQ:]], axis=1).astype(BF16)
    w_small = jnp.concatenate(
        [w_in[:, _OFF_B:_OFF_FQ], w_in[:, _OFF_FF:_OFF_MQ],
         jnp.zeros((D_MODEL, SMALL_COLS - 2 * DN_HEADS - FOX_HEADS), F32)], axis=1).astype(BF16)
    half = PEER_DKEY // 2
    pairs = 2 * PEER_HEADS
    sk = peer_sub_keys.reshape(pairs, PEER_NKEYS, half)
    eye = jnp.eye(pairs, dtype=F32)
    keys_t = (eye[:, None, :, None] * sk[:, :, None, :]).reshape(pairs * PEER_NKEYS, pairs * half).astype(BF16)
    return dict(w_big=w_big, w_small=w_small, w_mem_kv=w_mem_kv.astype(BF16), w_out=w_out.astype(BF16),
                peer_w_q=peer_w_q.astype(BF16), keys_t=keys_t, peer_u=peer_u.astype(BF16),
                peer_v=peer_v.astype(BF16))


def _tile(m, pref):
    return pref if m % pref == 0 else m


def _project(x2, ln_g, wts):
    m = x2.shape[0]
    xn = rmsnorm_rows(x2, ln_g, _tile(m, 256))
    tm = _tile(m, 1024)
    p = matmul([xn], wts['w_big'], tm, 512)
    ps = matmul([xn], wts['w_small'], tm, SMALL_COLS)
    return p, ps


def _channel_mix(x2, o_dn, o_fox, o_mem, wts, ln_ffn_g):
    m = x2.shape[0]
    h = matmul([o_dn, o_fox, o_mem], wts['w_out'], _tile(m, 1024), 512, residual=x2)
    hn = rmsnorm_rows(h, ln_ffn_g, _tile(m, 256))
    s1, g1, s2, e2, tau = peer_stats(hn, wts['peer_w_q'], wts['keys_t'], _tile(m, 256))
    s1_t = jnp.swapaxes(s1, 0, 1)
    g1_t = jnp.swapaxes(g1, 0, 1)
    ffn = peer_mix(hn, wts['peer_u'], wts['peer_v'], s1_t, g1_t, s2, e2, tau, _tile(m, 512), 256)
    return h + ffn


def kernel(x_prompt, x_sample, cache_fox_k, cache_fox_v, cache_fox_logf, state_delta, state_conv, cache_mem_k, cache_mem_v, page_table, mem_prompt, ln_mix_g, w_in, conv_w, dn_a_log, dn_dt_bias, dn_norm_g, fox_f_bias, fox_qn_g, fox_kn_g, ln_mem_g, w_mem_kv, mem_qn_g, mem_kn_g, w_out, ln_ffn_g, peer_w_q, peer_sub_keys, peer_u, peer_v):
    depth = w_in.shape[0]
    assert depth == 1
    l = 0
    batch, seq, _ = x_prompt.shape
    nb = x_sample.shape[0]
    m_p = batch * seq
    wts = _prep_weights(w_in[l], w_mem_kv[l], w_out[l], peer_w_q[l], peer_sub_keys[l], peer_u[l], peer_v[l])

    x2 = x_prompt.reshape(m_p, D_MODEL)
    p, ps = _project(x2, ln_mix_g[l], wts)
    gt = gates(ps, dn_a_log[l], dn_dt_bias[l], fox_f_bias[l], 1024)
    beta = gt[:, :DN_HEADS]
    gdec = gt[:, DN_HEADS:2 * DN_HEADS]
    logf = gt[:, 2 * DN_HEADS:2 * DN_HEADS + FOX_HEADS]

    def to_chunks(a):
        return a.reshape(batch, seq, DN_HEADS).transpose(0, 2, 1).reshape(batch * DN_HEADS, seq // DN_CHUNK, DN_CHUNK)

    o_dn, dn_state = dn_prompt(p, conv_w[l], to_chunks(gdec), to_chunks(beta), dn_norm_g[l], batch, seq, 512)

    fqn = head_rmsnorm(p, PB_FQ, fox_qn_g[l], FOX_HEADS, FOX_HD, 512, out_dtype=BF16)
    fkn = head_rmsnorm(p, PB_FK, fox_kn_g[l], FOX_HEADS, FOX_HD, 512)
    fv = p[:, PB_FV * FOX_W:(PB_FV + 1) * FOX_W]
    lf_rows = logf.reshape(batch, seq, FOX_HEADS).transpose(0, 2, 1).reshape(batch * FOX_HEADS * (seq // LANE), LANE)
    c_row = cumsum_time(lf_rows, seq // LANE).reshape(batch * FOX_HEADS, 1, seq)
    o_fox = fox_prompt(fqn, fkn.astype(BF16), fv.astype(BF16), c_row, batch, seq, 256, 256)

    mem2 = mem_prompt.reshape(batch * MEM_TOKENS, D_MODEL)
    memn = rmsnorm_rows(mem2, ln_mem_g[l], 256)
    mkv = matmul([memn], wts['w_mem_kv'], batch * MEM_TOKENS, 512)
    mk = head_rmsnorm(mkv, 0, mem_kn_g[l], MEM_HEADS, MEM_HD, 256)
    mv = mkv[:, MEM_W:]
    o_mem = mem_attend_prompt(p, mk.astype(BF16), mv.astype(BF16), mem_qn_g[l], batch, seq, 512)

    y_p = _channel_mix(x2, o_dn, o_fox, o_mem, wts, ln_ffn_g[l])

    xs = x_sample.reshape(nb, D_MODEL)
    sp, sps = _project(xs, ln_mix_g[l], wts)
    sgt = gates(sps, dn_a_log[l], dn_dt_bias[l], fox_f_bias[l], nb)
    s_beta = sgt[:, :DN_HEADS].T.reshape(DN_HEADS, nb, 1)
    s_g = sgt[:, DN_HEADS:2 * DN_HEADS].T.reshape(DN_HEADS, nb, 1)
    s_logf = sgt[:, 2 * DN_HEADS:2 * DN_HEADS + FOX_HEADS]
    so_dn, s_state = dn_sample(sp, state_conv[l], conv_w[l], s_g, s_beta, dn_norm_g[l], state_delta[l], 8)
    conv_s = jnp.concatenate([state_conv[l][:, 1:, :], sp[:, None, :CONV_CH]], axis=1)

    sfq = head_rmsnorm(sp, PB_FQ, fox_qn_g[l], FOX_HEADS, FOX_HD, nb)
    sfk = head_rmsnorm(sp, PB_FK, fox_kn_g[l], FOX_HEADS, FOX_HD, nb)
    sfv = sp[:, PB_FV * FOX_W:(PB_FV + 1) * FOX_W]
    n_pool = cache_fox_k.shape[1]
    so_fox = fox_sample(sfq.reshape(nb, 1, FOX_W), sfk.reshape(nb, 1, FOX_W), sfv.reshape(nb, 1, FOX_W),
                        s_logf.reshape(nb, 1, FOX_HEADS),
                        cache_fox_k[l].reshape(n_pool, PAGE_SIZE, FOX_W),
                        cache_fox_v[l].reshape(n_pool, PAGE_SIZE, FOX_W),
                        cache_fox_logf[l], page_table, 4).reshape(nb, FOX_W)
    smq = sp[:, PB_MQ * MEM_W:(PB_MQ + 1) * MEM_W].reshape(nb, 1, MEM_W)
    so_mem = mem_sample(smq, cache_mem_k[l].reshape(nb, MEM_TOKENS, MEM_W),
                        cache_mem_v[l].reshape(nb, MEM_TOKENS, MEM_W), mem_qn_g[l]).reshape(nb, MEM_W)
    y_s = _channel_mix(xs, so_dn, so_fox, so_mem, wts, ln_ffn_g[l])

    return (
        y_p.reshape(batch, seq, D_MODEL),
        y_s.reshape(nb, 1, D_MODEL),
        fkn.reshape(1, batch, seq, FOX_HEADS, FOX_HD),
        fv.reshape(1, batch, seq, FOX_HEADS, FOX_HD),
        logf.reshape(1, batch, seq, FOX_HEADS),
        dn_state.reshape(1, batch, DN_HEADS, DN_D, DN_D),
        p.reshape(batch, seq, P_COLS)[:, seq - (DN_CONV - 1):, :CONV_CH][None],
        mk.reshape(1, batch, MEM_TOKENS, MEM_HEADS, MEM_HD),
        mv.reshape(1, batch, MEM_TOKENS, MEM_HEADS, MEM_HD),
        sfk.reshape(1, nb, 1, FOX_HEADS, FOX_HD),
        sfv.reshape(1, nb, 1, FOX_HEADS, FOX_HD),
        s_logf.reshape(1, nb, 1, FOX_HEADS),
        s_state[None],
        conv_s[None],
    )
```

```python
import functools

import jax
import jax.numpy as jnp
from jax import lax
from jax.experimental import pallas as pl
from jax.experimental.pallas import tpu as pltpu

F32 = jnp.float32
BF16 = jnp.bfloat16
HIGHEST = lax.Precision.HIGHEST
EPS = 1e-6
NEG_INF = float("-inf")

D_MODEL = 4096
DN_HEADS = 16
DN_D = 128
DN_CONV = 4
DN_CHUNK = 64
FOX_HEADS = 8
FOX_HD = 128
MEM_TOKENS = 256
MEM_HEADS = 4
MEM_HD = 256
PEER_HEADS = 8
PEER_NKEYS = 128
PEER_DKEY = 128
PEER_TOPK = 16
PAGE_SIZE = 128

DN_QK = DN_HEADS * DN_D
CONV_CH = 3 * DN_QK
FOX_W = FOX_HEADS * FOX_HD
MEM_W = MEM_HEADS * MEM_HD
_OFF_Z = CONV_CH
_OFF_B = _OFF_Z + DN_QK
_OFF_A = _OFF_B + DN_HEADS
_OFF_FQ = _OFF_A + DN_HEADS
_OFF_FK = _OFF_FQ + FOX_W
_OFF_FV = _OFF_FK + FOX_W
_OFF_FF = _OFF_FV + FOX_W
_OFF_MQ = _OFF_FF + FOX_HEADS
_IN_COLS = _OFF_MQ + MEM_W
P_COLS = CONV_CH + DN_QK + 3 * FOX_W + MEM_W
PB_FQ, PB_FK, PB_FV, PB_MQ = 8, 9, 10, 11
LANE = 128
SMALL_COLS = LANE

VMEM_LIMIT_BYTES = 56 * 1024 * 1024


def _cparams(*sem):
    return pltpu.CompilerParams(dimension_semantics=sem, vmem_limit_bytes=VMEM_LIMIT_BYTES)


def _sigmoid(x):
    return 1.0 / (1.0 + jnp.exp(-x))


def _softplus(x):
    return jnp.maximum(x, 0.0) + jnp.log1p(jnp.exp(-jnp.abs(x)))


def _dot_hi(a, b):
    return jnp.dot(a, b, precision=HIGHEST, preferred_element_type=F32)


def _split_bf16(x):
    hi = x.astype(BF16)
    return hi, (x - hi.astype(F32)).astype(BF16)


def _dot_split(a, b):
    (ah, al), (bh, bl) = a, b
    small = jnp.dot(ah, bl, preferred_element_type=F32) + jnp.dot(al, bh, preferred_element_type=F32)
    return small + jnp.dot(ah, bh, preferred_element_type=F32)


def _dot_nt(a, b):
    return lax.dot_general(a, b, (((1,), (1,)), ((), ())), preferred_element_type=F32)


def _row_to_col(row, n):
    a = lax.broadcasted_iota(jnp.int32, (n, n), 0)
    b = lax.broadcasted_iota(jnp.int32, (n, n), 1)
    return jnp.sum(jnp.where(a == b, row, 0.0), axis=1, keepdims=True)


def _rmsnorm_kernel(x_ref, g_ref, o_ref):
    x = x_ref[...]
    y = x * lax.rsqrt(jnp.mean(x * x, axis=-1, keepdims=True) + EPS) * g_ref[...]
    o_ref[...] = y.astype(o_ref.dtype)


def rmsnorm_rows(x, g, tm, out_dtype=BF16):
    m, k = x.shape
    return pl.pallas_call(
        _rmsnorm_kernel,
        grid=(m // tm,),
        in_specs=[pl.BlockSpec((tm, k), lambda i: (i, 0)), pl.BlockSpec((1, k), lambda i: (0, 0))],
        out_specs=pl.BlockSpec((tm, k), lambda i: (i, 0)),
        out_shape=jax.ShapeDtypeStruct((m, k), out_dtype),
        compiler_params=_cparams("parallel"),
        name="rmsnorm_rows",
    )(x, g.reshape(1, k))


def _head_rmsnorm_kernel(x_ref, g_ref, o_ref, *, heads, hd):
    g = g_ref[...]
    for h in range(heads):
        x = x_ref[:, h * hd:(h + 1) * hd]
        y = x * lax.rsqrt(jnp.mean(x * x, axis=-1, keepdims=True) + EPS) * g
        o_ref[:, h * hd:(h + 1) * hd] = y.astype(o_ref.dtype)


def head_rmsnorm(x, col_block, g, heads, hd, tm, out_dtype=F32):
    m = x.shape[0]
    w = heads * hd
    return pl.pallas_call(
        functools.partial(_head_rmsnorm_kernel, heads=heads, hd=hd),
        grid=(m // tm,),
        in_specs=[pl.BlockSpec((tm, w), lambda i: (i, col_block)), pl.BlockSpec((1, hd), lambda i: (0, 0))],
        out_specs=pl.BlockSpec((tm, w), lambda i: (i, 0)),
        out_shape=jax.ShapeDtypeStruct((m, w), out_dtype),
        compiler_params=_cparams("parallel"),
        name="head_rmsnorm",
    )(x, g.reshape(1, hd))


def _matmul_kernel(*refs, ksizes, has_res):
    n_a = len(ksizes)
    w_ref = refs[n_a]
    o_ref = refs[-1]
    acc = None
    off = 0
    for a_ref, ks in zip(refs[:n_a], ksizes):
        part = jnp.dot(a_ref[...], w_ref[off:off + ks, :], preferred_element_type=F32)
        acc = part if acc is None else acc + part
        off += ks
    if has_res:
        acc = acc + refs[n_a + 1][...]
    o_ref[...] = acc


def matmul(a_list, w, tm, tn, residual=None):
    m = a_list[0].shape[0]
    k, n = w.shape
    ksizes = tuple(a.shape[1] for a in a_list)
    assert sum(ksizes) == k and m % tm == 0 and n % tn == 0
    in_specs = [pl.BlockSpec((tm, ks), lambda i, j: (i, 0)) for ks in ksizes]
    in_specs.append(pl.BlockSpec((k, tn), lambda i, j: (0, j)))
    args = list(a_list) + [w]
    if residual is not None:
        in_specs.append(pl.BlockSpec((tm, tn), lambda i, j: (i, j)))
        args.append(residual)
    return pl.pallas_call(
        functools.partial(_matmul_kernel, ksizes=ksizes, has_res=residual is not None),
        grid=(m // tm, n // tn),
        in_specs=in_specs,
        out_specs=pl.BlockSpec((tm, tn), lambda i, j: (i, j)),
        out_shape=jax.ShapeDtypeStruct((m, n), F32),
        compiler_params=_cparams("parallel", "parallel"),
        name="matmul",
    )(*args)


def _gates_kernel(p_ref, alog_ref, dtb_ref, fb_ref, o_ref):
    x = p_ref[...]
    lane = lax.broadcasted_iota(jnp.int32, x.shape, 1)
    beta = _sigmoid(x)
    g = -jnp.exp(alog_ref[...]) * _softplus(x + dtb_ref[...])
    logf = -_softplus(-(x + fb_ref[...]))
    out = jnp.where(lane < DN_HEADS, beta,
                    jnp.where(lane < 2 * DN_HEADS, g,
                              jnp.where(lane < 2 * DN_HEADS + FOX_HEADS, logf, 0.0)))
    o_ref[...] = out


def gates(p_small, a_log, dt_bias, f_bias, tm):
    m = p_small.shape[0]

    def pad(v, off):
        return jnp.zeros((1, SMALL_COLS), F32).at[0, off:off + v.shape[0]].set(v)

    row = pl.BlockSpec((1, SMALL_COLS), lambda i: (0, 0))
    return pl.pallas_call(
        _gates_kernel,
        grid=(m // tm,),
        in_specs=[pl.BlockSpec((tm, SMALL_COLS), lambda i: (i, 0)), row, row, row],
        out_specs=pl.BlockSpec((tm, SMALL_COLS), lambda i: (i, 0)),
        out_shape=jax.ShapeDtypeStruct((m, SMALL_COLS), F32),
        compiler_params=_cparams("parallel"),
        name="gates",
    )(p_small, pad(a_log, DN_HEADS), pad(dt_bias, DN_HEADS), pad(f_bias, 2 * DN_HEADS))


def _cumsum_kernel(x_ref, o_ref, *, blocks_per_group):
    x = x_ref[...]
    r, n = x.shape
    a = lax.broadcasted_iota(jnp.int32, (n, n), 0)
    b = lax.broadcasted_iota(jnp.int32, (n, n), 1)
    local = _dot_hi(x, (a <= b).astype(F32))
    tot = jnp.broadcast_to(local[:, n - 1:n], (r, n))
    ra = lax.broadcasted_iota(jnp.int32, (r, r), 0)
    rb = lax.broadcasted_iota(jnp.int32, (r, r), 1)
    earlier = jnp.logical_and(rb < ra, (ra // blocks_per_group) == (rb // blocks_per_group)).astype(F32)
    o_ref[...] = local + _dot_hi(earlier, tot)


def cumsum_time(x, blocks_per_group):
    return pl.pallas_call(
        functools.partial(_cumsum_kernel, blocks_per_group=blocks_per_group),
        out_shape=jax.ShapeDtypeStruct(x.shape, F32),
        compiler_params=pltpu.CompilerParams(vmem_limit_bytes=VMEM_LIMIT_BYTES),
        name="cumsum_time",
    )(x)


def _fox_prompt_kernel(q_ref, k_ref, v_ref, c_ref, o_ref, *, tq, tk, scale):
    i = pl.program_id(2)
    q = q_ref[...]
    q0 = pl.multiple_of(i * tq, tq)
    cq = _row_to_col(c_ref[0, :, pl.ds(q0, tq)], tq)
    qpos = q0 + lax.broadcasted_iota(jnp.int32, (tq, 1), 0)
    kiota = lax.broadcasted_iota(jnp.int32, (1, tk), 1)

    def body(j, carry):
        m, l, acc = carry
        k0 = pl.multiple_of(j * tk, tk)
        k = k_ref[pl.ds(k0, tk), :]
        v = v_ref[pl.ds(k0, tk), :]
        s = _dot_nt(q, k) * scale
        s = s + (cq - c_ref[0, :, pl.ds(k0, tk)])
        s = jnp.where(k0 + kiota <= qpos, s, NEG_INF)
        m_new = jnp.maximum(m, jnp.max(s, axis=1, keepdims=True))
        alpha = jnp.exp(m - m_new)
        p = jnp.exp(s - m_new)
        l = l * alpha + jnp.sum(p, axis=1, keepdims=True)
        acc = acc * alpha + jnp.dot(p.astype(BF16), v, preferred_element_type=F32)
        return m_new, l, acc

    n_kv = (i + 1) * (tq // tk)
    init = (jnp.full((tq, 1), NEG_INF, F32), jnp.zeros((tq, 1), F32), jnp.zeros((tq, FOX_HD), F32))
    _, l, acc = lax.fori_loop(0, n_kv, body, init)
    o_ref[...] = (acc / l).astype(o_ref.dtype)


def fox_prompt(qn, kn, v, c_row, batch, seq, tq, tk):
    nq = seq // tq
    kv_spec = pl.BlockSpec((seq, FOX_HD), lambda b, h, i: (b, h))
    return pl.pallas_call(
        functools.partial(_fox_prompt_kernel, tq=tq, tk=tk, scale=FOX_HD ** -0.5),
        grid=(batch, FOX_HEADS, nq),
        in_specs=[pl.BlockSpec((tq, FOX_HD), lambda b, h, i: (b * nq + i, h)), kv_spec, kv_spec,
                  pl.BlockSpec((1, 1, seq), lambda b, h, i: (b * FOX_HEADS + h, 0, 0))],
        out_specs=pl.BlockSpec((tq, FOX_HD), lambda b, h, i: (b * nq + i, h)),
        out_shape=jax.ShapeDtypeStruct((batch * seq, FOX_W), BF16),
        compiler_params=_cparams("parallel", "parallel", "arbitrary"),
        name="fox_prompt",
    )(qn, kn, v, c_row)


def _dn_prompt_kernel(q_ref, k_ref, v_ref, z_ref, wq_ref, wk_ref, wv_ref, g_ref, b_ref, ng_ref,
                      o_ref, s_out_ref, s_ref, xq_ref, xk_ref, xv_ref, *, tc, chunk, hg):
    t = pl.program_id(2)
    nc = tc // chunk
    pad = 8

    @pl.when(t == 0)
    def _():
        s_ref[...] = jnp.zeros_like(s_ref)
        for buf in (xq_ref, xk_ref, xv_ref):
            buf[0:pad, :] = jnp.zeros((pad, hg * DN_D), F32)

    def conv_silu(x_ref, w_ref, buf_ref):
        u = x_ref[...]
        buf_ref[pad:pad + tc, :] = u
        w = w_ref[...]
        out = buf_ref[pad - 3:pad - 3 + tc, :] * w[0:1, :]
        out = out + buf_ref[pad - 2:pad - 2 + tc, :] * w[1:2, :]
        out = out + buf_ref[pad - 1:pad - 1 + tc, :] * w[2:3, :]
        out = out + u * w[3:4, :]
        buf_ref[0:pad, :] = u[tc - pad:tc, :]
        return out * _sigmoid(out)

    def l2norm(x):
        return x * lax.rsqrt(jnp.sum(x * x, axis=-1, keepdims=True) + EPS)

    qc = conv_silu(q_ref, wq_ref, xq_ref)
    kc = conv_silu(k_ref, wk_ref, xk_ref)
    vc = conv_silu(v_ref, wv_ref, xv_ref)

    ra = lax.broadcasted_iota(jnp.int32, (chunk, chunk), 0)
    rb = lax.broadcasted_iota(jnp.int32, (chunk, chunk), 1)
    tri = ra >= rb
    strict = ra > rb
    eye_f = (ra == rb).astype(F32)
    upper_f = (ra <= rb).astype(F32)

    heads = []
    for hh in range(hg):
        cs = slice(hh * DN_D, (hh + 1) * DN_D)
        heads.append(dict(
            q=l2norm(qc[:, cs]) * (DN_D ** -0.5), k=l2norm(kc[:, cs]), v=vc[:, cs],
            gc=_dot_hi(g_ref[hh], upper_f),
            beta=b_ref[hh], s=s_ref[hh], outs=[]))

    work = []
    for c in range(nc):
        sl = slice(c * chunk, (c + 1) * chunk)
        for hd in heads:
            q, k, v = hd['q'][sl], hd['k'][sl], hd['v'][sl]
            gc_row = hd['gc'][c:c + 1, :]
            gc = _row_to_col(gc_row, chunk)
            beta = _row_to_col(hd['beta'][c:c + 1, :], chunk)
            decay = jnp.where(tri, jnp.exp(jnp.where(tri, gc - gc_row, 0.0)), 0.0)
            qk_b = jnp.concatenate([q, k], axis=0).astype(BF16)
            work.append(dict(hd=hd, q=q, k=k, v=v, gc=gc, beta=beta, decay=decay, qk_b=qk_b,
                             gc_last=gc_row[:, chunk - 1:chunk]))
    for wk in work:
        wk['gram'] = _dot_nt(wk['qk_b'], wk['qk_b'][chunk:, :])
    for wk in work:
        lower = jnp.where(strict, wk['beta'] * wk['gram'][chunk:, :] * wk['decay'], 0.0)
        wk['inv'] = eye_f - lower
        wk['power'] = _split_bf16(lower)
    for wk in work:
        wk['power'] = _split_bf16(_dot_split(wk['power'], wk['power']))
    span = 2
    while span < chunk:
        for wk in work:
            wk['inv'] = wk['inv'] + _dot_split(_split_bf16(wk['inv']), wk['power'])
        span *= 2
        if span < chunk:
            for wk in work:
                wk['power'] = _split_bf16(_dot_split(wk['power'], wk['power']))
    for wk in work:
        egc = jnp.exp(wk['gc'])
        k, beta = wk['k'], wk['beta']
        uw = _dot_split(_split_bf16(wk['inv']),
                        _split_bf16(jnp.concatenate([wk['v'] * beta, k * (beta * egc)], axis=1)))
        wk['u'] = uw[:, :DN_D]
        wk['lhs1'] = jnp.concatenate([uw[:, DN_D:], wk['q'] * egc], axis=0).astype(BF16)
        qk = jnp.where(tri, wk['gram'][:chunk, :] * wk['decay'], 0.0)
        k_dec = k * jnp.exp(wk['gc_last'] - wk['gc'])
        wk['lhs2'] = jnp.concatenate([qk, k_dec.T], axis=0).astype(BF16)
        wk['g_last'] = jnp.exp(wk['gc_last'])
    for wk in work:
        hd = wk['hd']
        s = hd['s']
        ws = jnp.dot(wk['lhs1'], s.astype(BF16), preferred_element_type=F32)
        v_new = wk['u'] - ws[:chunk, :]
        upd = jnp.dot(wk['lhs2'], v_new.astype(BF16), preferred_element_type=F32)
        hd['outs'].append(ws[chunk:, :] + upd[:chunk, :])
        hd['s'] = s * wk['g_last'] + upd[chunk:, :]

    ng = ng_ref[...]
    for hh, hd in enumerate(heads):
        cs = slice(hh * DN_D, (hh + 1) * DN_D)
        s_ref[hh] = hd['s']
        s_out_ref[hh] = hd['s']
        o = jnp.concatenate(hd['outs'], axis=0)
        o = o * lax.rsqrt(jnp.mean(o * o, axis=-1, keepdims=True) + EPS) * ng
        z = z_ref[:, cs]
        o_ref[:, cs] = (o * (z * _sigmoid(z))).astype(o_ref.dtype)


def dn_prompt(p, conv_w, g_chunks, beta_chunks, norm_g, batch, seq, tc, hg):
    nt = seq // tc
    nc = tc // DN_CHUNK
    ng = DN_HEADS // hg
    w = hg * DN_D

    def col(off):
        return pl.BlockSpec((tc, w), lambda b, h, t: (b * nt + t, off * ng + h))

    def wcol(off):
        return pl.BlockSpec((DN_CONV, w), lambda b, h, t: (0, off * ng + h))

    gspec = pl.BlockSpec((hg, nc, DN_CHUNK), lambda b, h, t: (b * ng + h, t, 0))
    return pl.pallas_call(
        functools.partial(_dn_prompt_kernel, tc=tc, chunk=DN_CHUNK, hg=hg),
        grid=(batch, ng, nt),
        in_specs=[col(0), col(1), col(2), col(3), wcol(0), wcol(1), wcol(2), gspec, gspec,
                  pl.BlockSpec((1, DN_D), lambda b, h, t: (0, 0))],
        out_specs=[pl.BlockSpec((tc, w), lambda b, h, t: (b * nt + t, h)),
                   pl.BlockSpec((hg, DN_D, DN_D), lambda b, h, t: (b * ng + h, 0, 0))],
        out_shape=[jax.ShapeDtypeStruct((batch * seq, DN_QK), BF16),
                   jax.ShapeDtypeStruct((batch * DN_HEADS, DN_D, DN_D), F32)],
        scratch_shapes=[pltpu.VMEM((hg, DN_D, DN_D), F32)] + [pltpu.VMEM((tc + 8, w), F32)] * 3,
        compiler_params=_cparams("parallel", "parallel", "arbitrary"),
        name="dn_prompt",
    )(p, p, p, p, conv_w, conv_w, conv_w, g_chunks, beta_chunks, norm_g.reshape(1, DN_D))


def _dn_sample_kernel(q_ref, k_ref, v_ref, z_ref, cq_ref, ck_ref, cv_ref, wq_ref, wk_ref, wv_ref,
                      g_ref, b_ref, ng_ref, s_ref, o_ref, s_out_ref, *, bb):
    def conv_silu(x_ref, c_ref, w_ref):
        w = w_ref[...]
        out = c_ref[:, 0, :] * w[0:1, :]
        out = out + c_ref[:, 1, :] * w[1:2, :]
        out = out + c_ref[:, 2, :] * w[2:3, :]
        out = out + x_ref[...] * w[3:4, :]
        return out * _sigmoid(out)

    def l2norm(x):
        return x * lax.rsqrt(jnp.sum(x * x, axis=-1, keepdims=True) + EPS)

    q = l2norm(conv_silu(q_ref, cq_ref, wq_ref)) * (DN_D ** -0.5)
    k = l2norm(conv_silu(k_ref, ck_ref, wk_ref))
    v = conv_silu(v_ref, cv_ref, wv_ref)
    q_t = q.T
    k_t = k.T
    decay = jnp.exp(g_ref[0])
    beta = b_ref[0]
    rows = []
    for b in range(bb):
        s = s_ref[b, 0] * decay[b:b + 1, :]
        k_col = k_t[:, b:b + 1]
        kv = jnp.sum(k_col * s, axis=0, keepdims=True)
        s = s + k_col * ((v[b:b + 1, :] - kv) * beta[b:b + 1, :])
        s_out_ref[b, 0] = s
        rows.append(jnp.sum(q_t[:, b:b + 1] * s, axis=0, keepdims=True))
    o = jnp.concatenate(rows, axis=0)
    o = o * lax.rsqrt(jnp.mean(o * o, axis=-1, keepdims=True) + EPS) * ng_ref[...]
    z = z_ref[...]
    o_ref[...] = (o * (z * _sigmoid(z))).astype(o_ref.dtype)


def dn_sample(p, state_conv, conv_w, g_t, beta_t, norm_g, state, bb):
    nb = p.shape[0]
    hb = DN_HEADS

    def col(off):
        return pl.BlockSpec((bb, DN_D), lambda i, h: (i, off + h))

    def ccol(off):
        return pl.BlockSpec((bb, DN_CONV - 1, DN_D), lambda i, h: (i, 0, off + h))

    def wcol(off):
        return pl.BlockSpec((DN_CONV, DN_D), lambda i, h: (0, off + h))

    gspec = pl.BlockSpec((1, bb, 1), lambda i, h: (h, i, 0))
    sspec = pl.BlockSpec((bb, 1, DN_D, DN_D), lambda i, h: (i, h, 0, 0))
    return pl.pallas_call(
        functools.partial(_dn_sample_kernel, bb=bb),
        grid=(nb // bb, DN_HEADS),
        in_specs=[col(0), col(hb), col(2 * hb), col(3 * hb), ccol(0), ccol(hb), ccol(2 * hb),
                  wcol(0), wcol(hb), wcol(2 * hb), gspec, gspec,
                  pl.BlockSpec((1, DN_D), lambda i, h: (0, 0)), sspec],
        out_specs=[pl.BlockSpec((bb, DN_D), lambda i, h: (i, h)), sspec],
        out_shape=[jax.ShapeDtypeStruct((nb, DN_QK), BF16), jax.ShapeDtypeStruct(state.shape, F32)],
        compiler_params=_cparams("parallel", "parallel"),
        name="dn_sample",
    )(p, p, p, p, state_conv, state_conv, state_conv, conv_w, conv_w, conv_w, g_t, beta_t,
      norm_g.reshape(1, DN_D), state)


def _fox_sample_kernel(pt_ref, q_ref, kn_ref, vn_ref, lfn_ref, *refs, pp, scale):
    k_refs = refs[0:pp]
    v_refs = refs[pp:2 * pp]
    lf_refs = refs[2 * pp:3 * pp]
    o_ref = refs[3 * pp]
    m_ref, l_ref, acc_ref, carry_ref = refs[3 * pp + 1:]
    j = pl.program_id(1)
    qs = q_ref[0] * scale

    @pl.when(j == 0)
    def _():
        m_ref[...] = jnp.sum(qs * kn_ref[0], axis=1, keepdims=True)
        l_ref[...] = jnp.ones_like(l_ref)
        acc_ref[...] = vn_ref[0]
        carry_ref[...] = lfn_ref[0]

    ra = lax.broadcasted_iota(jnp.int32, (PAGE_SIZE, PAGE_SIZE), 0)
    rb = lax.broadcasted_iota(jnp.int32, (PAGE_SIZE, PAGE_SIZE), 1)
    later_f = (ra > rb).astype(F32)
    shape3 = (PAGE_SIZE, FOX_HEADS, FOX_HD)
    diag3 = lax.broadcasted_iota(jnp.int32, shape3, 0) == lax.broadcasted_iota(jnp.int32, shape3, 2)
    carry = carry_ref[...]
    logits = []
    for i in range(pp):
        lf = lf_refs[i][0]
        bias = _dot_hi(lf, later_f) + carry
        carry = carry + jnp.sum(lf, axis=1, keepdims=True)
        logits.append(jnp.sum(k_refs[i][0] * qs[None] + jnp.where(diag3, bias[None], 0.0),
                              axis=2, keepdims=True))
    carry_ref[...] = carry
    m_old = m_ref[...]
    m_new = m_old
    for s3 in logits:
        m_new = jnp.maximum(m_new, jnp.max(s3, axis=0))
    alpha = jnp.exp(m_old - m_new)
    l_new = l_ref[...] * alpha
    acc = acc_ref[...] * alpha
    for i, s3 in enumerate(logits):
        p3 = jnp.exp(s3 - m_new[None])
        l_new = l_new + jnp.sum(p3, axis=0)
        acc = acc + jnp.sum(p3 * v_refs[i][0], axis=0)
    l_ref[...] = l_new
    acc_ref[...] = acc
    m_ref[...] = m_new

    @pl.when(j == pl.num_programs(1) - 1)
    def _():
        o_ref[0] = (acc_ref[...] / l_ref[...]).astype(o_ref.dtype)


def fox_sample(q, k_new, v_new, lf_new, k_pool, v_pool, lf_pool_t, page_table, pp):
    nb, n_pages = page_table.shape
    assert n_pages % pp == 0
    steps = n_pages // pp

    def page_idx4(i):
        return lambda b, j, pt: (pt[b, n_pages - 1 - (j * pp + i)], 0, 0, 0)

    def page_idx3(i):
        return lambda b, j, pt: (pt[b, n_pages - 1 - (j * pp + i)], 0, 0)

    tok = lambda b, j, pt: (b, 0, 0)
    in_specs = [pl.BlockSpec((1, FOX_HEADS, FOX_HD), tok)] * 3 + [pl.BlockSpec((1, FOX_HEADS, 1), tok)]
    in_specs += [pl.BlockSpec((1, PAGE_SIZE, FOX_HEADS, FOX_HD), page_idx4(i)) for i in range(pp)]
    in_specs += [pl.BlockSpec((1, PAGE_SIZE, FOX_HEADS, FOX_HD), page_idx4(i)) for i in range(pp)]
    in_specs += [pl.BlockSpec((1, FOX_HEADS, PAGE_SIZE), page_idx3(i)) for i in range(pp)]
    grid_spec = pltpu.PrefetchScalarGridSpec(
        num_scalar_prefetch=1,
        grid=(nb, steps),
        in_specs=in_specs,
        out_specs=pl.BlockSpec((1, FOX_HEADS, FOX_HD), tok),
        scratch_shapes=[pltpu.VMEM((FOX_HEADS, 1), F32), pltpu.VMEM((FOX_HEADS, 1), F32),
                        pltpu.VMEM((FOX_HEADS, FOX_HD), F32), pltpu.VMEM((FOX_HEADS, 1), F32)],
    )
    return pl.pallas_call(
        functools.partial(_fox_sample_kernel, pp=pp, scale=FOX_HD ** -0.5),
        grid_spec=grid_spec,
        out_shape=jax.ShapeDtypeStruct((nb, FOX_HEADS, FOX_HD), BF16),
        compiler_params=_cparams("parallel", "arbitrary"),
        name="fox_sample",
    )(page_table, q, k_new, v_new, lf_new, *([k_pool] * pp), *([v_pool] * pp), *([lf_pool_t] * pp))


def _mem_prompt_kernel(q_ref, k_ref, v_ref, g_ref, o_ref, *, scale):
    g = g_ref[...]
    for h in range(MEM_HEADS):
        cs = slice(h * MEM_HD, (h + 1) * MEM_HD)
        q = q_ref[:, cs]
        qn = (q * lax.rsqrt(jnp.mean(q * q, axis=-1, keepdims=True) + EPS) * g).astype(BF16)
        s = _dot_nt(qn, k_ref[:, cs]) * scale
        p = jnp.exp(s - jnp.max(s, axis=1, keepdims=True))
        p = p / jnp.sum(p, axis=1, keepdims=True)
        o_ref[:, cs] = jnp.dot(p.astype(BF16), v_ref[:, cs], preferred_element_type=F32).astype(o_ref.dtype)


def mem_attend_prompt(p, mk, mv, qn_g, batch, seq, tq):
    nq = seq // tq
    kv = pl.BlockSpec((MEM_TOKENS, MEM_W), lambda b, i: (b, 0))
    return pl.pallas_call(
        functools.partial(_mem_prompt_kernel, scale=MEM_HD ** -0.5),
        grid=(batch, nq),
        in_specs=[pl.BlockSpec((tq, MEM_W), lambda b, i: (b * nq + i, PB_MQ)), kv, kv,
                  pl.BlockSpec((1, MEM_HD), lambda b, i: (0, 0))],
        out_specs=pl.BlockSpec((tq, MEM_W), lambda b, i: (b * nq + i, 0)),
        out_shape=jax.ShapeDtypeStruct((batch * seq, MEM_W), BF16),
        compiler_params=_cparams("parallel", "parallel"),
        name="mem_prompt",
    )(p, mk, mv, qn_g.reshape(1, MEM_HD))


def _mem_sample_kernel(q_ref, k_ref, v_ref, g_ref, o_ref, *, scale):
    q = q_ref[0]
    qn = q * lax.rsqrt(jnp.mean(q * q, axis=-1, keepdims=True) + EPS) * g_ref[...] * scale
    s3 = jnp.sum(k_ref[0] * qn[None], axis=2, keepdims=True)
    p3 = jnp.exp(s3 - jnp.max(s3, axis=0)[None])
    o = jnp.sum(p3 * v_ref[0], axis=0) / jnp.sum(p3, axis=0)
    o_ref[0] = o.astype(o_ref.dtype)


def mem_sample(q, mk, mv, qn_g):
    nb = q.shape[0]
    tok = pl.BlockSpec((1, MEM_HEADS, MEM_HD), lambda b: (b, 0, 0))
    kv = pl.BlockSpec((1, MEM_TOKENS, MEM_HEADS, MEM_HD), lambda b: (b, 0, 0, 0))
    return pl.pallas_call(
        functools.partial(_mem_sample_kernel, scale=MEM_HD ** -0.5),
        grid=(nb,),
        in_specs=[tok, kv, kv, pl.BlockSpec((1, MEM_HD), lambda b: (0, 0))],
        out_specs=tok,
        out_shape=jax.ShapeDtypeStruct((nb, MEM_HEADS, MEM_HD), BF16),
        compiler_params=_cparams("parallel"),
        name="mem_sample",
    )(q, mk, mv, qn_g.reshape(1, MEM_HD))


def _top_values(s, count):
    rows = s.shape[0]
    idx = lax.broadcasted_iota(jnp.int32, s.shape, 0)
    vals = []
    for _ in range(count):
        m = jnp.max(s, axis=0, keepdims=True)
        first = jnp.min(jnp.where(s == m, idx, rows), axis=0, keepdims=True)
        s = jnp.where(idx == first, NEG_INF, s)
        vals.append(m)
    return jnp.concatenate(vals, axis=0)


def _peer_stats_kernel(hn_ref, wq_ref, keys_ref, s1_ref, g1_ref, s2_ref, e2_ref, tau_ref, st_ref, top_ref):
    nk = PEER_NKEYS
    kk = PEER_TOPK
    q = jnp.dot(hn_ref[...], wq_ref[...], preferred_element_type=F32)
    st_ref[...] = _dot_nt(keys_ref[...], q.astype(BF16))

    def half_body(c, carry):
        r0 = pl.multiple_of(c * nk, nk)
        t0 = pl.multiple_of(c * kk, kk)
        top_ref[pl.ds(t0, kk), :] = _top_values(st_ref[pl.ds(r0, nk), :], kk)
        return carry

    lax.fori_loop(0, 2 * PEER_HEADS, half_body, 0)

    def head_body(h, carry):
        t0 = pl.multiple_of(h * 2 * kk, 2 * kk)
        v1 = top_ref[pl.ds(t0, kk), :]
        v2 = top_ref[pl.ds(t0 + kk, kk), :]
        cand = jnp.concatenate([v1[0:1, :] + v2] + [v1[a:a + 1, :] + v2[0:8, :] for a in range(1, 8)]
                               + [v1[8:kk, :] + v2[0:1, :]], axis=0)
        cv = _top_values(cand, kk)
        z = jnp.sum(jnp.exp(cv - cv[0:1, :]), axis=0, keepdims=True)
        tau_ref[pl.ds(h, 1), :] = cv[kk - 1:kk, :]
        r0 = pl.multiple_of(h * 2 * nk, 2 * nk)
        s1 = st_ref[pl.ds(r0, nk), :]
        s2 = st_ref[pl.ds(r0 + nk, nk), :]
        s1_ref[h] = s1
        s2_ref[h] = s2
        g1_ref[h] = jnp.exp(s1 - v1[0:1, :]) / z
        e2_ref[h] = jnp.exp(s2 - v2[0:1, :])
        return carry

    lax.fori_loop(0, PEER_HEADS, head_body, 0)


def peer_stats(hn, w_q, keys_t, mt):
    m = hn.shape[0]
    nrow = PEER_HEADS * 2 * PEER_NKEYS
    hspec = pl.BlockSpec((PEER_HEADS, PEER_NKEYS, mt), lambda i: (0, 0, i))
    hshape = jax.ShapeDtypeStruct((PEER_HEADS, PEER_NKEYS, m), F32)
    return pl.pallas_call(
        _peer_stats_kernel,
        grid=(m // mt,),
        in_specs=[pl.BlockSpec((mt, D_MODEL), lambda i: (i, 0)),
                  pl.BlockSpec(w_q.shape, lambda i: (0, 0)),
                  pl.BlockSpec(keys_t.shape, lambda i: (0, 0))],
        out_specs=[hspec, hspec, hspec, hspec, pl.BlockSpec((PEER_HEADS, mt), lambda i: (0, i))],
        out_shape=[hshape, hshape, hshape, hshape, jax.ShapeDtypeStruct((PEER_HEADS, m), F32)],
        scratch_shapes=[pltpu.VMEM((nrow, mt), F32), pltpu.VMEM((2 * PEER_HEADS * PEER_TOPK, mt), F32)],
        compiler_params=_cparams("parallel"),
        name="peer_stats",
    )(hn, w_q, keys_t)


def _peer_gates(s1_ref, g1_ref, s2_ref, e2_ref, tau_ref, gate_ref, a_range):
    mt = gate_ref.shape[1]
    rows = 32
    for a in a_range:
        for c in range(mt // LANE):
            cs = slice(c * LANE, (c + 1) * LANE)
            for r in range(PEER_NKEYS // rows):
                rs = slice(r * rows, (r + 1) * rows)
                w = None
                for h in range(PEER_HEADS):
                    hit = (s1_ref[a, h:h + 1, cs] + s2_ref[h, rs, cs]) >= tau_ref[h:h + 1, cs]
                    term = jnp.where(hit, g1_ref[a, h:h + 1, cs] * e2_ref[h, rs, cs], 0.0)
                    w = term if w is None else w + term
                gate_ref[a * PEER_NKEYS + r * rows:a * PEER_NKEYS + (r + 1) * rows, cs] = w


def _peer_mix_kernel(hn_ref, u_ref, v_ref, s1_ref, g1_ref, s1n_ref, g1n_ref, s2_ref, e2_ref, tau_ref,
                     o_ref, gate_a_ref, gate_b_ref, *, et):
    e = pl.program_id(1)

    @pl.when(e == 0)
    def _():
        o_ref[...] = jnp.zeros_like(o_ref)
        _peer_gates(s1_ref, g1_ref, s2_ref, e2_ref, tau_ref, gate_a_ref, range(et // PEER_NKEYS))

    def step(gate_cur_ref, gate_next_ref):
        n_a = et // PEER_NKEYS
        mt = o_ref.shape[0]
        halves = 2 if mt % 512 == 0 else 1
        th = mt // halves
        pieces = []
        for j in range(halves):
            ts = slice(j * th, (j + 1) * th)
            act_t = _dot_nt(u_ref[...], hn_ref[ts, :])
            _peer_gates(s1n_ref, g1n_ref, s2_ref, e2_ref, tau_ref, gate_next_ref,
                        range(j * n_a // halves, (j + 1) * n_a // halves))
            gel_t = 0.5 * act_t * (1.0 + lax.erf(act_t * (0.5 ** 0.5)))
            pieces.append((gate_cur_ref[:, ts] * gel_t).astype(BF16).T)
        hmat = pieces[0] if halves == 1 else jnp.concatenate(pieces, axis=0)
        o_ref[...] += jnp.dot(hmat, v_ref[...], preferred_element_type=F32)

    @pl.when(e % 2 == 0)
    def _():
        step(gate_a_ref, gate_b_ref)

    @pl.when(e % 2 == 1)
    def _():
        step(gate_b_ref, gate_a_ref)


def peer_mix(hn, u_tab, v_tab, s1_t, g1_t, s2, e2, tau, mt, et):
    m = hn.shape[0]
    n_exp = u_tab.shape[0]
    n_e = n_exp // et
    a_per = et // PEER_NKEYS
    once = pl.Buffered(1)
    sel1 = pl.BlockSpec((a_per, PEER_HEADS, mt), lambda i, e: (e, 0, i))
    sel1_next = pl.BlockSpec((a_per, PEER_HEADS, mt), lambda i, e: (jnp.minimum(e + 1, n_e - 1), 0, i))
    sel2 = pl.BlockSpec((PEER_HEADS, PEER_NKEYS, mt), lambda i, e: (0, 0, i), pipeline_mode=once)
    tab = pl.BlockSpec((et, D_MODEL), lambda i, e: (e, 0))
    return pl.pallas_call(
        functools.partial(_peer_mix_kernel, et=et),
        grid=(m // mt, n_e),
        in_specs=[pl.BlockSpec((mt, D_MODEL), lambda i, e: (i, 0), pipeline_mode=once), tab, tab,
                  sel1, sel1, sel1_next, sel1_next, sel2, sel2,
                  pl.BlockSpec((PEER_HEADS, mt), lambda i, e: (0, i), pipeline_mode=once)],
        out_specs=pl.BlockSpec((mt, D_MODEL), lambda i, e: (i, 0)),
        out_shape=jax.ShapeDtypeStruct((m, D_MODEL), F32),
        scratch_shapes=[pltpu.VMEM((et, mt), F32), pltpu.VMEM((et, mt), F32)],
        compiler_params=_cparams("parallel", "arbitrary"),
        name="peer_mix",
    )(hn, u_tab, v_tab, s1_t, g1_t, s1_t, g1_t, s2, e2, tau)


def _prep_weights(w_in, w_mem_kv, w_out, peer_w_q, peer_sub_keys, peer_u, peer_v):
    w_big = jnp.concatenate([w_in[:, :_OFF_B], w_in[:, _OFF_FQ:_OFF_FF], w_in[:, _OFF_MQ:]], axis=1).astype(BF16)
    w_small = jnp.concatenate(
        [w_in[:, _OFF_B:_OFF_FQ], w_in[:, _OFF_FF:_OFF_MQ],
         jnp.zeros((D_MODEL, SMALL_COLS - 2 * DN_HEADS - FOX_HEADS), F32)], axis=1).astype(BF16)
    half = PEER_DKEY // 2
    pairs = 2 * PEER_HEADS
    sk = peer_sub_keys.reshape(pairs, PEER_NKEYS, half)
    eye = jnp.eye(pairs, dtype=F32)
    keys_t = (eye[:, None, :, None] * sk[:, :, None, :]).reshape(pairs * PEER_NKEYS, pairs * half).astype(BF16)
    return dict(w_big=w_big, w_small=w_small, w_mem_kv=w_mem_kv.astype(BF16), w_out=w_out.astype(BF16),
                peer_w_q=peer_w_q.astype(BF16), keys_t=keys_t, peer_u=peer_u.astype(BF16),
                peer_v=peer_v.astype(BF16))


def _tile(m, pref):
    return pref if m % pref == 0 else m


def _project(x2, ln_g, wts):
    m = x2.shape[0]
    xn = rmsnorm_rows(x2, ln_g, _tile(m, 256))
    tm = _tile(m, 1024)
    p = matmul([xn], wts['w_big'], tm, 512)
    ps = matmul([xn], wts['w_small'], tm, SMALL_COLS)
    return p, ps


def _channel_mix(x2, o_dn, o_fox, o_mem, wts, ln_ffn_g):
    m = x2.shape[0]
    h = matmul([o_dn, o_fox, o_mem], wts['w_out'], _tile(m, 1024), 512, residual=x2)
    hn = rmsnorm_rows(h, ln_ffn_g, _tile(m, 256))
    s1, g1, s2, e2, tau = peer_stats(hn, wts['peer_w_q'], wts['keys_t'], _tile(m, 256))
    s1_t = jnp.swapaxes(s1, 0, 1)
    g1_t = jnp.swapaxes(g1, 0, 1)
    ffn = peer_mix(hn, wts['peer_u'], wts['peer_v'], s1_t, g1_t, s2, e2, tau, _tile(m, 512), 512)
    return h + ffn


def kernel(x_prompt, x_sample, cache_fox_k, cache_fox_v, cache_fox_logf, state_delta, state_conv, cache_mem_k, cache_mem_v, page_table, mem_prompt, ln_mix_g, w_in, conv_w, dn_a_log, dn_dt_bias, dn_norm_g, fox_f_bias, fox_qn_g, fox_kn_g, ln_mem_g, w_mem_kv, mem_qn_g, mem_kn_g, w_out, ln_ffn_g, peer_w_q, peer_sub_keys, peer_u, peer_v):
    depth = w_in.shape[0]
    assert depth == 1
    l = 0
    batch, seq, _ = x_prompt.shape
    nb = x_sample.shape[0]
    m_p = batch * seq
    wts = _prep_weights(w_in[l], w_mem_kv[l], w_out[l], peer_w_q[l], peer_sub_keys[l], peer_u[l], peer_v[l])

    x2 = x_prompt.reshape(m_p, D_MODEL)
    p, ps = _project(x2, ln_mix_g[l], wts)
    gt = gates(ps, dn_a_log[l], dn_dt_bias[l], fox_f_bias[l], 1024)
    beta = gt[:, :DN_HEADS]
    gdec = gt[:, DN_HEADS:2 * DN_HEADS]
    logf = gt[:, 2 * DN_HEADS:2 * DN_HEADS + FOX_HEADS]

    def to_chunks(a):
        return a.reshape(batch, seq, DN_HEADS).transpose(0, 2, 1).reshape(batch * DN_HEADS, seq // DN_CHUNK, DN_CHUNK)

    o_dn, dn_state = dn_prompt(p, conv_w[l], to_chunks(gdec), to_chunks(beta), dn_norm_g[l], batch, seq, 512, 2)

    fqn = head_rmsnorm(p, PB_FQ, fox_qn_g[l], FOX_HEADS, FOX_HD, 512, out_dtype=BF16)
    fkn = head_rmsnorm(p, PB_FK, fox_kn_g[l], FOX_HEADS, FOX_HD, 512)
    fv = p[:, PB_FV * FOX_W:(PB_FV + 1) * FOX_W]
    lf_rows = logf.reshape(batch, seq, FOX_HEADS).transpose(0, 2, 1).reshape(batch * FOX_HEADS * (seq // LANE), LANE)
    c_row = cumsum_time(lf_rows, seq // LANE).reshape(batch * FOX_HEADS, 1, seq)
    o_fox = fox_prompt(fqn, fkn.astype(BF16), fv.astype(BF16), c_row, batch, seq, 256, 256)

    mem2 = mem_prompt.reshape(batch * MEM_TOKENS, D_MODEL)
    memn = rmsnorm_rows(mem2, ln_mem_g[l], 256)
    mkv = matmul([memn], wts['w_mem_kv'], batch * MEM_TOKENS, 512)
    mk = head_rmsnorm(mkv, 0, mem_kn_g[l], MEM_HEADS, MEM_HD, 256)
    mv = mkv[:, MEM_W:]
    o_mem = mem_attend_prompt(p, mk.astype(BF16), mv.astype(BF16), mem_qn_g[l], batch, seq, 512)

    y_p = _channel_mix(x2, o_dn, o_fox, o_mem, wts, ln_ffn_g[l])

    xs = x_sample.reshape(nb, D_MODEL)
    sp, sps = _project(xs, ln_mix_g[l], wts)
    sgt = gates(sps, dn_a_log[l], dn_dt_bias[l], fox_f_bias[l], nb)
    s_beta = sgt[:, :DN_HEADS].T.reshape(DN_HEADS, nb, 1)
    s_g = sgt[:, DN_HEADS:2 * DN_HEADS].T.reshape(DN_HEADS, nb, 1)
    s_logf = sgt[:, 2 * DN_HEADS:2 * DN_HEADS + FOX_HEADS]
    so_dn, s_state = dn_sample(sp, state_conv[l], conv_w[l], s_g, s_beta, dn_norm_g[l], state_delta[l], 8)
    conv_s = jnp.concatenate([state_conv[l][:, 1:, :], sp[:, None, :CONV_CH]], axis=1)

    sfq = head_rmsnorm(sp, PB_FQ, fox_qn_g[l], FOX_HEADS, FOX_HD, nb)
    sfk = head_rmsnorm(sp, PB_FK, fox_kn_g[l], FOX_HEADS, FOX_HD, nb)
    sfv = sp[:, PB_FV * FOX_W:(PB_FV + 1) * FOX_W]
    so_fox = fox_sample(sfq.reshape(nb, FOX_HEADS, FOX_HD), sfk.reshape(nb, FOX_HEADS, FOX_HD),
                        sfv.reshape(nb, FOX_HEADS, FOX_HD), s_logf.reshape(nb, FOX_HEADS, 1),
                        cache_fox_k[l], cache_fox_v[l], jnp.swapaxes(cache_fox_logf[l], 1, 2),
                        page_table, 4).reshape(nb, FOX_W)
    smq = sp[:, PB_MQ * MEM_W:(PB_MQ + 1) * MEM_W].reshape(nb, MEM_HEADS, MEM_HD)
    so_mem = mem_sample(smq, cache_mem_k[l], cache_mem_v[l], mem_qn_g[l]).reshape(nb, MEM_W)
    y_s = _channel_mix(xs, so_dn, so_fox, so_mem, wts, ln_ffn_g[l])

    return (
        y_p.reshape(batch, seq, D_MODEL),
        y_s.reshape(nb, 1, D_MODEL),
        fkn.reshape(1, batch, seq, FOX_HEADS, FOX_HD),
        fv.reshape(1, batch, seq, FOX_HEADS, FOX_HD),
        logf.reshape(1, batch, seq, FOX_HEADS),
        dn_state.reshape(1, batch, DN_HEADS, DN_D, DN_D),
        p.reshape(batch, seq, P_COLS)[:, seq - (DN_CONV - 1):, :CONV_CH][None],
        mk.reshape(1, batch, MEM_TOKENS, MEM_HEADS, MEM_HD),
        mv.reshape(1, batch, MEM_TOKENS, MEM_HEADS, MEM_HD),
        sfk.reshape(1, nb, 1, FOX_HEADS, FOX_HD),
        sfv.reshape(1, nb, 1, FOX_HEADS, FOX_HD),
        s_logf.reshape(1, nb, 1, FOX_HEADS),
        s_state[None],
        conv_s[None],
    )
```

```python
import functools

import jax
import jax.numpy as jnp
from jax import lax
from jax.experimental import pallas as pl
from jax.experimental.pallas import tpu as pltpu

F32 = jnp.float32
BF16 = jnp.bfloat16
HIGHEST = lax.Precision.HIGHEST
EPS = 1e-6
NEG_INF = float("-inf")

D_MODEL = 4096
DN_HEADS = 16
DN_D = 128
DN_CONV = 4
DN_CHUNK = 64
FOX_HEADS = 8
FOX_HD = 128
MEM_TOKENS = 256
MEM_HEADS = 4
MEM_HD = 256
PEER_HEADS = 8
PEER_NKEYS = 128
PEER_DKEY = 128
PEER_TOPK = 16
PAGE_SIZE = 128

DN_QK = DN_HEADS * DN_D
CONV_CH = 3 * DN_QK
FOX_W = FOX_HEADS * FOX_HD
MEM_W = MEM_HEADS * MEM_HD
_OFF_Z = CONV_CH
_OFF_B = _OFF_Z + DN_QK
_OFF_A = _OFF_B + DN_HEADS
_OFF_FQ = _OFF_A + DN_HEADS
_OFF_FK = _OFF_FQ + FOX_W
_OFF_FV = _OFF_FK + FOX_W
_OFF_FF = _OFF_FV + FOX_W
_OFF_MQ = _OFF_FF + FOX_HEADS
_IN_COLS = _OFF_MQ + MEM_W
P_DN_COLS = CONV_CH + DN_QK
LANE = 128
SMALL_COLS = LANE

VMEM_LIMIT_BYTES = 56 * 1024 * 1024


def _cparams(*sem):
    return pltpu.CompilerParams(dimension_semantics=sem, vmem_limit_bytes=VMEM_LIMIT_BYTES)


def _sigmoid(x):
    return 1.0 / (1.0 + jnp.exp(-x))


def _softplus(x):
    return jnp.maximum(x, 0.0) + jnp.log1p(jnp.exp(-jnp.abs(x)))


def _dot_hi(a, b):
    return jnp.dot(a, b, precision=HIGHEST, preferred_element_type=F32)


def _split_bf16(x):
    hi = x.astype(BF16)
    return hi, (x - hi.astype(F32)).astype(BF16)


def _dot_split(a, b):
    (ah, al), (bh, bl) = a, b
    small = jnp.dot(ah, bl, preferred_element_type=F32) + jnp.dot(al, bh, preferred_element_type=F32)
    return small + jnp.dot(ah, bh, preferred_element_type=F32)


def _dot_nt(a, b):
    return lax.dot_general(a, b, (((1,), (1,)), ((), ())), preferred_element_type=F32)


def _row_to_col(row, n):
    a = lax.broadcasted_iota(jnp.int32, (n, n), 0)
    b = lax.broadcasted_iota(jnp.int32, (n, n), 1)
    return jnp.sum(jnp.where(a == b, row, 0.0), axis=1, keepdims=True)


def _rmsnorm_kernel(x_ref, g_ref, o_ref):
    x = x_ref[...]
    y = x * lax.rsqrt(jnp.mean(x * x, axis=-1, keepdims=True) + EPS) * g_ref[...]
    o_ref[...] = y.astype(o_ref.dtype)


def rmsnorm_rows(x, g, tm, out_dtype=BF16):
    m, k = x.shape
    return pl.pallas_call(
        _rmsnorm_kernel,
        grid=(m // tm,),
        in_specs=[pl.BlockSpec((tm, k), lambda i: (i, 0)), pl.BlockSpec((1, k), lambda i: (0, 0))],
        out_specs=pl.BlockSpec((tm, k), lambda i: (i, 0)),
        out_shape=jax.ShapeDtypeStruct((m, k), out_dtype),
        compiler_params=_cparams("parallel"),
        name="rmsnorm_rows",
    )(x, g.reshape(1, k))


def _head_rmsnorm_kernel(x_ref, g_ref, o_ref, *, heads, hd):
    g = g_ref[...]
    for h in range(heads):
        x = x_ref[:, h * hd:(h + 1) * hd]
        y = x * lax.rsqrt(jnp.mean(x * x, axis=-1, keepdims=True) + EPS) * g
        o_ref[:, h * hd:(h + 1) * hd] = y.astype(o_ref.dtype)


def head_rmsnorm(x, col_block, g, heads, hd, tm, out_dtype=F32):
    m = x.shape[0]
    w = heads * hd
    return pl.pallas_call(
        functools.partial(_head_rmsnorm_kernel, heads=heads, hd=hd),
        grid=(m // tm,),
        in_specs=[pl.BlockSpec((tm, w), lambda i: (i, col_block)), pl.BlockSpec((1, hd), lambda i: (0, 0))],
        out_specs=pl.BlockSpec((tm, w), lambda i: (i, 0)),
        out_shape=jax.ShapeDtypeStruct((m, w), out_dtype),
        compiler_params=_cparams("parallel"),
        name="head_rmsnorm",
    )(x, g.reshape(1, hd))


def _matmul_kernel(*refs, ksizes, has_res):
    n_a = len(ksizes)
    w_ref = refs[n_a]
    o_ref = refs[-1]
    acc = None
    off = 0
    for a_ref, ks in zip(refs[:n_a], ksizes):
        part = jnp.dot(a_ref[...], w_ref[off:off + ks, :], preferred_element_type=F32)
        acc = part if acc is None else acc + part
        off += ks
    if has_res:
        acc = acc + refs[n_a + 1][...]
    o_ref[...] = acc


def matmul(a_list, w, tm, tn, residual=None):
    m = a_list[0].shape[0]
    k, n = w.shape
    ksizes = tuple(a.shape[1] for a in a_list)
    assert sum(ksizes) == k and m % tm == 0 and n % tn == 0
    in_specs = [pl.BlockSpec((tm, ks), lambda i, j: (i, 0)) for ks in ksizes]
    in_specs.append(pl.BlockSpec((k, tn), lambda i, j: (0, j)))
    args = list(a_list) + [w]
    if residual is not None:
        in_specs.append(pl.BlockSpec((tm, tn), lambda i, j: (i, j)))
        args.append(residual)
    return pl.pallas_call(
        functools.partial(_matmul_kernel, ksizes=ksizes, has_res=residual is not None),
        grid=(m // tm, n // tn),
        in_specs=in_specs,
        out_specs=pl.BlockSpec((tm, tn), lambda i, j: (i, j)),
        out_shape=jax.ShapeDtypeStruct((m, n), F32),
        compiler_params=_cparams("parallel", "parallel"),
        name="matmul",
    )(*args)


def _gates_kernel(p_ref, alog_ref, dtb_ref, fb_ref, o_ref):
    x = p_ref[...]
    lane = lax.broadcasted_iota(jnp.int32, x.shape, 1)
    beta = _sigmoid(x)
    g = -jnp.exp(alog_ref[...]) * _softplus(x + dtb_ref[...])
    logf = -_softplus(-(x + fb_ref[...]))
    out = jnp.where(lane < DN_HEADS, beta,
                    jnp.where(lane < 2 * DN_HEADS, g,
                              jnp.where(lane < 2 * DN_HEADS + FOX_HEADS, logf, 0.0)))
    o_ref[...] = out


def gates(p_small, a_log, dt_bias, f_bias, tm):
    m = p_small.shape[0]

    def pad(v, off):
        return jnp.zeros((1, SMALL_COLS), F32).at[0, off:off + v.shape[0]].set(v)

    row = pl.BlockSpec((1, SMALL_COLS), lambda i: (0, 0))
    return pl.pallas_call(
        _gates_kernel,
        grid=(m // tm,),
        in_specs=[pl.BlockSpec((tm, SMALL_COLS), lambda i: (i, 0)), row, row, row],
        out_specs=pl.BlockSpec((tm, SMALL_COLS), lambda i: (i, 0)),
        out_shape=jax.ShapeDtypeStruct((m, SMALL_COLS), F32),
        compiler_params=_cparams("parallel"),
        name="gates",
    )(p_small, pad(a_log, DN_HEADS), pad(dt_bias, DN_HEADS), pad(f_bias, 2 * DN_HEADS))


def _cumsum_kernel(x_ref, o_ref, *, blocks_per_group):
    x = x_ref[...]
    r, n = x.shape
    a = lax.broadcasted_iota(jnp.int32, (n, n), 0)
    b = lax.broadcasted_iota(jnp.int32, (n, n), 1)
    local = _dot_hi(x, (a <= b).astype(F32))
    tot = jnp.broadcast_to(local[:, n - 1:n], (r, n))
    ra = lax.broadcasted_iota(jnp.int32, (r, r), 0)
    rb = lax.broadcasted_iota(jnp.int32, (r, r), 1)
    earlier = jnp.logical_and(rb < ra, (ra // blocks_per_group) == (rb // blocks_per_group)).astype(F32)
    o_ref[...] = local + _dot_hi(earlier, tot)


def cumsum_time(x, blocks_per_group):
    return pl.pallas_call(
        functools.partial(_cumsum_kernel, blocks_per_group=blocks_per_group),
        out_shape=jax.ShapeDtypeStruct(x.shape, F32),
        compiler_params=pltpu.CompilerParams(vmem_limit_bytes=VMEM_LIMIT_BYTES),
        name="cumsum_time",
    )(x)


def _fox_prompt_kernel(q_ref, k_ref, v_ref, c_ref, o_ref, *, blk, hg, scale):
    i = pl.program_id(2)
    q0 = pl.multiple_of(i * blk, blk)
    hd = FOX_HD
    qs = [q_ref[:, h * hd:(h + 1) * hd] for h in range(hg)]
    cqs = [_row_to_col(c_ref[h, :, pl.ds(q0, blk)], blk) for h in range(hg)]

    def scores(h, k0):
        k = k_ref[pl.ds(k0, blk), h * hd:(h + 1) * hd]
        return _dot_nt(qs[h], k) * scale + (cqs[h] - c_ref[h, :, pl.ds(k0, blk)])

    def update(h, state, s, k0):
        m, l, acc = state
        m_new = jnp.maximum(m, jnp.max(s, axis=1, keepdims=True))
        alpha = jnp.exp(m - m_new)
        p = jnp.exp(s - m_new)
        l = l * alpha + jnp.sum(p, axis=1, keepdims=True)
        v = v_ref[pl.ds(k0, blk), h * hd:(h + 1) * hd]
        acc = acc * alpha + jnp.dot(p.astype(BF16), v, preferred_element_type=F32)
        return m_new, l, acc

    def body(j, carry):
        k0 = pl.multiple_of(j * blk, blk)
        ss = [scores(h, k0) for h in range(hg)]
        return tuple(update(h, carry[h], ss[h], k0) for h in range(hg))

    init = tuple((jnp.full((blk, 1), NEG_INF, F32), jnp.zeros((blk, 1), F32), jnp.zeros((blk, hd), F32))
                 for _ in range(hg))
    carry = lax.fori_loop(0, i, body, init)
    causal = (lax.broadcasted_iota(jnp.int32, (blk, blk), 1) <= lax.broadcasted_iota(jnp.int32, (blk, blk), 0))
    for h in range(hg):
        s = jnp.where(causal, scores(h, q0), NEG_INF)
        _, l, acc = update(h, carry[h], s, q0)
        o_ref[:, h * hd:(h + 1) * hd] = (acc / l).astype(o_ref.dtype)


def fox_prompt(qn, kn, v, c_row, batch, seq, blk, hg):
    nq = seq // blk
    ng = FOX_HEADS // hg
    w = hg * FOX_HD
    kv_spec = pl.BlockSpec((seq, w), lambda b, h, i: (b, h))
    return pl.pallas_call(
        functools.partial(_fox_prompt_kernel, blk=blk, hg=hg, scale=FOX_HD ** -0.5),
        grid=(batch, ng, nq),
        in_specs=[pl.BlockSpec((blk, w), lambda b, h, i: (b * nq + i, h)), kv_spec, kv_spec,
                  pl.BlockSpec((hg, 1, seq), lambda b, h, i: (b * ng + h, 0, 0))],
        out_specs=pl.BlockSpec((blk, w), lambda b, h, i: (b * nq + i, h)),
        out_shape=jax.ShapeDtypeStruct((batch * seq, FOX_W), BF16),
        compiler_params=_cparams("parallel", "parallel", "arbitrary"),
        name="fox_prompt",
    )(qn, kn, v, c_row)


def _dn_prompt_kernel(q_ref, k_ref, v_ref, z_ref, wq_ref, wk_ref, wv_ref, g_ref, b_ref, ng_ref,
                      o_ref, s_out_ref, s_ref, xq_ref, xk_ref, xv_ref, *, tc, chunk, hg):
    t = pl.program_id(2)
    nc = tc // chunk
    pad = 8

    @pl.when(t == 0)
    def _():
        s_ref[...] = jnp.zeros_like(s_ref)
        for buf in (xq_ref, xk_ref, xv_ref):
            buf[0:pad, :] = jnp.zeros((pad, hg * DN_D), F32)

    def conv_silu(x_ref, w_ref, buf_ref):
        u = x_ref[...]
        buf_ref[pad:pad + tc, :] = u
        w = w_ref[...]
        out = buf_ref[pad - 3:pad - 3 + tc, :] * w[0:1, :]
        out = out + buf_ref[pad - 2:pad - 2 + tc, :] * w[1:2, :]
        out = out + buf_ref[pad - 1:pad - 1 + tc, :] * w[2:3, :]
        out = out + u * w[3:4, :]
        buf_ref[0:pad, :] = u[tc - pad:tc, :]
        return out * _sigmoid(out)

    def l2norm(x):
        return x * lax.rsqrt(jnp.sum(x * x, axis=-1, keepdims=True) + EPS)

    qc = conv_silu(q_ref, wq_ref, xq_ref)
    kc = conv_silu(k_ref, wk_ref, xk_ref)
    vc = conv_silu(v_ref, wv_ref, xv_ref)

    ra = lax.broadcasted_iota(jnp.int32, (chunk, chunk), 0)
    rb = lax.broadcasted_iota(jnp.int32, (chunk, chunk), 1)
    tri = ra >= rb
    strict = ra > rb
    eye_f = (ra == rb).astype(F32)
    upper_f = (ra <= rb).astype(F32)

    heads = []
    for hh in range(hg):
        cs = slice(hh * DN_D, (hh + 1) * DN_D)
        heads.append(dict(
            q=l2norm(qc[:, cs]) * (DN_D ** -0.5), k=l2norm(kc[:, cs]), v=vc[:, cs],
            gc=_dot_hi(g_ref[hh], upper_f),
            beta=b_ref[hh], s=s_ref[hh], outs=[]))

    work = []
    for c in range(nc):
        sl = slice(c * chunk, (c + 1) * chunk)
        for hd in heads:
            q, k, v = hd['q'][sl], hd['k'][sl], hd['v'][sl]
            gc_row = hd['gc'][c:c + 1, :]
            gc = _row_to_col(gc_row, chunk)
            beta = _row_to_col(hd['beta'][c:c + 1, :], chunk)
            decay = jnp.where(tri, jnp.exp(jnp.where(tri, gc - gc_row, 0.0)), 0.0)
            qk_b = jnp.concatenate([q, k], axis=0).astype(BF16)
            work.append(dict(hd=hd, q=q, k=k, v=v, gc=gc, beta=beta, decay=decay, qk_b=qk_b,
                             gc_last=gc_row[:, chunk - 1:chunk]))
    for wk in work:
        wk['gram'] = _dot_nt(wk['qk_b'], wk['qk_b'][chunk:, :])
    for wk in work:
        lower = jnp.where(strict, wk['beta'] * wk['gram'][chunk:, :] * wk['decay'], 0.0)
        wk['inv'] = eye_f - lower
        wk['power'] = _split_bf16(lower)
    for wk in work:
        wk['power'] = _split_bf16(_dot_split(wk['power'], wk['power']))
    span = 2
    while span < chunk:
        for wk in work:
            wk['inv'] = wk['inv'] + _dot_split(_split_bf16(wk['inv']), wk['power'])
        span *= 2
        if span < chunk:
            for wk in work:
                wk['power'] = _split_bf16(_dot_split(wk['power'], wk['power']))
    for wk in work:
        egc = jnp.exp(wk['gc'])
        k, beta = wk['k'], wk['beta']
        uw = _dot_split(_split_bf16(wk['inv']),
                        _split_bf16(jnp.concatenate([wk['v'] * beta, k * (beta * egc)], axis=1)))
        wk['u'] = uw[:, :DN_D]
        wk['lhs1'] = jnp.concatenate([uw[:, DN_D:], wk['q'] * egc], axis=0).astype(BF16)
        qk = jnp.where(tri, wk['gram'][:chunk, :] * wk['decay'], 0.0)
        k_dec = k * jnp.exp(wk['gc_last'] - wk['gc'])
        wk['lhs2'] = jnp.concatenate([qk, k_dec.T], axis=0).astype(BF16)
        wk['g_last'] = jnp.exp(wk['gc_last'])
    for wk in work:
        hd = wk['hd']
        s = hd['s']
        ws = jnp.dot(wk['lhs1'], s.astype(BF16), preferred_element_type=F32)
        v_new = wk['u'] - ws[:chunk, :]
        upd = jnp.dot(wk['lhs2'], v_new.astype(BF16), preferred_element_type=F32)
        hd['outs'].append(ws[chunk:, :] + upd[:chunk, :])
        hd['s'] = s * wk['g_last'] + upd[chunk:, :]

    ng = ng_ref[...]
    for hh, hd in enumerate(heads):
        cs = slice(hh * DN_D, (hh + 1) * DN_D)
        s_ref[hh] = hd['s']
        s_out_ref[hh] = hd['s']
        o = jnp.concatenate(hd['outs'], axis=0)
        o = o * lax.rsqrt(jnp.mean(o * o, axis=-1, keepdims=True) + EPS) * ng
        z = z_ref[:, cs]
        o_ref[:, cs] = (o * (z * _sigmoid(z))).astype(o_ref.dtype)


def dn_prompt(p, conv_w, g_chunks, beta_chunks, norm_g, batch, seq, tc, hg):
    nt = seq // tc
    nc = tc // DN_CHUNK
    ng = DN_HEADS // hg
    w = hg * DN_D

    def col(off):
        return pl.BlockSpec((tc, w), lambda b, h, t: (b * nt + t, off * ng + h))

    def wcol(off):
        return pl.BlockSpec((DN_CONV, w), lambda b, h, t: (0, off * ng + h))

    gspec = pl.BlockSpec((hg, nc, DN_CHUNK), lambda b, h, t: (b * ng + h, t, 0))
    return pl.pallas_call(
        functools.partial(_dn_prompt_kernel, tc=tc, chunk=DN_CHUNK, hg=hg),
        grid=(batch, ng, nt),
        in_specs=[col(0), col(1), col(2), col(3), wcol(0), wcol(1), wcol(2), gspec, gspec,
                  pl.BlockSpec((1, DN_D), lambda b, h, t: (0, 0))],
        out_specs=[pl.BlockSpec((tc, w), lambda b, h, t: (b * nt + t, h)),
                   pl.BlockSpec((hg, DN_D, DN_D), lambda b, h, t: (b * ng + h, 0, 0))],
        out_shape=[jax.ShapeDtypeStruct((batch * seq, DN_QK), BF16),
                   jax.ShapeDtypeStruct((batch * DN_HEADS, DN_D, DN_D), F32)],
        scratch_shapes=[pltpu.VMEM((hg, DN_D, DN_D), F32)] + [pltpu.VMEM((tc + 8, w), F32)] * 3,
        compiler_params=_cparams("parallel", "parallel", "arbitrary"),
        name="dn_prompt",
    )(p, p, p, p, conv_w, conv_w, conv_w, g_chunks, beta_chunks, norm_g.reshape(1, DN_D))


def _dn_sample_kernel(q_ref, k_ref, v_ref, z_ref, cq_ref, ck_ref, cv_ref, wq_ref, wk_ref, wv_ref,
                      g_ref, b_ref, ng_ref, s_ref, o_ref, s_out_ref, *, bb):
    def conv_silu(x_ref, c_ref, w_ref):
        w = w_ref[...]
        out = c_ref[:, 0, :] * w[0:1, :]
        out = out + c_ref[:, 1, :] * w[1:2, :]
        out = out + c_ref[:, 2, :] * w[2:3, :]
        out = out + x_ref[...] * w[3:4, :]
        return out * _sigmoid(out)

    def l2norm(x):
        return x * lax.rsqrt(jnp.sum(x * x, axis=-1, keepdims=True) + EPS)

    q = l2norm(conv_silu(q_ref, cq_ref, wq_ref)) * (DN_D ** -0.5)
    k = l2norm(conv_silu(k_ref, ck_ref, wk_ref))
    v = conv_silu(v_ref, cv_ref, wv_ref)
    q_t = q.T
    k_t = k.T
    decay = jnp.exp(g_ref[0])
    beta = b_ref[0]
    rows = []
    for b in range(bb):
        s = s_ref[b, 0] * decay[b:b + 1, :]
        k_col = k_t[:, b:b + 1]
        kv = jnp.sum(k_col * s, axis=0, keepdims=True)
        s = s + k_col * ((v[b:b + 1, :] - kv) * beta[b:b + 1, :])
        s_out_ref[b, 0] = s
        rows.append(jnp.sum(q_t[:, b:b + 1] * s, axis=0, keepdims=True))
    o = jnp.concatenate(rows, axis=0)
    o = o * lax.rsqrt(jnp.mean(o * o, axis=-1, keepdims=True) + EPS) * ng_ref[...]
    z = z_ref[...]
    o_ref[...] = (o * (z * _sigmoid(z))).astype(o_ref.dtype)


def dn_sample(p, state_conv, conv_w, g_t, beta_t, norm_g, state, bb):
    nb = p.shape[0]
    hb = DN_HEADS

    def col(off):
        return pl.BlockSpec((bb, DN_D), lambda i, h: (i, off + h))

    def ccol(off):
        return pl.BlockSpec((bb, DN_CONV - 1, DN_D), lambda i, h: (i, 0, off + h))

    def wcol(off):
        return pl.BlockSpec((DN_CONV, DN_D), lambda i, h: (0, off + h))

    gspec = pl.BlockSpec((1, bb, 1), lambda i, h: (h, i, 0))
    sspec = pl.BlockSpec((bb, 1, DN_D, DN_D), lambda i, h: (i, h, 0, 0))
    return pl.pallas_call(
        functools.partial(_dn_sample_kernel, bb=bb),
        grid=(nb // bb, DN_HEADS),
        in_specs=[col(0), col(hb), col(2 * hb), col(3 * hb), ccol(0), ccol(hb), ccol(2 * hb),
                  wcol(0), wcol(hb), wcol(2 * hb), gspec, gspec,
                  pl.BlockSpec((1, DN_D), lambda i, h: (0, 0)), sspec],
        out_specs=[pl.BlockSpec((bb, DN_D), lambda i, h: (i, h)), sspec],
        out_shape=[jax.ShapeDtypeStruct((nb, DN_QK), BF16), jax.ShapeDtypeStruct(state.shape, F32)],
        compiler_params=_cparams("parallel", "parallel"),
        name="dn_sample",
    )(p, p, p, p, state_conv, state_conv, state_conv, conv_w, conv_w, conv_w, g_t, beta_t,
      norm_g.reshape(1, DN_D), state)


def _fox_sample_kernel(pt_ref, q_ref, kn_ref, vn_ref, lfn_ref, *refs, pp, scale):
    k_refs = refs[0:pp]
    v_refs = refs[pp:2 * pp]
    lf_refs = refs[2 * pp:3 * pp]
    o_ref = refs[3 * pp]
    m_ref, l_ref, acc_ref, carry_ref = refs[3 * pp + 1:]
    j = pl.program_id(1)
    qs = q_ref[0] * scale

    @pl.when(j == 0)
    def _():
        m_ref[...] = jnp.sum(qs * kn_ref[0], axis=1, keepdims=True)
        l_ref[...] = jnp.ones_like(l_ref)
        acc_ref[...] = vn_ref[0]
        carry_ref[...] = lfn_ref[0]

    ra = lax.broadcasted_iota(jnp.int32, (PAGE_SIZE, PAGE_SIZE), 0)
    rb = lax.broadcasted_iota(jnp.int32, (PAGE_SIZE, PAGE_SIZE), 1)
    later_f = (ra > rb).astype(F32)
    shape3 = (PAGE_SIZE, FOX_HEADS, FOX_HD)
    diag3 = lax.broadcasted_iota(jnp.int32, shape3, 0) == lax.broadcasted_iota(jnp.int32, shape3, 2)
    carry = carry_ref[...]
    logits = []
    for i in range(pp):
        lf = lf_refs[i][0]
        bias = _dot_hi(lf, later_f) + carry
        carry = carry + jnp.sum(lf, axis=1, keepdims=True)
        logits.append(jnp.sum(k_refs[i][0] * qs[None] + jnp.where(diag3, bias[None], 0.0),
                              axis=2, keepdims=True))
    carry_ref[...] = carry
    m_old = m_ref[...]
    m_new = m_old
    for s3 in logits:
        m_new = jnp.maximum(m_new, jnp.max(s3, axis=0))
    alpha = jnp.exp(m_old - m_new)
    l_new = l_ref[...] * alpha
    acc = acc_ref[...] * alpha
    for i, s3 in enumerate(logits):
        p3 = jnp.exp(s3 - m_new[None])
        l_new = l_new + jnp.sum(p3, axis=0)
        acc = acc + jnp.sum(p3 * v_refs[i][0], axis=0)
    l_ref[...] = l_new
    acc_ref[...] = acc
    m_ref[...] = m_new

    @pl.when(j == pl.num_programs(1) - 1)
    def _():
        o_ref[0] = (acc_ref[...] / l_ref[...]).astype(o_ref.dtype)


def fox_sample(q, k_new, v_new, lf_new, k_pool, v_pool, lf_pool_t, page_table, pp):
    nb, n_pages = page_table.shape
    assert n_pages % pp == 0
    steps = n_pages // pp

    def page_idx4(i):
        return lambda b, j, pt: (pt[b, n_pages - 1 - (j * pp + i)], 0, 0, 0)

    def page_idx3(i):
        return lambda b, j, pt: (pt[b, n_pages - 1 - (j * pp + i)], 0, 0)

    tok = lambda b, j, pt: (b, 0, 0)
    in_specs = [pl.BlockSpec((1, FOX_HEADS, FOX_HD), tok)] * 3 + [pl.BlockSpec((1, FOX_HEADS, 1), tok)]
    in_specs += [pl.BlockSpec((1, PAGE_SIZE, FOX_HEADS, FOX_HD), page_idx4(i)) for i in range(pp)]
    in_specs += [pl.BlockSpec((1, PAGE_SIZE, FOX_HEADS, FOX_HD), page_idx4(i)) for i in range(pp)]
    in_specs += [pl.BlockSpec((1, FOX_HEADS, PAGE_SIZE), page_idx3(i)) for i in range(pp)]
    grid_spec = pltpu.PrefetchScalarGridSpec(
        num_scalar_prefetch=1,
        grid=(nb, steps),
        in_specs=in_specs,
        out_specs=pl.BlockSpec((1, FOX_HEADS, FOX_HD), tok),
        scratch_shapes=[pltpu.VMEM((FOX_HEADS, 1), F32), pltpu.VMEM((FOX_HEADS, 1), F32),
                        pltpu.VMEM((FOX_HEADS, FOX_HD), F32), pltpu.VMEM((FOX_HEADS, 1), F32)],
    )
    return pl.pallas_call(
        functools.partial(_fox_sample_kernel, pp=pp, scale=FOX_HD ** -0.5),
        grid_spec=grid_spec,
        out_shape=jax.ShapeDtypeStruct((nb, FOX_HEADS, FOX_HD), BF16),
        compiler_params=_cparams("parallel", "arbitrary"),
        name="fox_sample",
    )(page_table, q, k_new, v_new, lf_new, *([k_pool] * pp), *([v_pool] * pp), *([lf_pool_t] * pp))


def _mem_prompt_kernel(q_ref, k_ref, v_ref, g_ref, o_ref, *, scale):
    g = g_ref[...]
    for h in range(MEM_HEADS):
        cs = slice(h * MEM_HD, (h + 1) * MEM_HD)
        q = q_ref[:, cs]
        qn = (q * lax.rsqrt(jnp.mean(q * q, axis=-1, keepdims=True) + EPS) * g).astype(BF16)
        s = _dot_nt(qn, k_ref[:, cs]) * scale
        p = jnp.exp(s - jnp.max(s, axis=1, keepdims=True))
        p = p / jnp.sum(p, axis=1, keepdims=True)
        o_ref[:, cs] = jnp.dot(p.astype(BF16), v_ref[:, cs], preferred_element_type=F32).astype(o_ref.dtype)


def mem_attend_prompt(p, mk, mv, qn_g, batch, seq, tq):
    nq = seq // tq
    kv = pl.BlockSpec((MEM_TOKENS, MEM_W), lambda b, i: (b, 0))
    return pl.pallas_call(
        functools.partial(_mem_prompt_kernel, scale=MEM_HD ** -0.5),
        grid=(batch, nq),
        in_specs=[pl.BlockSpec((tq, MEM_W), lambda b, i: (b * nq + i, 0)), kv, kv,
                  pl.BlockSpec((1, MEM_HD), lambda b, i: (0, 0))],
        out_specs=pl.BlockSpec((tq, MEM_W), lambda b, i: (b * nq + i, 0)),
        out_shape=jax.ShapeDtypeStruct((batch * seq, MEM_W), BF16),
        compiler_params=_cparams("parallel", "parallel"),
        name="mem_prompt",
    )(p, mk, mv, qn_g.reshape(1, MEM_HD))


def _mem_sample_kernel(q_ref, k_ref, v_ref, g_ref, o_ref, *, scale):
    q = q_ref[0]
    qn = q * lax.rsqrt(jnp.mean(q * q, axis=-1, keepdims=True) + EPS) * g_ref[...] * scale
    s3 = jnp.sum(k_ref[0] * qn[None], axis=2, keepdims=True)
    p3 = jnp.exp(s3 - jnp.max(s3, axis=0)[None])
    o = jnp.sum(p3 * v_ref[0], axis=0) / jnp.sum(p3, axis=0)
    o_ref[0] = o.astype(o_ref.dtype)


def mem_sample(q, mk, mv, qn_g):
    nb = q.shape[0]
    tok = pl.BlockSpec((1, MEM_HEADS, MEM_HD), lambda b: (b, 0, 0))
    kv = pl.BlockSpec((1, MEM_TOKENS, MEM_HEADS, MEM_HD), lambda b: (b, 0, 0, 0))
    return pl.pallas_call(
        functools.partial(_mem_sample_kernel, scale=MEM_HD ** -0.5),
        grid=(nb,),
        in_specs=[tok, kv, kv, pl.BlockSpec((1, MEM_HD), lambda b: (0, 0))],
        out_specs=tok,
        out_shape=jax.ShapeDtypeStruct((nb, MEM_HEADS, MEM_HD), BF16),
        compiler_params=_cparams("parallel"),
        name="mem_sample",
    )(q, mk, mv, qn_g.reshape(1, MEM_HD))


def _top_values(s, count):
    rows = s.shape[0]
    idx = lax.broadcasted_iota(jnp.int32, s.shape, 0)
    vals = []
    for _ in range(count):
        m = jnp.max(s, axis=0, keepdims=True)
        first = jnp.min(jnp.where(s == m, idx, rows), axis=0, keepdims=True)
        s = jnp.where(idx == first, NEG_INF, s)
        vals.append(m)
    return jnp.concatenate(vals, axis=0)


def _peer_stats_kernel(hn_ref, wq_ref, keys_ref, s1_ref, g1_ref, s2_ref, e2_ref, tau_ref, st_ref, top_ref):
    nk = PEER_NKEYS
    kk = PEER_TOPK
    q = jnp.dot(hn_ref[...], wq_ref[...], preferred_element_type=F32)
    st_ref[...] = _dot_nt(keys_ref[...], q.astype(BF16))

    def half_body(c, carry):
        r0 = pl.multiple_of(c * nk, nk)
        t0 = pl.multiple_of(c * kk, kk)
        top_ref[pl.ds(t0, kk), :] = _top_values(st_ref[pl.ds(r0, nk), :], kk)
        return carry

    lax.fori_loop(0, 2 * PEER_HEADS, half_body, 0)

    def head_body(h, carry):
        t0 = pl.multiple_of(h * 2 * kk, 2 * kk)
        v1 = top_ref[pl.ds(t0, kk), :]
        v2 = top_ref[pl.ds(t0 + kk, kk), :]
        cand = jnp.concatenate([v1[0:1, :] + v2] + [v1[a:a + 1, :] + v2[0:8, :] for a in range(1, 8)]
                               + [v1[8:kk, :] + v2[0:1, :]], axis=0)
        cv = _top_values(cand, kk)
        z = jnp.sum(jnp.exp(cv - cv[0:1, :]), axis=0, keepdims=True)
        tau_ref[pl.ds(h, 1), :] = cv[kk - 1:kk, :]
        r0 = pl.multiple_of(h * 2 * nk, 2 * nk)
        s1 = st_ref[pl.ds(r0, nk), :]
        s2 = st_ref[pl.ds(r0 + nk, nk), :]
        s1_ref[h] = s1
        s2_ref[h] = s2
        g1_ref[h] = jnp.exp(s1 - v1[0:1, :]) / z
        e2_ref[h] = jnp.exp(s2 - v2[0:1, :])
        return carry

    lax.fori_loop(0, PEER_HEADS, head_body, 0)


def peer_stats(hn, w_q, keys_t, mt):
    m = hn.shape[0]
    nrow = PEER_HEADS * 2 * PEER_NKEYS
    hspec = pl.BlockSpec((PEER_HEADS, PEER_NKEYS, mt), lambda i: (0, 0, i))
    hshape = jax.ShapeDtypeStruct((PEER_HEADS, PEER_NKEYS, m), F32)
    return pl.pallas_call(
        _peer_stats_kernel,
        grid=(m // mt,),
        in_specs=[pl.BlockSpec((mt, D_MODEL), lambda i: (i, 0)),
                  pl.BlockSpec(w_q.shape, lambda i: (0, 0)),
                  pl.BlockSpec(keys_t.shape, lambda i: (0, 0))],
        out_specs=[hspec, hspec, hspec, hspec, pl.BlockSpec((PEER_HEADS, mt), lambda i: (0, i))],
        out_shape=[hshape, hshape, hshape, hshape, jax.ShapeDtypeStruct((PEER_HEADS, m), F32)],
        scratch_shapes=[pltpu.VMEM((nrow, mt), F32), pltpu.VMEM((2 * PEER_HEADS * PEER_TOPK, mt), F32)],
        compiler_params=_cparams("parallel"),
        name="peer_stats",
    )(hn, w_q, keys_t)


def _peer_gates(s1_ref, g1_ref, s2_ref, e2_ref, tau_ref, gate_ref, a_range):
    mt = gate_ref.shape[1]
    rows = 32
    for a in a_range:
        for c in range(mt // LANE):
            cs = slice(c * LANE, (c + 1) * LANE)
            for r in range(PEER_NKEYS // rows):
                rs = slice(r * rows, (r + 1) * rows)
                w = None
                for h in range(PEER_HEADS):
                    hit = (s1_ref[a, h:h + 1, cs] + s2_ref[h, rs, cs]) >= tau_ref[h:h + 1, cs]
                    term = jnp.where(hit, g1_ref[a, h:h + 1, cs] * e2_ref[h, rs, cs], 0.0)
                    w = term if w is None else w + term
                gate_ref[a * PEER_NKEYS + r * rows:a * PEER_NKEYS + (r + 1) * rows, cs] = w


def _peer_mix_kernel(hn_ref, res_ref, u_ref, v_ref, s1_ref, g1_ref, s1n_ref, g1n_ref, s2_ref, e2_ref, tau_ref,
                     o_ref, gate_a_ref, gate_b_ref, *, et):
    e = pl.program_id(1)

    @pl.when(e == 0)
    def _():
        o_ref[...] = res_ref[...]
        _peer_gates(s1_ref, g1_ref, s2_ref, e2_ref, tau_ref, gate_a_ref, range(et // PEER_NKEYS))

    def step(gate_cur_ref, gate_next_ref):
        n_a = et // PEER_NKEYS
        mt = o_ref.shape[0]
        halves = 2 if mt % 512 == 0 else 1
        th = mt // halves
        pieces = []
        for j in range(halves):
            ts = slice(j * th, (j + 1) * th)
            act_t = _dot_nt(u_ref[...], hn_ref[ts, :])
            _peer_gates(s1n_ref, g1n_ref, s2_ref, e2_ref, tau_ref, gate_next_ref,
                        range(j * n_a // halves, (j + 1) * n_a // halves))
            gel_t = 0.5 * act_t * (1.0 + lax.erf(act_t * (0.5 ** 0.5)))
            pieces.append((gate_cur_ref[:, ts] * gel_t).astype(BF16).T)
        hmat = pieces[0] if halves == 1 else jnp.concatenate(pieces, axis=0)
        o_ref[...] += jnp.dot(hmat, v_ref[...], preferred_element_type=F32)

    @pl.when(e % 2 == 0)
    def _():
        step(gate_a_ref, gate_b_ref)

    @pl.when(e % 2 == 1)
    def _():
        step(gate_b_ref, gate_a_ref)


def peer_mix(hn, res, u_tab, v_tab, s1_t, g1_t, s2, e2, tau, mt, et):
    m = hn.shape[0]
    n_exp = u_tab.shape[0]
    n_e = n_exp // et
    a_per = et // PEER_NKEYS
    once = pl.Buffered(1)
    sel1 = pl.BlockSpec((a_per, PEER_HEADS, mt), lambda i, e: (e, 0, i))
    sel1_next = pl.BlockSpec((a_per, PEER_HEADS, mt), lambda i, e: (jnp.minimum(e + 1, n_e - 1), 0, i))
    sel2 = pl.BlockSpec((PEER_HEADS, PEER_NKEYS, mt), lambda i, e: (0, 0, i), pipeline_mode=once)
    tab = pl.BlockSpec((et, D_MODEL), lambda i, e: (e, 0))
    return pl.pallas_call(
        functools.partial(_peer_mix_kernel, et=et),
        grid=(m // mt, n_e),
        in_specs=[pl.BlockSpec((mt, D_MODEL), lambda i, e: (i, 0), pipeline_mode=once),
                  pl.BlockSpec((mt, D_MODEL), lambda i, e: (i, 0), pipeline_mode=once), tab, tab,
                  sel1, sel1, sel1_next, sel1_next, sel2, sel2,
                  pl.BlockSpec((PEER_HEADS, mt), lambda i, e: (0, i), pipeline_mode=once)],
        out_specs=pl.BlockSpec((mt, D_MODEL), lambda i, e: (i, 0)),
        out_shape=jax.ShapeDtypeStruct((m, D_MODEL), F32),
        scratch_shapes=[pltpu.VMEM((et, mt), F32), pltpu.VMEM((et, mt), F32)],
        compiler_params=_cparams("parallel", "arbitrary"),
        name="peer_mix",
    )(hn, res, u_tab, v_tab, s1_t, g1_t, s1_t, g1_t, s2, e2, tau)


def _prep_weights(w_in, w_mem_kv, w_out, peer_w_q, peer_sub_keys, peer_u, peer_v):
    w_dn = w_in[:, :_OFF_B].astype(BF16)
    w_fox = w_in[:, _OFF_FQ:_OFF_FF].astype(BF16)
    w_mq = w_in[:, _OFF_MQ:].astype(BF16)
    w_small = jnp.concatenate(
        [w_in[:, _OFF_B:_OFF_FQ], w_in[:, _OFF_FF:_OFF_MQ],
         jnp.zeros((D_MODEL, SMALL_COLS - 2 * DN_HEADS - FOX_HEADS), F32)], axis=1).astype(BF16)
    half = PEER_DKEY // 2
    pairs = 2 * PEER_HEADS
    sk = peer_sub_keys.reshape(pairs, PEER_NKEYS, half)
    eye = jnp.eye(pairs, dtype=F32)
    keys_t = (eye[:, None, :, None] * sk[:, :, None, :]).reshape(pairs * PEER_NKEYS, pairs * half).astype(BF16)
    return dict(w_dn=w_dn, w_fox=w_fox, w_mq=w_mq, w_small=w_small, w_mem_kv=w_mem_kv.astype(BF16), w_out=w_out.astype(BF16),
                peer_w_q=peer_w_q.astype(BF16), keys_t=keys_t, peer_u=peer_u.astype(BF16),
                peer_v=peer_v.astype(BF16))


def _tile(m, pref):
    return pref if m % pref == 0 else m


def _project(x2, ln_g, wts):
    m = x2.shape[0]
    xn = rmsnorm_rows(x2, ln_g, _tile(m, 256))
    tm = _tile(m, 1024)
    p_dn = matmul([xn], wts['w_dn'], tm, 512)
    p_fox = matmul([xn], wts['w_fox'], tm, 512)
    p_mq = matmul([xn], wts['w_mq'], tm, 512)
    ps = matmul([xn], wts['w_small'], tm, SMALL_COLS)
    return p_dn, p_fox, p_mq, ps


def _channel_mix(x2, o_dn, o_fox, o_mem, wts, ln_ffn_g):
    m = x2.shape[0]
    h = matmul([o_dn, o_fox, o_mem], wts['w_out'], _tile(m, 1024), 512, residual=x2)
    hn = rmsnorm_rows(h, ln_ffn_g, _tile(m, 256))
    s1, g1, s2, e2, tau = peer_stats(hn, wts['peer_w_q'], wts['keys_t'], _tile(m, 256))
    s1_t = jnp.swapaxes(s1, 0, 1)
    g1_t = jnp.swapaxes(g1, 0, 1)
    return peer_mix(hn, h, wts['peer_u'], wts['peer_v'], s1_t, g1_t, s2, e2, tau, _tile(m, 512), 512)


def kernel(x_prompt, x_sample, cache_fox_k, cache_fox_v, cache_fox_logf, state_delta, state_conv, cache_mem_k, cache_mem_v, page_table, mem_prompt, ln_mix_g, w_in, conv_w, dn_a_log, dn_dt_bias, dn_norm_g, fox_f_bias, fox_qn_g, fox_kn_g, ln_mem_g, w_mem_kv, mem_qn_g, mem_kn_g, w_out, ln_ffn_g, peer_w_q, peer_sub_keys, peer_u, peer_v):
    depth = w_in.shape[0]
    assert depth == 1
    l = 0
    batch, seq, _ = x_prompt.shape
    nb = x_sample.shape[0]
    m_p = batch * seq
    wts = _prep_weights(w_in[l], w_mem_kv[l], w_out[l], peer_w_q[l], peer_sub_keys[l], peer_u[l], peer_v[l])

    x2 = x_prompt.reshape(m_p, D_MODEL)
    p, p_fox, p_mq, ps = _project(x2, ln_mix_g[l], wts)
    gt = gates(ps, dn_a_log[l], dn_dt_bias[l], fox_f_bias[l], 1024)
    beta = gt[:, :DN_HEADS]
    gdec = gt[:, DN_HEADS:2 * DN_HEADS]
    logf = gt[:, 2 * DN_HEADS:2 * DN_HEADS + FOX_HEADS]

    def to_chunks(a):
        return a.reshape(batch, seq, DN_HEADS).transpose(0, 2, 1).reshape(batch * DN_HEADS, seq // DN_CHUNK, DN_CHUNK)

    o_dn, dn_state = dn_prompt(p, conv_w[l], to_chunks(gdec), to_chunks(beta), dn_norm_g[l], batch, seq, 512, 2)

    fqn = head_rmsnorm(p_fox, 0, fox_qn_g[l], FOX_HEADS, FOX_HD, 512, out_dtype=BF16)
    fkn = head_rmsnorm(p_fox, 1, fox_kn_g[l], FOX_HEADS, FOX_HD, 512)
    fv = p_fox[:, 2 * FOX_W:]
    lf_rows = logf.reshape(batch, seq, FOX_HEADS).transpose(0, 2, 1).reshape(batch * FOX_HEADS * (seq // LANE), LANE)
    c_row = cumsum_time(lf_rows, seq // LANE).reshape(batch * FOX_HEADS, 1, seq)
    o_fox = fox_prompt(fqn, fkn.astype(BF16), fv.astype(BF16), c_row, batch, seq, 512, 2)

    mem2 = mem_prompt.reshape(batch * MEM_TOKENS, D_MODEL)
    memn = rmsnorm_rows(mem2, ln_mem_g[l], 256)
    mkv = matmul([memn], wts['w_mem_kv'], batch * MEM_TOKENS, 512)
    mk = head_rmsnorm(mkv, 0, mem_kn_g[l], MEM_HEADS, MEM_HD, 256)
    mv = mkv[:, MEM_W:]
    o_mem = mem_attend_prompt(p_mq, mk.astype(BF16), mv.astype(BF16), mem_qn_g[l], batch, seq, 512)

    y_p = _channel_mix(x2, o_dn, o_fox, o_mem, wts, ln_ffn_g[l])

    xs = x_sample.reshape(nb, D_MODEL)
    sp, sp_fox, sp_mq, sps = _project(xs, ln_mix_g[l], wts)
    sgt = gates(sps, dn_a_log[l], dn_dt_bias[l], fox_f_bias[l], nb)
    s_beta = sgt[:, :DN_HEADS].T.reshape(DN_HEADS, nb, 1)
    s_g = sgt[:, DN_HEADS:2 * DN_HEADS].T.reshape(DN_HEADS, nb, 1)
    s_logf = sgt[:, 2 * DN_HEADS:2 * DN_HEADS + FOX_HEADS]
    so_dn, s_state = dn_sample(sp, state_conv[l], conv_w[l], s_g, s_beta, dn_norm_g[l], state_delta[l], 8)
    conv_s = jnp.concatenate([state_conv[l][:, 1:, :], sp[:, None, :CONV_CH]], axis=1)

    sfq = head_rmsnorm(sp_fox, 0, fox_qn_g[l], FOX_HEADS, FOX_HD, nb)
    sfk = head_rmsnorm(sp_fox, 1, fox_kn_g[l], FOX_HEADS, FOX_HD, nb)
    sfv = sp_fox[:, 2 * FOX_W:]
    so_fox = fox_sample(sfq.reshape(nb, FOX_HEADS, FOX_HD), sfk.reshape(nb, FOX_HEADS, FOX_HD),
                        sfv.reshape(nb, FOX_HEADS, FOX_HD), s_logf.reshape(nb, FOX_HEADS, 1),
                        cache_fox_k[l], cache_fox_v[l], jnp.swapaxes(cache_fox_logf[l], 1, 2),
                        page_table, 4).reshape(nb, FOX_W)
    smq = sp_mq.reshape(nb, MEM_HEADS, MEM_HD)
    so_mem = mem_sample(smq, cache_mem_k[l], cache_mem_v[l], mem_qn_g[l]).reshape(nb, MEM_W)
    y_s = _channel_mix(xs, so_dn, so_fox, so_mem, wts, ln_ffn_g[l])

    return (
        y_p.reshape(batch, seq, D_MODEL),
        y_s.reshape(nb, 1, D_MODEL),
        fkn.reshape(1, batch, seq, FOX_HEADS, FOX_HD),
        fv.reshape(1, batch, seq, FOX_HEADS, FOX_HD),
        logf.reshape(1, batch, seq, FOX_HEADS),
        dn_state.reshape(1, batch, DN_HEADS, DN_D, DN_D),
        p.reshape(batch, seq, P_DN_COLS)[:, seq - (DN_CONV - 1):, :CONV_CH][None],
        mk.reshape(1, batch, MEM_TOKENS, MEM_HEADS, MEM_HD),
        mv.reshape(1, batch, MEM_TOKENS, MEM_HEADS, MEM_HD),
        sfk.reshape(1, nb, 1, FOX_HEADS, FOX_HD),
        sfv.reshape(1, nb, 1, FOX_HEADS, FOX_HD),
        s_logf.reshape(1, nb, 1, FOX_HEADS),
        s_state[None],
        conv_s[None],
    )
```

```python
import functools

import jax
import jax.numpy as jnp
from jax import lax
from jax.experimental import pallas as pl
from jax.experimental.pallas import tpu as pltpu

F32 = jnp.float32
BF16 = jnp.bfloat16
HIGHEST = lax.Precision.HIGHEST
EPS = 1e-6
NEG_INF = float("-inf")

D_MODEL = 4096
DN_HEADS = 16
DN_D = 128
DN_CONV = 4
DN_CHUNK = 64
FOX_HEADS = 8
FOX_HD = 128
MEM_TOKENS = 256
MEM_HEADS = 4
MEM_HD = 256
PEER_HEADS = 8
PEER_NKEYS = 128
PEER_DKEY = 128
PEER_TOPK = 16
PAGE_SIZE = 128

DN_QK = DN_HEADS * DN_D
CONV_CH = 3 * DN_QK
FOX_W = FOX_HEADS * FOX_HD
MEM_W = MEM_HEADS * MEM_HD
_OFF_Z = CONV_CH
_OFF_B = _OFF_Z + DN_QK
_OFF_A = _OFF_B + DN_HEADS
_OFF_FQ = _OFF_A + DN_HEADS
_OFF_FK = _OFF_FQ + FOX_W
_OFF_FV = _OFF_FK + FOX_W
_OFF_FF = _OFF_FV + FOX_W
_OFF_MQ = _OFF_FF + FOX_HEADS
_IN_COLS = _OFF_MQ + MEM_W
P_DN_COLS = CONV_CH + DN_QK
LANE = 128
SMALL_COLS = LANE

VMEM_LIMIT_BYTES = 56 * 1024 * 1024


def _cparams(*sem):
    return pltpu.CompilerParams(dimension_semantics=sem, vmem_limit_bytes=VMEM_LIMIT_BYTES)


def _sigmoid(x):
    return 1.0 / (1.0 + jnp.exp(-x))


def _softplus(x):
    return jnp.maximum(x, 0.0) + jnp.log1p(jnp.exp(-jnp.abs(x)))


def _dot_hi(a, b):
    return jnp.dot(a, b, precision=HIGHEST, preferred_element_type=F32)


def _split_bf16(x):
    hi = x.astype(BF16)
    return hi, (x - hi.astype(F32)).astype(BF16)


def _dot_split(a, b):
    (ah, al), (bh, bl) = a, b
    small = jnp.dot(ah, bl, preferred_element_type=F32) + jnp.dot(al, bh, preferred_element_type=F32)
    return small + jnp.dot(ah, bh, preferred_element_type=F32)


def _dot_nt(a, b):
    return lax.dot_general(a, b, (((1,), (1,)), ((), ())), preferred_element_type=F32)


def _row_to_col(row, n):
    a = lax.broadcasted_iota(jnp.int32, (n, n), 0)
    b = lax.broadcasted_iota(jnp.int32, (n, n), 1)
    return jnp.sum(jnp.where(a == b, row, 0.0), axis=1, keepdims=True)


def _rmsnorm_kernel(x_ref, g_ref, o_ref):
    x = x_ref[...]
    y = x * lax.rsqrt(jnp.mean(x * x, axis=-1, keepdims=True) + EPS) * g_ref[...]
    o_ref[...] = y.astype(o_ref.dtype)


def rmsnorm_rows(x, g, tm, out_dtype=BF16):
    m, k = x.shape
    return pl.pallas_call(
        _rmsnorm_kernel,
        grid=(m // tm,),
        in_specs=[pl.BlockSpec((tm, k), lambda i: (i, 0)), pl.BlockSpec((1, k), lambda i: (0, 0))],
        out_specs=pl.BlockSpec((tm, k), lambda i: (i, 0)),
        out_shape=jax.ShapeDtypeStruct((m, k), out_dtype),
        compiler_params=_cparams("parallel"),
        name="rmsnorm_rows",
    )(x, g.reshape(1, k))


def _head_rmsnorm_kernel(x_ref, g_ref, o_ref, *, heads, hd):
    g = g_ref[...]
    for h in range(heads):
        x = x_ref[:, h * hd:(h + 1) * hd]
        y = x * lax.rsqrt(jnp.mean(x * x, axis=-1, keepdims=True) + EPS) * g
        o_ref[:, h * hd:(h + 1) * hd] = y.astype(o_ref.dtype)


def head_rmsnorm(x, col_block, g, heads, hd, tm, out_dtype=F32):
    m = x.shape[0]
    w = heads * hd
    return pl.pallas_call(
        functools.partial(_head_rmsnorm_kernel, heads=heads, hd=hd),
        grid=(m // tm,),
        in_specs=[pl.BlockSpec((tm, w), lambda i: (i, col_block)), pl.BlockSpec((1, hd), lambda i: (0, 0))],
        out_specs=pl.BlockSpec((tm, w), lambda i: (i, 0)),
        out_shape=jax.ShapeDtypeStruct((m, w), out_dtype),
        compiler_params=_cparams("parallel"),
        name="head_rmsnorm",
    )(x, g.reshape(1, hd))


def _matmul_kernel(*refs, ksizes, has_res):
    n_a = len(ksizes)
    w_ref = refs[n_a]
    o_ref = refs[-1]
    acc = None
    off = 0
    for a_ref, ks in zip(refs[:n_a], ksizes):
        part = jnp.dot(a_ref[...], w_ref[off:off + ks, :], preferred_element_type=F32)
        acc = part if acc is None else acc + part
        off += ks
    if has_res:
        acc = acc + refs[n_a + 1][...]
    o_ref[...] = acc


def matmul(a_list, w, tm, tn, residual=None):
    m = a_list[0].shape[0]
    k, n = w.shape
    ksizes = tuple(a.shape[1] for a in a_list)
    assert sum(ksizes) == k and m % tm == 0 and n % tn == 0
    in_specs = [pl.BlockSpec((tm, ks), lambda i, j: (i, 0)) for ks in ksizes]
    in_specs.append(pl.BlockSpec((k, tn), lambda i, j: (0, j)))
    args = list(a_list) + [w]
    if residual is not None:
        in_specs.append(pl.BlockSpec((tm, tn), lambda i, j: (i, j)))
        args.append(residual)
    return pl.pallas_call(
        functools.partial(_matmul_kernel, ksizes=ksizes, has_res=residual is not None),
        grid=(m // tm, n // tn),
        in_specs=in_specs,
        out_specs=pl.BlockSpec((tm, tn), lambda i, j: (i, j)),
        out_shape=jax.ShapeDtypeStruct((m, n), F32),
        compiler_params=_cparams("parallel", "parallel"),
        name="matmul",
    )(*args)


def _gates_kernel(p_ref, alog_ref, dtb_ref, fb_ref, o_ref):
    x = p_ref[...]
    lane = lax.broadcasted_iota(jnp.int32, x.shape, 1)
    beta = _sigmoid(x)
    g = -jnp.exp(alog_ref[...]) * _softplus(x + dtb_ref[...])
    logf = -_softplus(-(x + fb_ref[...]))
    out = jnp.where(lane < DN_HEADS, beta,
                    jnp.where(lane < 2 * DN_HEADS, g,
                              jnp.where(lane < 2 * DN_HEADS + FOX_HEADS, logf, 0.0)))
    o_ref[...] = out


def gates(p_small, a_log, dt_bias, f_bias, tm):
    m = p_small.shape[0]

    def pad(v, off):
        return jnp.zeros((1, SMALL_COLS), F32).at[0, off:off + v.shape[0]].set(v)

    row = pl.BlockSpec((1, SMALL_COLS), lambda i: (0, 0))
    return pl.pallas_call(
        _gates_kernel,
        grid=(m // tm,),
        in_specs=[pl.BlockSpec((tm, SMALL_COLS), lambda i: (i, 0)), row, row, row],
        out_specs=pl.BlockSpec((tm, SMALL_COLS), lambda i: (i, 0)),
        out_shape=jax.ShapeDtypeStruct((m, SMALL_COLS), F32),
        compiler_params=_cparams("parallel"),
        name="gates",
    )(p_small, pad(a_log, DN_HEADS), pad(dt_bias, DN_HEADS), pad(f_bias, 2 * DN_HEADS))


def _cumsum_kernel(x_ref, o_ref, *, blocks_per_group):
    x = x_ref[...]
    r, n = x.shape
    a = lax.broadcasted_iota(jnp.int32, (n, n), 0)
    b = lax.broadcasted_iota(jnp.int32, (n, n), 1)
    local = _dot_hi(x, (a <= b).astype(F32))
    tot = jnp.broadcast_to(local[:, n - 1:n], (r, n))
    ra = lax.broadcasted_iota(jnp.int32, (r, r), 0)
    rb = lax.broadcasted_iota(jnp.int32, (r, r), 1)
    earlier = jnp.logical_and(rb < ra, (ra // blocks_per_group) == (rb // blocks_per_group)).astype(F32)
    o_ref[...] = local + _dot_hi(earlier, tot)


def cumsum_time(x, blocks_per_group):
    return pl.pallas_call(
        functools.partial(_cumsum_kernel, blocks_per_group=blocks_per_group),
        out_shape=jax.ShapeDtypeStruct(x.shape, F32),
        compiler_params=pltpu.CompilerParams(vmem_limit_bytes=VMEM_LIMIT_BYTES),
        name="cumsum_time",
    )(x)


def _fox_prompt_kernel(q_ref, k_ref, v_ref, c_ref, o_ref, *, blk, hg, scale):
    i = pl.program_id(2)
    q0 = pl.multiple_of(i * blk, blk)
    hd = FOX_HD
    qs = [q_ref[:, h * hd:(h + 1) * hd] for h in range(hg)]
    cqs = [_row_to_col(c_ref[h, :, pl.ds(q0, blk)], blk) for h in range(hg)]

    def scores(h, k0):
        k = k_ref[pl.ds(k0, blk), h * hd:(h + 1) * hd]
        return _dot_nt(qs[h], k) * scale + (cqs[h] - c_ref[h, :, pl.ds(k0, blk)])

    def update(h, state, s, k0):
        m, l, acc = state
        m_new = jnp.maximum(m, jnp.max(s, axis=1, keepdims=True))
        alpha = jnp.exp(m - m_new)
        p = jnp.exp(s - m_new)
        l = l * alpha + jnp.sum(p, axis=1, keepdims=True)
        v = v_ref[pl.ds(k0, blk), h * hd:(h + 1) * hd]
        acc = acc * alpha + jnp.dot(p.astype(BF16), v, preferred_element_type=F32)
        return m_new, l, acc

    def body(j, carry):
        k0 = pl.multiple_of(j * blk, blk)
        ss = [scores(h, k0) for h in range(hg)]
        return tuple(update(h, carry[h], ss[h], k0) for h in range(hg))

    init = tuple((jnp.full((blk, 1), NEG_INF, F32), jnp.zeros((blk, 1), F32), jnp.zeros((blk, hd), F32))
                 for _ in range(hg))
    carry = lax.fori_loop(0, i, body, init)
    causal = (lax.broadcasted_iota(jnp.int32, (blk, blk), 1) <= lax.broadcasted_iota(jnp.int32, (blk, blk), 0))
    for h in range(hg):
        s = jnp.where(causal, scores(h, q0), NEG_INF)
        _, l, acc = update(h, carry[h], s, q0)
        o_ref[:, h * hd:(h + 1) * hd] = (acc / l).astype(o_ref.dtype)


def fox_prompt(qn, kn, v, c_row, batch, seq, blk, hg):
    nq = seq // blk
    ng = FOX_HEADS // hg
    w = hg * FOX_HD
    kv_spec = pl.BlockSpec((seq, w), lambda b, h, i: (b, h))
    return pl.pallas_call(
        functools.partial(_fox_prompt_kernel, blk=blk, hg=hg, scale=FOX_HD ** -0.5),
        grid=(batch, ng, nq),
        in_specs=[pl.BlockSpec((blk, w), lambda b, h, i: (b * nq + i, h)), kv_spec, kv_spec,
                  pl.BlockSpec((hg, 1, seq), lambda b, h, i: (b * ng + h, 0, 0))],
        out_specs=pl.BlockSpec((blk, w), lambda b, h, i: (b * nq + i, h)),
        out_shape=jax.ShapeDtypeStruct((batch * seq, FOX_W), BF16),
        compiler_params=_cparams("parallel", "parallel", "arbitrary"),
        name="fox_prompt",
    )(qn, kn, v, c_row)


def _dn_prompt_kernel(q_ref, k_ref, v_ref, z_ref, wq_ref, wk_ref, wv_ref, g_ref, b_ref, ng_ref,
                      o_ref, s_out_ref, s_ref, xq_ref, xk_ref, xv_ref, *, tc, chunk, hg):
    t = pl.program_id(2)
    nc = tc // chunk
    pad = 8

    @pl.when(t == 0)
    def _():
        s_ref[...] = jnp.zeros_like(s_ref)
        for buf in (xq_ref, xk_ref, xv_ref):
            buf[0:pad, :] = jnp.zeros((pad, hg * DN_D), F32)

    def conv_silu(x_ref, w_ref, buf_ref):
        u = x_ref[...]
        buf_ref[pad:pad + tc, :] = u
        w = w_ref[...]
        out = buf_ref[pad - 3:pad - 3 + tc, :] * w[0:1, :]
        out = out + buf_ref[pad - 2:pad - 2 + tc, :] * w[1:2, :]
        out = out + buf_ref[pad - 1:pad - 1 + tc, :] * w[2:3, :]
        out = out + u * w[3:4, :]
        buf_ref[0:pad, :] = u[tc - pad:tc, :]
        return out * _sigmoid(out)

    def l2norm(x):
        return x * lax.rsqrt(jnp.sum(x * x, axis=-1, keepdims=True) + EPS)

    qc = conv_silu(q_ref, wq_ref, xq_ref)
    kc = conv_silu(k_ref, wk_ref, xk_ref)
    vc = conv_silu(v_ref, wv_ref, xv_ref)

    ra = lax.broadcasted_iota(jnp.int32, (chunk, chunk), 0)
    rb = lax.broadcasted_iota(jnp.int32, (chunk, chunk), 1)
    tri = ra >= rb
    strict = ra > rb
    eye_f = (ra == rb).astype(F32)
    upper_f = (ra <= rb).astype(F32)

    heads = []
    for hh in range(hg):
        cs = slice(hh * DN_D, (hh + 1) * DN_D)
        heads.append(dict(
            q=l2norm(qc[:, cs]) * (DN_D ** -0.5), k=l2norm(kc[:, cs]), v=vc[:, cs],
            gc=_dot_hi(g_ref[hh], upper_f),
            beta=b_ref[hh], s=s_ref[hh], outs=[]))

    work = []
    for c in range(nc):
        sl = slice(c * chunk, (c + 1) * chunk)
        for hd in heads:
            q, k, v = hd['q'][sl], hd['k'][sl], hd['v'][sl]
            gc_row = hd['gc'][c:c + 1, :]
            gc = _row_to_col(gc_row, chunk)
            beta = _row_to_col(hd['beta'][c:c + 1, :], chunk)
            decay = jnp.where(tri, jnp.exp(jnp.where(tri, gc - gc_row, 0.0)), 0.0)
            qk_b = jnp.concatenate([q, k], axis=0).astype(BF16)
            work.append(dict(hd=hd, q=q, k=k, v=v, gc=gc, beta=beta, decay=decay, qk_b=qk_b,
                             gc_last=gc_row[:, chunk - 1:chunk]))
    for wk in work:
        wk['gram'] = _dot_nt(wk['qk_b'], wk['qk_b'][chunk:, :])
    for wk in work:
        lower = jnp.where(strict, wk['beta'] * wk['gram'][chunk:, :] * wk['decay'], 0.0)
        wk['inv'] = eye_f - lower
        wk['power'] = _split_bf16(lower)
    for wk in work:
        wk['power'] = _split_bf16(_dot_split(wk['power'], wk['power']))
    span = 2
    while span < chunk:
        for wk in work:
            wk['inv'] = wk['inv'] + _dot_split(_split_bf16(wk['inv']), wk['power'])
        span *= 2
        if span < chunk:
            for wk in work:
                wk['power'] = _split_bf16(_dot_split(wk['power'], wk['power']))
    for wk in work:
        egc = jnp.exp(wk['gc'])
        k, beta = wk['k'], wk['beta']
        uw = _dot_split(_split_bf16(wk['inv']),
                        _split_bf16(jnp.concatenate([wk['v'] * beta, k * (beta * egc)], axis=1)))
        wk['u'] = uw[:, :DN_D]
        wk['lhs1'] = jnp.concatenate([uw[:, DN_D:], wk['q'] * egc], axis=0).astype(BF16)
        qk = jnp.where(tri, wk['gram'][:chunk, :] * wk['decay'], 0.0)
        k_dec = k * jnp.exp(wk['gc_last'] - wk['gc'])
        wk['lhs2'] = jnp.concatenate([qk, k_dec.T], axis=0).astype(BF16)
        wk['g_last'] = jnp.exp(wk['gc_last'])
    for wk in work:
        hd = wk['hd']
        s = hd['s']
        ws = jnp.dot(wk['lhs1'], s.astype(BF16), preferred_element_type=F32)
        v_new = wk['u'] - ws[:chunk, :]
        upd = jnp.dot(wk['lhs2'], v_new.astype(BF16), preferred_element_type=F32)
        hd['outs'].append(ws[chunk:, :] + upd[:chunk, :])
        hd['s'] = s * wk['g_last'] + upd[chunk:, :]

    ng = ng_ref[...]
    for hh, hd in enumerate(heads):
        cs = slice(hh * DN_D, (hh + 1) * DN_D)
        s_ref[hh] = hd['s']
        s_out_ref[hh] = hd['s']
        o = jnp.concatenate(hd['outs'], axis=0)
        o = o * lax.rsqrt(jnp.mean(o * o, axis=-1, keepdims=True) + EPS) * ng
        z = z_ref[:, cs]
        o_ref[:, cs] = (o * (z * _sigmoid(z))).astype(o_ref.dtype)


def dn_prompt(p, conv_w, g_chunks, beta_chunks, norm_g, batch, seq, tc, hg):
    nt = seq // tc
    nc = tc // DN_CHUNK
    ng = DN_HEADS // hg
    w = hg * DN_D

    def col(off):
        return pl.BlockSpec((tc, w), lambda b, h, t: (b * nt + t, off * ng + h))

    def wcol(off):
        return pl.BlockSpec((DN_CONV, w), lambda b, h, t: (0, off * ng + h))

    gspec = pl.BlockSpec((hg, nc, DN_CHUNK), lambda b, h, t: (b * ng + h, t, 0))
    return pl.pallas_call(
        functools.partial(_dn_prompt_kernel, tc=tc, chunk=DN_CHUNK, hg=hg),
        grid=(batch, ng, nt),
        in_specs=[col(0), col(1), col(2), col(3), wcol(0), wcol(1), wcol(2), gspec, gspec,
                  pl.BlockSpec((1, DN_D), lambda b, h, t: (0, 0))],
        out_specs=[pl.BlockSpec((tc, w), lambda b, h, t: (b * nt + t, h)),
                   pl.BlockSpec((hg, DN_D, DN_D), lambda b, h, t: (b * ng + h, 0, 0))],
        out_shape=[jax.ShapeDtypeStruct((batch * seq, DN_QK), BF16),
                   jax.ShapeDtypeStruct((batch * DN_HEADS, DN_D, DN_D), F32)],
        scratch_shapes=[pltpu.VMEM((hg, DN_D, DN_D), F32)] + [pltpu.VMEM((tc + 8, w), F32)] * 3,
        compiler_params=_cparams("parallel", "parallel", "arbitrary"),
        name="dn_prompt",
    )(p, p, p, p, conv_w, conv_w, conv_w, g_chunks, beta_chunks, norm_g.reshape(1, DN_D))


def _dn_sample_kernel(q_ref, k_ref, v_ref, z_ref, cq_ref, ck_ref, cv_ref, wq_ref, wk_ref, wv_ref,
                      g_ref, b_ref, ng_ref, s_ref, o_ref, s_out_ref, *, bb):
    def conv_silu(x_ref, c_ref, w_ref):
        w = w_ref[...]
        out = c_ref[:, 0, :] * w[0:1, :]
        out = out + c_ref[:, 1, :] * w[1:2, :]
        out = out + c_ref[:, 2, :] * w[2:3, :]
        out = out + x_ref[...] * w[3:4, :]
        return out * _sigmoid(out)

    def l2norm(x):
        return x * lax.rsqrt(jnp.sum(x * x, axis=-1, keepdims=True) + EPS)

    q = l2norm(conv_silu(q_ref, cq_ref, wq_ref)) * (DN_D ** -0.5)
    k = l2norm(conv_silu(k_ref, ck_ref, wk_ref))
    v = conv_silu(v_ref, cv_ref, wv_ref)
    q_t = q.T
    k_t = k.T
    decay = jnp.exp(g_ref[0])
    beta = b_ref[0]
    rows = []
    for b in range(bb):
        s = s_ref[b, 0] * decay[b:b + 1, :]
        k_col = k_t[:, b:b + 1]
        kv = jnp.sum(k_col * s, axis=0, keepdims=True)
        s = s + k_col * ((v[b:b + 1, :] - kv) * beta[b:b + 1, :])
        s_out_ref[b, 0] = s
        rows.append(jnp.sum(q_t[:, b:b + 1] * s, axis=0, keepdims=True))
    o = jnp.concatenate(rows, axis=0)
    o = o * lax.rsqrt(jnp.mean(o * o, axis=-1, keepdims=True) + EPS) * ng_ref[...]
    z = z_ref[...]
    o_ref[...] = (o * (z * _sigmoid(z))).astype(o_ref.dtype)


def dn_sample(p, state_conv, conv_w, g_t, beta_t, norm_g, state, bb):
    nb = p.shape[0]
    hb = DN_HEADS

    def col(off):
        return pl.BlockSpec((bb, DN_D), lambda i, h: (i, off + h))

    def ccol(off):
        return pl.BlockSpec((bb, DN_CONV - 1, DN_D), lambda i, h: (i, 0, off + h))

    def wcol(off):
        return pl.BlockSpec((DN_CONV, DN_D), lambda i, h: (0, off + h))

    gspec = pl.BlockSpec((1, bb, 1), lambda i, h: (h, i, 0))
    sspec = pl.BlockSpec((bb, 1, DN_D, DN_D), lambda i, h: (i, h, 0, 0))
    return pl.pallas_call(
        functools.partial(_dn_sample_kernel, bb=bb),
        grid=(nb // bb, DN_HEADS),
        in_specs=[col(0), col(hb), col(2 * hb), col(3 * hb), ccol(0), ccol(hb), ccol(2 * hb),
                  wcol(0), wcol(hb), wcol(2 * hb), gspec, gspec,
                  pl.BlockSpec((1, DN_D), lambda i, h: (0, 0)), sspec],
        out_specs=[pl.BlockSpec((bb, DN_D), lambda i, h: (i, h)), sspec],
        out_shape=[jax.ShapeDtypeStruct((nb, DN_QK), BF16), jax.ShapeDtypeStruct(state.shape, F32)],
        compiler_params=_cparams("parallel", "parallel"),
        name="dn_sample",
    )(p, p, p, p, state_conv, state_conv, state_conv, conv_w, conv_w, conv_w, g_t, beta_t,
      norm_g.reshape(1, DN_D), state)


def _fox_sample_kernel(pt_ref, q_ref, kn_ref, vn_ref, lfn_ref, *refs, pp, scale):
    k_refs = refs[0:pp]
    v_refs = refs[pp:2 * pp]
    lf_refs = refs[2 * pp:3 * pp]
    o_ref = refs[3 * pp]
    m_ref, l_ref, acc_ref, carry_ref = refs[3 * pp + 1:]
    j = pl.program_id(1)
    qs = q_ref[0] * scale

    @pl.when(j == 0)
    def _():
        m_ref[...] = jnp.sum(qs * kn_ref[0], axis=1, keepdims=True)
        l_ref[...] = jnp.ones_like(l_ref)
        acc_ref[...] = vn_ref[0]
        carry_ref[...] = lfn_ref[0]

    ra = lax.broadcasted_iota(jnp.int32, (PAGE_SIZE, PAGE_SIZE), 0)
    rb = lax.broadcasted_iota(jnp.int32, (PAGE_SIZE, PAGE_SIZE), 1)
    later_f = (ra > rb).astype(F32)
    shape3 = (PAGE_SIZE, FOX_HEADS, FOX_HD)
    diag3 = lax.broadcasted_iota(jnp.int32, shape3, 0) == lax.broadcasted_iota(jnp.int32, shape3, 2)
    carry = carry_ref[...]
    logits = []
    for i in range(pp):
        lf = lf_refs[i][0]
        bias = _dot_hi(lf, later_f) + carry
        carry = carry + jnp.sum(lf, axis=1, keepdims=True)
        logits.append(jnp.sum(k_refs[i][0] * qs[None] + jnp.where(diag3, bias[None], 0.0),
                              axis=2, keepdims=True))
    carry_ref[...] = carry
    m_old = m_ref[...]
    m_new = m_old
    for s3 in logits:
        m_new = jnp.maximum(m_new, jnp.max(s3, axis=0))
    alpha = jnp.exp(m_old - m_new)
    l_new = l_ref[...] * alpha
    acc = acc_ref[...] * alpha
    for i, s3 in enumerate(logits):
        p3 = jnp.exp(s3 - m_new[None])
        l_new = l_new + jnp.sum(p3, axis=0)
        acc = acc + jnp.sum(p3 * v_refs[i][0], axis=0)
    l_ref[...] = l_new
    acc_ref[...] = acc
    m_ref[...] = m_new

    @pl.when(j == pl.num_programs(1) - 1)
    def _():
        o_ref[0] = (acc_ref[...] / l_ref[...]).astype(o_ref.dtype)


def fox_sample(q, k_new, v_new, lf_new, k_pool, v_pool, lf_pool_t, page_table, pp):
    nb, n_pages = page_table.shape
    assert n_pages % pp == 0
    steps = n_pages // pp

    def page_idx4(i):
        return lambda b, j, pt: (pt[b, n_pages - 1 - (j * pp + i)], 0, 0, 0)

    def page_idx3(i):
        return lambda b, j, pt: (pt[b, n_pages - 1 - (j * pp + i)], 0, 0)

    tok = lambda b, j, pt: (b, 0, 0)
    in_specs = [pl.BlockSpec((1, FOX_HEADS, FOX_HD), tok)] * 3 + [pl.BlockSpec((1, FOX_HEADS, 1), tok)]
    in_specs += [pl.BlockSpec((1, PAGE_SIZE, FOX_HEADS, FOX_HD), page_idx4(i)) for i in range(pp)]
    in_specs += [pl.BlockSpec((1, PAGE_SIZE, FOX_HEADS, FOX_HD), page_idx4(i)) for i in range(pp)]
    in_specs += [pl.BlockSpec((1, FOX_HEADS, PAGE_SIZE), page_idx3(i)) for i in range(pp)]
    grid_spec = pltpu.PrefetchScalarGridSpec(
        num_scalar_prefetch=1,
        grid=(nb, steps),
        in_specs=in_specs,
        out_specs=pl.BlockSpec((1, FOX_HEADS, FOX_HD), tok),
        scratch_shapes=[pltpu.VMEM((FOX_HEADS, 1), F32), pltpu.VMEM((FOX_HEADS, 1), F32),
                        pltpu.VMEM((FOX_HEADS, FOX_HD), F32), pltpu.VMEM((FOX_HEADS, 1), F32)],
    )
    return pl.pallas_call(
        functools.partial(_fox_sample_kernel, pp=pp, scale=FOX_HD ** -0.5),
        grid_spec=grid_spec,
        out_shape=jax.ShapeDtypeStruct((nb, FOX_HEADS, FOX_HD), BF16),
        compiler_params=_cparams("parallel", "arbitrary"),
        name="fox_sample",
    )(page_table, q, k_new, v_new, lf_new, *([k_pool] * pp), *([v_pool] * pp), *([lf_pool_t] * pp))


def _mem_prompt_kernel(q_ref, k_ref, v_ref, g_ref, o_ref, *, scale):
    g = g_ref[...]
    for h in range(MEM_HEADS):
        cs = slice(h * MEM_HD, (h + 1) * MEM_HD)
        q = q_ref[:, cs]
        qn = (q * lax.rsqrt(jnp.mean(q * q, axis=-1, keepdims=True) + EPS) * g).astype(BF16)
        s = _dot_nt(qn, k_ref[:, cs]) * scale
        p = jnp.exp(s - jnp.max(s, axis=1, keepdims=True))
        p = p / jnp.sum(p, axis=1, keepdims=True)
        o_ref[:, cs] = jnp.dot(p.astype(BF16), v_ref[:, cs], preferred_element_type=F32).astype(o_ref.dtype)


def mem_attend_prompt(p, mk, mv, qn_g, batch, seq, tq):
    nq = seq // tq
    kv = pl.BlockSpec((MEM_TOKENS, MEM_W), lambda b, i: (b, 0))
    return pl.pallas_call(
        functools.partial(_mem_prompt_kernel, scale=MEM_HD ** -0.5),
        grid=(batch, nq),
        in_specs=[pl.BlockSpec((tq, MEM_W), lambda b, i: (b * nq + i, 0)), kv, kv,
                  pl.BlockSpec((1, MEM_HD), lambda b, i: (0, 0))],
        out_specs=pl.BlockSpec((tq, MEM_W), lambda b, i: (b * nq + i, 0)),
        out_shape=jax.ShapeDtypeStruct((batch * seq, MEM_W), BF16),
        compiler_params=_cparams("parallel", "parallel"),
        name="mem_prompt",
    )(p, mk, mv, qn_g.reshape(1, MEM_HD))


def _mem_sample_kernel(q_ref, k_ref, v_ref, g_ref, o_ref, *, scale):
    q = q_ref[0]
    qn = q * lax.rsqrt(jnp.mean(q * q, axis=-1, keepdims=True) + EPS) * g_ref[...] * scale
    s3 = jnp.sum(k_ref[0] * qn[None], axis=2, keepdims=True)
    p3 = jnp.exp(s3 - jnp.max(s3, axis=0)[None])
    o = jnp.sum(p3 * v_ref[0], axis=0) / jnp.sum(p3, axis=0)
    o_ref[0] = o.astype(o_ref.dtype)


def mem_sample(q, mk, mv, qn_g):
    nb = q.shape[0]
    tok = pl.BlockSpec((1, MEM_HEADS, MEM_HD), lambda b: (b, 0, 0))
    kv = pl.BlockSpec((1, MEM_TOKENS, MEM_HEADS, MEM_HD), lambda b: (b, 0, 0, 0))
    return pl.pallas_call(
        functools.partial(_mem_sample_kernel, scale=MEM_HD ** -0.5),
        grid=(nb,),
        in_specs=[tok, kv, kv, pl.BlockSpec((1, MEM_HD), lambda b: (0, 0))],
        out_specs=tok,
        out_shape=jax.ShapeDtypeStruct((nb, MEM_HEADS, MEM_HD), BF16),
        compiler_params=_cparams("parallel"),
        name="mem_sample",
    )(q, mk, mv, qn_g.reshape(1, MEM_HD))


PEER_NO_RANK = float(PEER_NKEYS)


def _top_values(s, count, with_rank=False):
    rows = s.shape[0]
    idx = lax.broadcasted_iota(jnp.int32, s.shape, 0)
    rank = jnp.full(s.shape, PEER_NO_RANK, F32) if with_rank else None
    vals = []
    for r in range(count):
        m = jnp.max(s, axis=0, keepdims=True)
        picked = idx == jnp.min(jnp.where(s == m, idx, rows), axis=0, keepdims=True)
        s = jnp.where(picked, NEG_INF, s)
        if with_rank:
            rank = jnp.where(picked, float(r), rank)
        vals.append(m)
    return jnp.concatenate(vals, axis=0), rank


def _peer_stats_kernel(hn_ref, wq_ref, keys_ref, cnt_ref, g1_ref, r2_ref, e2_ref, st_ref, top_ref, rank_ref):
    nk = PEER_NKEYS
    kk = PEER_TOPK
    q = jnp.dot(hn_ref[...], wq_ref[...], preferred_element_type=F32)
    st_ref[...] = _dot_nt(keys_ref[...], q.astype(BF16))

    def half_body(c, carry):
        r0 = pl.multiple_of(c * nk, nk)
        t0 = pl.multiple_of(c * kk, kk)
        vals, rank = _top_values(st_ref[pl.ds(r0, nk), :], kk, with_rank=True)
        top_ref[pl.ds(t0, kk), :] = vals
        rank_ref[pl.ds(r0, nk), :] = rank
        return carry

    lax.fori_loop(0, 2 * PEER_HEADS, half_body, 0)

    def head_body(h, carry):
        t0 = pl.multiple_of(h * 2 * kk, 2 * kk)
        v1 = top_ref[pl.ds(t0, kk), :]
        v2 = top_ref[pl.ds(t0 + kk, kk), :]
        cand = jnp.concatenate([v1[0:1, :] + v2] + [v1[a:a + 1, :] + v2[0:8, :] for a in range(1, 8)]
                               + [v1[8:kk, :] + v2[0:1, :]], axis=0)
        cv, _ = _top_values(cand, kk)
        z = jnp.sum(jnp.exp(cv - cv[0:1, :]), axis=0, keepdims=True)
        tau = cv[kk - 1:kk, :]
        r0 = pl.multiple_of(h * 2 * nk, 2 * nk)
        s1 = st_ref[pl.ds(r0, nk), :]
        s2 = st_ref[pl.ds(r0 + nk, nk), :]
        rank1 = rank_ref[pl.ds(r0, nk), :]
        cnt = jnp.zeros_like(s1)
        for a in range(kk):
            n_hit = jnp.sum(((v1[a:a + 1, :] + v2) >= tau).astype(F32), axis=0, keepdims=True)
            cnt = jnp.where(rank1 == float(a), n_hit, cnt)
        cnt_ref[h] = cnt
        g1_ref[h] = jnp.exp(s1 - v1[0:1, :]) / z
        r2_ref[h] = rank_ref[pl.ds(r0 + nk, nk), :].astype(BF16)
        e2_ref[h] = jnp.exp(s2 - v2[0:1, :]).astype(BF16)
        return carry

    lax.fori_loop(0, PEER_HEADS, head_body, 0)


def peer_stats(hn, w_q, keys_t, mt):
    m = hn.shape[0]
    nrow = PEER_HEADS * 2 * PEER_NKEYS
    hspec = pl.BlockSpec((PEER_HEADS, PEER_NKEYS, mt), lambda i: (0, 0, i))
    f32_shape = jax.ShapeDtypeStruct((PEER_HEADS, PEER_NKEYS, m), F32)
    bf16_shape = jax.ShapeDtypeStruct((PEER_HEADS, PEER_NKEYS, m), BF16)
    return pl.pallas_call(
        _peer_stats_kernel,
        grid=(m // mt,),
        in_specs=[pl.BlockSpec((mt, D_MODEL), lambda i: (i, 0)),
                  pl.BlockSpec(w_q.shape, lambda i: (0, 0)),
                  pl.BlockSpec(keys_t.shape, lambda i: (0, 0))],
        out_specs=[hspec, hspec, hspec, hspec],
        out_shape=[f32_shape, f32_shape, bf16_shape, bf16_shape],
        scratch_shapes=[pltpu.VMEM((nrow, mt), F32), pltpu.VMEM((2 * PEER_HEADS * PEER_TOPK, mt), F32),
                        pltpu.VMEM((nrow, mt), F32)],
        compiler_params=_cparams("parallel"),
        name="peer_stats",
    )(hn, w_q, keys_t)


def _peer_gates(cnt_ref, g1_ref, r2_ref, e2_ref, gate_ref, n_a):
    mt = gate_ref.shape[1]
    rows = 16
    heads = range(PEER_HEADS)
    for a in range(n_a):
        for c in range(mt // LANE):
            cs = slice(c * LANE, (c + 1) * LANE)
            cnt = [jnp.broadcast_to(cnt_ref[a, h:h + 1, cs], (rows, LANE)).astype(BF16) for h in heads]
            g1 = [jnp.broadcast_to(g1_ref[a, h:h + 1, cs], (rows, LANE)).astype(BF16) for h in heads]
            for r in range(PEER_NKEYS // rows):
                rs = slice(r * rows, (r + 1) * rows)
                w = None
                for h in heads:
                    term = jnp.where(r2_ref[h, rs, cs] < cnt[h], e2_ref[h, rs, cs], 0.0) * g1[h]
                    w = term if w is None else w + term
                gate_ref[a * PEER_NKEYS + r * rows:a * PEER_NKEYS + (r + 1) * rows, cs] = w


def _peer_mix_kernel(hn_ref, res_ref, u_ref, v_ref, cnt_ref, g1_ref, cntn_ref, g1n_ref, r2_ref, e2_ref,
                     o_ref, gate_a_ref, gate_b_ref, *, et):
    e = pl.program_id(1)
    n_a = et // PEER_NKEYS

    @pl.when(e == 0)
    def _():
        o_ref[...] = res_ref[...]
        _peer_gates(cnt_ref, g1_ref, r2_ref, e2_ref, gate_a_ref, n_a)

    def step(gate_cur_ref, gate_next_ref):
        _peer_gates(cntn_ref, g1n_ref, r2_ref, e2_ref, gate_next_ref, n_a)
        act_t = _dot_nt(u_ref[...], hn_ref[...])
        gel_t = 0.5 * act_t * (1.0 + lax.erf(act_t * (0.5 ** 0.5)))
        hmat = (gate_cur_ref[...].astype(F32) * gel_t).astype(BF16).T
        o_ref[...] += jnp.dot(hmat, v_ref[...], preferred_element_type=F32)

    @pl.when(e % 2 == 0)
    def _():
        step(gate_a_ref, gate_b_ref)

    @pl.when(e % 2 == 1)
    def _():
        step(gate_b_ref, gate_a_ref)


def peer_mix(hn, res, u_tab, v_tab, cnt_t, g1_t, r2, e2, mt, et):
    m = hn.shape[0]
    n_exp = u_tab.shape[0]
    n_e = n_exp // et
    a_per = et // PEER_NKEYS
    once = pl.Buffered(1)
    sel1 = pl.BlockSpec((a_per, PEER_HEADS, mt), lambda i, e: (e, 0, i))
    sel1_next = pl.BlockSpec((a_per, PEER_HEADS, mt), lambda i, e: (jnp.minimum(e + 1, n_e - 1), 0, i))
    sel2 = pl.BlockSpec((PEER_HEADS, PEER_NKEYS, mt), lambda i, e: (0, 0, i), pipeline_mode=once)
    tab = pl.BlockSpec((et, D_MODEL), lambda i, e: (e, 0))
    return pl.pallas_call(
        functools.partial(_peer_mix_kernel, et=et),
        grid=(m // mt, n_e),
        in_specs=[pl.BlockSpec((mt, D_MODEL), lambda i, e: (i, 0), pipeline_mode=once),
                  pl.BlockSpec((mt, D_MODEL), lambda i, e: (i, 0), pipeline_mode=once), tab, tab,
                  sel1, sel1, sel1_next, sel1_next, sel2, sel2],
        out_specs=pl.BlockSpec((mt, D_MODEL), lambda i, e: (i, 0)),
        out_shape=jax.ShapeDtypeStruct((m, D_MODEL), F32),
        scratch_shapes=[pltpu.VMEM((et, mt), BF16), pltpu.VMEM((et, mt), BF16)],
        compiler_params=_cparams("parallel", "arbitrary"),
        name="peer_mix",
    )(hn, res, u_tab, v_tab, cnt_t, g1_t, cnt_t, g1_t, r2, e2)


def _prep_weights(w_in, w_mem_kv, w_out, peer_w_q, peer_sub_keys, peer_u, peer_v):
    w_dn = w_in[:, :_OFF_B].astype(BF16)
    w_fox = w_in[:, _OFF_FQ:_OFF_FF].astype(BF16)
    w_mq = w_in[:, _OFF_MQ:].astype(BF16)
    w_small = jnp.concatenate(
        [w_in[:, _OFF_B:_OFF_FQ], w_in[:, _OFF_FF:_OFF_MQ],
         jnp.zeros((D_MODEL, SMALL_COLS - 2 * DN_HEADS - FOX_HEADS), F32)], axis=1).astype(BF16)
    half = PEER_DKEY // 2
    pairs = 2 * PEER_HEADS
    sk = peer_sub_keys.reshape(pairs, PEER_NKEYS, half)
    eye = jnp.eye(pairs, dtype=F32)
    keys_t = (eye[:, None, :, None] * sk[:, :, None, :]).reshape(pairs * PEER_NKEYS, pairs * half).astype(BF16)
    return dict(w_dn=w_dn, w_fox=w_fox, w_mq=w_mq, w_small=w_small, w_mem_kv=w_mem_kv.astype(BF16), w_out=w_out.astype(BF16),
                peer_w_q=peer_w_q.astype(BF16), keys_t=keys_t, peer_u=peer_u.astype(BF16),
                peer_v=peer_v.astype(BF16))


def _tile(m, pref):
    return pref if m % pref == 0 else m


def _project(x2, ln_g, wts):
    m = x2.shape[0]
    xn = rmsnorm_rows(x2, ln_g, _tile(m, 256))
    tm = _tile(m, 1024)
    p_dn = matmul([xn], wts['w_dn'], tm, 512)
    p_fox = matmul([xn], wts['w_fox'], tm, 512)
    p_mq = matmul([xn], wts['w_mq'], tm, 512)
    ps = matmul([xn], wts['w_small'], tm, SMALL_COLS)
    return p_dn, p_fox, p_mq, ps


def _channel_mix(x2, o_dn, o_fox, o_mem, wts, ln_ffn_g):
    m = x2.shape[0]
    h = matmul([o_dn, o_fox, o_mem], wts['w_out'], _tile(m, 1024), 512, residual=x2)
    hn = rmsnorm_rows(h, ln_ffn_g, _tile(m, 256))
    cnt, g1, r2, e2 = peer_stats(hn, wts['peer_w_q'], wts['keys_t'], _tile(m, 256))
    cnt_t = jnp.swapaxes(cnt, 0, 1)
    g1_t = jnp.swapaxes(g1, 0, 1)
    return peer_mix(hn, h, wts['peer_u'], wts['peer_v'], cnt_t, g1_t, r2, e2, _tile(m, 512), 512)


def kernel(x_prompt, x_sample, cache_fox_k, cache_fox_v, cache_fox_logf, state_delta, state_conv, cache_mem_k, cache_mem_v, page_table, mem_prompt, ln_mix_g, w_in, conv_w, dn_a_log, dn_dt_bias, dn_norm_g, fox_f_bias, fox_qn_g, fox_kn_g, ln_mem_g, w_mem_kv, mem_qn_g, mem_kn_g, w_out, ln_ffn_g, peer_w_q, peer_sub_keys, peer_u, peer_v):
    depth = w_in.shape[0]
    assert depth == 1
    l = 0
    batch, seq, _ = x_prompt.shape
    nb = x_sample.shape[0]
    m_p = batch * seq
    wts = _prep_weights(w_in[l], w_mem_kv[l], w_out[l], peer_w_q[l], peer_sub_keys[l], peer_u[l], peer_v[l])

    x2 = x_prompt.reshape(m_p, D_MODEL)
    p, p_fox, p_mq, ps = _project(x2, ln_mix_g[l], wts)
    gt = gates(ps, dn_a_log[l], dn_dt_bias[l], fox_f_bias[l], 1024)
    beta = gt[:, :DN_HEADS]
    gdec = gt[:, DN_HEADS:2 * DN_HEADS]
    logf = gt[:, 2 * DN_HEADS:2 * DN_HEADS + FOX_HEADS]

    def to_chunks(a):
        return a.reshape(batch, seq, DN_HEADS).transpose(0, 2, 1).reshape(batch * DN_HEADS, seq // DN_CHUNK, DN_CHUNK)

    o_dn, dn_state = dn_prompt(p, conv_w[l], to_chunks(gdec), to_chunks(beta), dn_norm_g[l], batch, seq, 512, 4)

    fqn = head_rmsnorm(p_fox, 0, fox_qn_g[l], FOX_HEADS, FOX_HD, 512, out_dtype=BF16)
    fkn = head_rmsnorm(p_fox, 1, fox_kn_g[l], FOX_HEADS, FOX_HD, 512)
    fv = p_fox[:, 2 * FOX_W:]
    lf_rows = logf.reshape(batch, seq, FOX_HEADS).transpose(0, 2, 1).reshape(batch * FOX_HEADS * (seq // LANE), LANE)
    c_row = cumsum_time(lf_rows, seq // LANE).reshape(batch * FOX_HEADS, 1, seq)
    o_fox = fox_prompt(fqn, fkn.astype(BF16), fv.astype(BF16), c_row, batch, seq, 512, 2)

    mem2 = mem_prompt.reshape(batch * MEM_TOKENS, D_MODEL)
    memn = rmsnorm_rows(mem2, ln_mem_g[l], 256)
    mkv = matmul([memn], wts['w_mem_kv'], batch * MEM_TOKENS, 512)
    mk = head_rmsnorm(mkv, 0, mem_kn_g[l], MEM_HEADS, MEM_HD, 256)
    mv = mkv[:, MEM_W:]
    o_mem = mem_attend_prompt(p_mq, mk.astype(BF16), mv.astype(BF16), mem_qn_g[l], batch, seq, 512)

    y_p = _channel_mix(x2, o_dn, o_fox, o_mem, wts, ln_ffn_g[l])

    xs = x_sample.reshape(nb, D_MODEL)
    sp, sp_fox, sp_mq, sps = _project(xs, ln_mix_g[l], wts)
    sgt = gates(sps, dn_a_log[l], dn_dt_bias[l], fox_f_bias[l], nb)
    s_beta = sgt[:, :DN_HEADS].T.reshape(DN_HEADS, nb, 1)
    s_g = sgt[:, DN_HEADS:2 * DN_HEADS].T.reshape(DN_HEADS, nb, 1)
    s_logf = sgt[:, 2 * DN_HEADS:2 * DN_HEADS + FOX_HEADS]
    so_dn, s_state = dn_sample(sp, state_conv[l], conv_w[l], s_g, s_beta, dn_norm_g[l], state_delta[l], 8)
    conv_s = jnp.concatenate([state_conv[l][:, 1:, :], sp[:, None, :CONV_CH]], axis=1)

    sfq = head_rmsnorm(sp_fox, 0, fox_qn_g[l], FOX_HEADS, FOX_HD, nb)
    sfk = head_rmsnorm(sp_fox, 1, fox_kn_g[l], FOX_HEADS, FOX_HD, nb)
    sfv = sp_fox[:, 2 * FOX_W:]
    so_fox = fox_sample(sfq.reshape(nb, FOX_HEADS, FOX_HD), sfk.reshape(nb, FOX_HEADS, FOX_HD),
                        sfv.reshape(nb, FOX_HEADS, FOX_HD), s_logf.reshape(nb, FOX_HEADS, 1),
                        cache_fox_k[l], cache_fox_v[l], jnp.swapaxes(cache_fox_logf[l], 1, 2),
                        page_table, 8).reshape(nb, FOX_W)
    smq = sp_mq.reshape(nb, MEM_HEADS, MEM_HD)
    so_mem = mem_sample(smq, cache_mem_k[l], cache_mem_v[l], mem_qn_g[l]).reshape(nb, MEM_W)
    y_s = _channel_mix(xs, so_dn, so_fox, so_mem, wts, ln_ffn_g[l])

    return (
        y_p.reshape(batch, seq, D_MODEL),
        y_s.reshape(nb, 1, D_MODEL),
        fkn.reshape(1, batch, seq, FOX_HEADS, FOX_HD),
        fv.reshape(1, batch, seq, FOX_HEADS, FOX_HD),
        logf.reshape(1, batch, seq, FOX_HEADS),
        dn_state.reshape(1, batch, DN_HEADS, DN_D, DN_D),
        p.reshape(batch, seq, P_DN_COLS)[:, seq - (DN_CONV - 1):, :CONV_CH][None],
        mk.reshape(1, batch, MEM_TOKENS, MEM_HEADS, MEM_HD),
        mv.reshape(1, batch, MEM_TOKENS, MEM_HEADS, MEM_HD),
        sfk.reshape(1, nb, 1, FOX_HEADS, FOX_HD),
        sfv.reshape(1, nb, 1, FOX_HEADS, FOX_HD),
        s_logf.reshape(1, nb, 1, FOX_HEADS),
        s_state[None],
        conv_s[None],
    )
```

```python
import functools

import jax
import jax.numpy as jnp
from jax import lax
from jax.experimental import pallas as pl
from jax.experimental.pallas import tpu as pltpu

F32 = jnp.float32
BF16 = jnp.bfloat16
HIGHEST = lax.Precision.HIGHEST
EPS = 1e-6
NEG_INF = float("-inf")

D_MODEL = 4096
DN_HEADS = 16
DN_D = 128
DN_CONV = 4
DN_CHUNK = 64
FOX_HEADS = 8
FOX_HD = 128
MEM_TOKENS = 256
MEM_HEADS = 4
MEM_HD = 256
PEER_HEADS = 8
PEER_NKEYS = 128
PEER_DKEY = 128
PEER_TOPK = 16
PAGE_SIZE = 128

DN_QK = DN_HEADS * DN_D
CONV_CH = 3 * DN_QK
FOX_W = FOX_HEADS * FOX_HD
MEM_W = MEM_HEADS * MEM_HD
_OFF_Z = CONV_CH
_OFF_B = _OFF_Z + DN_QK
_OFF_A = _OFF_B + DN_HEADS
_OFF_FQ = _OFF_A + DN_HEADS
_OFF_FK = _OFF_FQ + FOX_W
_OFF_FV = _OFF_FK + FOX_W
_OFF_FF = _OFF_FV + FOX_W
_OFF_MQ = _OFF_FF + FOX_HEADS
_IN_COLS = _OFF_MQ + MEM_W
P_DN_COLS = CONV_CH + DN_QK
LANE = 128
SMALL_COLS = LANE

VMEM_LIMIT_BYTES = 56 * 1024 * 1024


def _cparams(*sem):
    return pltpu.CompilerParams(dimension_semantics=sem, vmem_limit_bytes=VMEM_LIMIT_BYTES)


def _sigmoid(x):
    return 1.0 / (1.0 + jnp.exp(-x))


def _softplus(x):
    return jnp.maximum(x, 0.0) + jnp.log1p(jnp.exp(-jnp.abs(x)))


def _dot_hi(a, b):
    return jnp.dot(a, b, precision=HIGHEST, preferred_element_type=F32)


def _split_bf16(x):
    hi = x.astype(BF16)
    return hi, (x - hi.astype(F32)).astype(BF16)


def _dot_split(a, b):
    (ah, al), (bh, bl) = a, b
    small = jnp.dot(ah, bl, preferred_element_type=F32) + jnp.dot(al, bh, preferred_element_type=F32)
    return small + jnp.dot(ah, bh, preferred_element_type=F32)


def _dot_nt(a, b):
    return lax.dot_general(a, b, (((1,), (1,)), ((), ())), preferred_element_type=F32)


def _row_to_col(row, n):
    a = lax.broadcasted_iota(jnp.int32, (n, n), 0)
    b = lax.broadcasted_iota(jnp.int32, (n, n), 1)
    return jnp.sum(jnp.where(a == b, row, 0.0), axis=1, keepdims=True)


def _rmsnorm_kernel(x_ref, g_ref, o_ref):
    x = x_ref[...]
    y = x * lax.rsqrt(jnp.mean(x * x, axis=-1, keepdims=True) + EPS) * g_ref[...]
    o_ref[...] = y.astype(o_ref.dtype)


def rmsnorm_rows(x, g, tm, out_dtype=BF16):
    m, k = x.shape
    return pl.pallas_call(
        _rmsnorm_kernel,
        grid=(m // tm,),
        in_specs=[pl.BlockSpec((tm, k), lambda i: (i, 0)), pl.BlockSpec((1, k), lambda i: (0, 0))],
        out_specs=pl.BlockSpec((tm, k), lambda i: (i, 0)),
        out_shape=jax.ShapeDtypeStruct((m, k), out_dtype),
        compiler_params=_cparams("parallel"),
        name="rmsnorm_rows",
    )(x, g.reshape(1, k))


def _head_rmsnorm_kernel(x_ref, g_ref, o_ref, *, heads, hd):
    g = g_ref[...]
    for h in range(heads):
        x = x_ref[:, h * hd:(h + 1) * hd]
        y = x * lax.rsqrt(jnp.mean(x * x, axis=-1, keepdims=True) + EPS) * g
        o_ref[:, h * hd:(h + 1) * hd] = y.astype(o_ref.dtype)


def head_rmsnorm(x, col_block, g, heads, hd, tm, out_dtype=F32):
    m = x.shape[0]
    w = heads * hd
    return pl.pallas_call(
        functools.partial(_head_rmsnorm_kernel, heads=heads, hd=hd),
        grid=(m // tm,),
        in_specs=[pl.BlockSpec((tm, w), lambda i: (i, col_block)), pl.BlockSpec((1, hd), lambda i: (0, 0))],
        out_specs=pl.BlockSpec((tm, w), lambda i: (i, 0)),
        out_shape=jax.ShapeDtypeStruct((m, w), out_dtype),
        compiler_params=_cparams("parallel"),
        name="head_rmsnorm",
    )(x, g.reshape(1, hd))


def _matmul_kernel(*refs, ksizes, has_res):
    n_a = len(ksizes)
    w_ref = refs[n_a]
    o_ref = refs[-1]
    acc = None
    off = 0
    for a_ref, ks in zip(refs[:n_a], ksizes):
        part = jnp.dot(a_ref[...], w_ref[off:off + ks, :], preferred_element_type=F32)
        acc = part if acc is None else acc + part
        off += ks
    if has_res:
        acc = acc + refs[n_a + 1][...]
    o_ref[...] = acc


def matmul(a_list, w, tm, tn, residual=None):
    m = a_list[0].shape[0]
    k, n = w.shape
    ksizes = tuple(a.shape[1] for a in a_list)
    assert sum(ksizes) == k and m % tm == 0 and n % tn == 0
    in_specs = [pl.BlockSpec((tm, ks), lambda i, j: (i, 0)) for ks in ksizes]
    in_specs.append(pl.BlockSpec((k, tn), lambda i, j: (0, j)))
    args = list(a_list) + [w]
    if residual is not None:
        in_specs.append(pl.BlockSpec((tm, tn), lambda i, j: (i, j)))
        args.append(residual)
    return pl.pallas_call(
        functools.partial(_matmul_kernel, ksizes=ksizes, has_res=residual is not None),
        grid=(m // tm, n // tn),
        in_specs=in_specs,
        out_specs=pl.BlockSpec((tm, tn), lambda i, j: (i, j)),
        out_shape=jax.ShapeDtypeStruct((m, n), F32),
        compiler_params=_cparams("parallel", "parallel"),
        name="matmul",
    )(*args)


CAST_COLS = 512


def _cast_cols_kernel(*refs, lane_off):
    o_ref = refs[-1]
    if lane_off == 0:
        o_ref[...] = refs[0][...].astype(o_ref.dtype)
    else:
        x = jnp.concatenate([refs[0][...], refs[1][...]], axis=1)
        o_ref[...] = x[:, lane_off:lane_off + CAST_COLS].astype(o_ref.dtype)


def cast_columns(w, col_start, ncols, tr):
    k = w.shape[0]
    lane_off = col_start % LANE
    base = col_start - lane_off
    assert base % CAST_COLS == 0 and ncols % CAST_COLS == 0 and k % tr == 0
    wide = CAST_COLS // LANE
    in_specs = [pl.BlockSpec((tr, CAST_COLS), lambda i, j: (i, base // CAST_COLS + j))]
    args = [w]
    if lane_off:
        in_specs.append(pl.BlockSpec((tr, LANE), lambda i, j: (i, base // LANE + wide * (j + 1))))
        args.append(w)
    return pl.pallas_call(
        functools.partial(_cast_cols_kernel, lane_off=lane_off),
        grid=(k // tr, ncols // CAST_COLS),
        in_specs=in_specs,
        out_specs=pl.BlockSpec((tr, CAST_COLS), lambda i, j: (i, j)),
        out_shape=jax.ShapeDtypeStruct((k, ncols), BF16),
        compiler_params=_cparams("parallel", "parallel"),
        name="cast_columns",
    )(*args)


def _gates_kernel(p_ref, alog_ref, dtb_ref, fb_ref, o_ref):
    x = p_ref[...]
    lane = lax.broadcasted_iota(jnp.int32, x.shape, 1)
    beta = _sigmoid(x)
    g = -jnp.exp(alog_ref[...]) * _softplus(x + dtb_ref[...])
    logf = -_softplus(-(x + fb_ref[...]))
    out = jnp.where(lane < DN_HEADS, beta,
                    jnp.where(lane < 2 * DN_HEADS, g,
                              jnp.where(lane < 2 * DN_HEADS + FOX_HEADS, logf, 0.0)))
    o_ref[...] = out


def gates(p_small, a_log, dt_bias, f_bias, tm):
    m = p_small.shape[0]

    def pad(v, off):
        return jnp.zeros((1, SMALL_COLS), F32).at[0, off:off + v.shape[0]].set(v)

    row = pl.BlockSpec((1, SMALL_COLS), lambda i: (0, 0))
    return pl.pallas_call(
        _gates_kernel,
        grid=(m // tm,),
        in_specs=[pl.BlockSpec((tm, SMALL_COLS), lambda i: (i, 0)), row, row, row],
        out_specs=pl.BlockSpec((tm, SMALL_COLS), lambda i: (i, 0)),
        out_shape=jax.ShapeDtypeStruct((m, SMALL_COLS), F32),
        compiler_params=_cparams("parallel"),
        name="gates",
    )(p_small, pad(a_log, DN_HEADS), pad(dt_bias, DN_HEADS), pad(f_bias, 2 * DN_HEADS))


def _cumsum_kernel(x_ref, o_ref, *, blocks_per_group):
    x = x_ref[...]
    r, n = x.shape
    a = lax.broadcasted_iota(jnp.int32, (n, n), 0)
    b = lax.broadcasted_iota(jnp.int32, (n, n), 1)
    local = _dot_hi(x, (a <= b).astype(F32))
    tot = jnp.broadcast_to(local[:, n - 1:n], (r, n))
    ra = lax.broadcasted_iota(jnp.int32, (r, r), 0)
    rb = lax.broadcasted_iota(jnp.int32, (r, r), 1)
    earlier = jnp.logical_and(rb < ra, (ra // blocks_per_group) == (rb // blocks_per_group)).astype(F32)
    o_ref[...] = local + _dot_hi(earlier, tot)


def cumsum_time(x, blocks_per_group):
    return pl.pallas_call(
        functools.partial(_cumsum_kernel, blocks_per_group=blocks_per_group),
        out_shape=jax.ShapeDtypeStruct(x.shape, F32),
        compiler_params=pltpu.CompilerParams(vmem_limit_bytes=VMEM_LIMIT_BYTES),
        name="cumsum_time",
    )(x)


def _fox_prompt_kernel(q_ref, k_ref, v_ref, c_ref, o_ref, *, blk, hg, scale):
    i = pl.program_id(2)
    q0 = pl.multiple_of(i * blk, blk)
    hd = FOX_HD
    qs = [q_ref[:, h * hd:(h + 1) * hd] for h in range(hg)]
    cqs = [_row_to_col(c_ref[h, :, pl.ds(q0, blk)], blk) for h in range(hg)]

    def scores(h, k0):
        k = k_ref[pl.ds(k0, blk), h * hd:(h + 1) * hd]
        return _dot_nt(qs[h], k) * scale + (cqs[h] - c_ref[h, :, pl.ds(k0, blk)])

    def update(h, state, s, k0):
        m, l, acc = state
        m_new = jnp.maximum(m, jnp.max(s, axis=1, keepdims=True))
        alpha = jnp.exp(m - m_new)
        p = jnp.exp(s - m_new)
        l = l * alpha + jnp.sum(p, axis=1, keepdims=True)
        v = v_ref[pl.ds(k0, blk), h * hd:(h + 1) * hd]
        acc = acc * alpha + jnp.dot(p.astype(BF16), v, preferred_element_type=F32)
        return m_new, l, acc

    def body(j, carry):
        k0 = pl.multiple_of(j * blk, blk)
        ss = [scores(h, k0) for h in range(hg)]
        return tuple(update(h, carry[h], ss[h], k0) for h in range(hg))

    init = tuple((jnp.full((blk, 1), NEG_INF, F32), jnp.zeros((blk, 1), F32), jnp.zeros((blk, hd), F32))
                 for _ in range(hg))
    carry = lax.fori_loop(0, i, body, init)
    causal = (lax.broadcasted_iota(jnp.int32, (blk, blk), 1) <= lax.broadcasted_iota(jnp.int32, (blk, blk), 0))
    for h in range(hg):
        s = jnp.where(causal, scores(h, q0), NEG_INF)
        _, l, acc = update(h, carry[h], s, q0)
        o_ref[:, h * hd:(h + 1) * hd] = (acc / l).astype(o_ref.dtype)


def fox_prompt(qn, kn, v, c_row, batch, seq, blk, hg):
    nq = seq // blk
    ng = FOX_HEADS // hg
    w = hg * FOX_HD
    kv_spec = pl.BlockSpec((seq, w), lambda b, h, i: (b, h))
    return pl.pallas_call(
        functools.partial(_fox_prompt_kernel, blk=blk, hg=hg, scale=FOX_HD ** -0.5),
        grid=(batch, ng, nq),
        in_specs=[pl.BlockSpec((blk, w), lambda b, h, i: (b * nq + i, h)), kv_spec, kv_spec,
                  pl.BlockSpec((hg, 1, seq), lambda b, h, i: (b * ng + h, 0, 0))],
        out_specs=pl.BlockSpec((blk, w), lambda b, h, i: (b * nq + i, h)),
        out_shape=jax.ShapeDtypeStruct((batch * seq, FOX_W), BF16),
        compiler_params=_cparams("parallel", "parallel", "arbitrary"),
        name="fox_prompt",
    )(qn, kn, v, c_row)


def _dn_prompt_kernel(q_ref, k_ref, v_ref, z_ref, wq_ref, wk_ref, wv_ref, g_ref, b_ref, ng_ref,
                      o_ref, s_out_ref, s_ref, xq_ref, xk_ref, xv_ref, *, tc, chunk, hg):
    t = pl.program_id(2)
    nc = tc // chunk
    pad = 8

    @pl.when(t == 0)
    def _():
        s_ref[...] = jnp.zeros_like(s_ref)
        for buf in (xq_ref, xk_ref, xv_ref):
            buf[0:pad, :] = jnp.zeros((pad, hg * DN_D), F32)

    def conv_silu(x_ref, w_ref, buf_ref):
        u = x_ref[...]
        buf_ref[pad:pad + tc, :] = u
        w = w_ref[...]
        out = buf_ref[pad - 3:pad - 3 + tc, :] * w[0:1, :]
        out = out + buf_ref[pad - 2:pad - 2 + tc, :] * w[1:2, :]
        out = out + buf_ref[pad - 1:pad - 1 + tc, :] * w[2:3, :]
        out = out + u * w[3:4, :]
        buf_ref[0:pad, :] = u[tc - pad:tc, :]
        return out * _sigmoid(out)

    def l2norm(x):
        return x * lax.rsqrt(jnp.sum(x * x, axis=-1, keepdims=True) + EPS)

    qc = conv_silu(q_ref, wq_ref, xq_ref)
    kc = conv_silu(k_ref, wk_ref, xk_ref)
    vc = conv_silu(v_ref, wv_ref, xv_ref)

    ra = lax.broadcasted_iota(jnp.int32, (chunk, chunk), 0)
    rb = lax.broadcasted_iota(jnp.int32, (chunk, chunk), 1)
    tri = ra >= rb
    strict = ra > rb
    eye_f = (ra == rb).astype(F32)
    upper_f = (ra <= rb).astype(F32)

    heads = []
    for hh in range(hg):
        cs = slice(hh * DN_D, (hh + 1) * DN_D)
        heads.append(dict(
            q=l2norm(qc[:, cs]) * (DN_D ** -0.5), k=l2norm(kc[:, cs]), v=vc[:, cs],
            gc=_dot_hi(g_ref[hh], upper_f),
            beta=b_ref[hh], s=s_ref[hh], outs=[]))

    work = []
    for c in range(nc):
        sl = slice(c * chunk, (c + 1) * chunk)
        for hd in heads:
            q, k, v = hd['q'][sl], hd['k'][sl], hd['v'][sl]
            gc_row = hd['gc'][c:c + 1, :]
            gc = _row_to_col(gc_row, chunk)
            beta = _row_to_col(hd['beta'][c:c + 1, :], chunk)
            decay = jnp.where(tri, jnp.exp(jnp.where(tri, gc - gc_row, 0.0)), 0.0)
            qk_b = jnp.concatenate([q, k], axis=0).astype(BF16)
            work.append(dict(hd=hd, q=q, k=k, v=v, gc=gc, beta=beta, decay=decay, qk_b=qk_b,
                             gc_last=gc_row[:, chunk - 1:chunk]))
    for wk in work:
        wk['gram'] = _dot_nt(wk['qk_b'], wk['qk_b'][chunk:, :])
    for wk in work:
        lower = jnp.where(strict, wk['beta'] * wk['gram'][chunk:, :] * wk['decay'], 0.0)
        wk['inv'] = eye_f - lower
        wk['power'] = _split_bf16(lower)
    for wk in work:
        wk['power'] = _split_bf16(_dot_split(wk['power'], wk['power']))
    span = 2
    while span < chunk:
        for wk in work:
            wk['inv'] = wk['inv'] + _dot_split(_split_bf16(wk['inv']), wk['power'])
        span *= 2
        if span < chunk:
            for wk in work:
                wk['power'] = _split_bf16(_dot_split(wk['power'], wk['power']))
    for wk in work:
        egc = jnp.exp(wk['gc'])
        k, beta = wk['k'], wk['beta']
        uw = _dot_split(_split_bf16(wk['inv']),
                        _split_bf16(jnp.concatenate([wk['v'] * beta, k * (beta * egc)], axis=1)))
        wk['u'] = uw[:, :DN_D]
        wk['lhs1'] = jnp.concatenate([uw[:, DN_D:], wk['q'] * egc], axis=0).astype(BF16)
        qk = jnp.where(tri, wk['gram'][:chunk, :] * wk['decay'], 0.0)
        k_dec = k * jnp.exp(wk['gc_last'] - wk['gc'])
        wk['lhs2'] = jnp.concatenate([qk, k_dec.T], axis=0).astype(BF16)
        wk['g_last'] = jnp.exp(wk['gc_last'])
    for wk in work:
        hd = wk['hd']
        s = hd['s']
        ws = jnp.dot(wk['lhs1'], s.astype(BF16), preferred_element_type=F32)
        v_new = wk['u'] - ws[:chunk, :]
        upd = jnp.dot(wk['lhs2'], v_new.astype(BF16), preferred_element_type=F32)
        hd['outs'].append(ws[chunk:, :] + upd[:chunk, :])
        hd['s'] = s * wk['g_last'] + upd[chunk:, :]

    ng = ng_ref[...]
    for hh, hd in enumerate(heads):
        cs = slice(hh * DN_D, (hh + 1) * DN_D)
        s_ref[hh] = hd['s']
        s_out_ref[hh] = hd['s']
        o = jnp.concatenate(hd['outs'], axis=0)
        o = o * lax.rsqrt(jnp.mean(o * o, axis=-1, keepdims=True) + EPS) * ng
        z = z_ref[:, cs]
        o_ref[:, cs] = (o * (z * _sigmoid(z))).astype(o_ref.dtype)


def dn_prompt(p, conv_w, g_chunks, beta_chunks, norm_g, batch, seq, tc, hg):
    nt = seq // tc
    nc = tc // DN_CHUNK
    ng = DN_HEADS // hg
    w = hg * DN_D

    def col(off):
        return pl.BlockSpec((tc, w), lambda b, h, t: (b * nt + t, off * ng + h))

    def wcol(off):
        return pl.BlockSpec((DN_CONV, w), lambda b, h, t: (0, off * ng + h))

    gspec = pl.BlockSpec((hg, nc, DN_CHUNK), lambda b, h, t: (b * ng + h, t, 0))
    return pl.pallas_call(
        functools.partial(_dn_prompt_kernel, tc=tc, chunk=DN_CHUNK, hg=hg),
        grid=(batch, ng, nt),
        in_specs=[col(0), col(1), col(2), col(3), wcol(0), wcol(1), wcol(2), gspec, gspec,
                  pl.BlockSpec((1, DN_D), lambda b, h, t: (0, 0))],
        out_specs=[pl.BlockSpec((tc, w), lambda b, h, t: (b * nt + t, h)),
                   pl.BlockSpec((hg, DN_D, DN_D), lambda b, h, t: (b * ng + h, 0, 0))],
        out_shape=[jax.ShapeDtypeStruct((batch * seq, DN_QK), BF16),
                   jax.ShapeDtypeStruct((batch * DN_HEADS, DN_D, DN_D), F32)],
        scratch_shapes=[pltpu.VMEM((hg, DN_D, DN_D), F32)] + [pltpu.VMEM((tc + 8, w), F32)] * 3,
        compiler_params=_cparams("parallel", "parallel", "arbitrary"),
        name="dn_prompt",
    )(p, p, p, p, conv_w, conv_w, conv_w, g_chunks, beta_chunks, norm_g.reshape(1, DN_D))


def _dn_sample_kernel(q_ref, k_ref, v_ref, z_ref, cq_ref, ck_ref, cv_ref, wq_ref, wk_ref, wv_ref,
                      g_ref, b_ref, ng_ref, s_ref, o_ref, s_out_ref, *, bb):
    def conv_silu(x_ref, c_ref, w_ref):
        w = w_ref[...]
        out = c_ref[:, 0, :] * w[0:1, :]
        out = out + c_ref[:, 1, :] * w[1:2, :]
        out = out + c_ref[:, 2, :] * w[2:3, :]
        out = out + x_ref[...] * w[3:4, :]
        return out * _sigmoid(out)

    def l2norm(x):
        return x * lax.rsqrt(jnp.sum(x * x, axis=-1, keepdims=True) + EPS)

    q = l2norm(conv_silu(q_ref, cq_ref, wq_ref)) * (DN_D ** -0.5)
    k = l2norm(conv_silu(k_ref, ck_ref, wk_ref))
    v = conv_silu(v_ref, cv_ref, wv_ref)
    q_t = q.T
    k_t = k.T
    decay = jnp.exp(g_ref[0])
    beta = b_ref[0]
    rows = []
    for b in range(bb):
        s = s_ref[b, 0] * decay[b:b + 1, :]
        k_col = k_t[:, b:b + 1]
        kv = jnp.sum(k_col * s, axis=0, keepdims=True)
        s = s + k_col * ((v[b:b + 1, :] - kv) * beta[b:b + 1, :])
        s_out_ref[b, 0] = s
        rows.append(jnp.sum(q_t[:, b:b + 1] * s, axis=0, keepdims=True))
    o = jnp.concatenate(rows, axis=0)
    o = o * lax.rsqrt(jnp.mean(o * o, axis=-1, keepdims=True) + EPS) * ng_ref[...]
    z = z_ref[...]
    o_ref[...] = (o * (z * _sigmoid(z))).astype(o_ref.dtype)


def dn_sample(p, state_conv, conv_w, g_t, beta_t, norm_g, state, bb):
    nb = p.shape[0]
    hb = DN_HEADS

    def col(off):
        return pl.BlockSpec((bb, DN_D), lambda i, h: (i, off + h))

    def ccol(off):
        return pl.BlockSpec((bb, DN_CONV - 1, DN_D), lambda i, h: (i, 0, off + h))

    def wcol(off):
        return pl.BlockSpec((DN_CONV, DN_D), lambda i, h: (0, off + h))

    gspec = pl.BlockSpec((1, bb, 1), lambda i, h: (h, i, 0))
    sspec = pl.BlockSpec((bb, 1, DN_D, DN_D), lambda i, h: (i, h, 0, 0))
    return pl.pallas_call(
        functools.partial(_dn_sample_kernel, bb=bb),
        grid=(nb // bb, DN_HEADS),
        in_specs=[col(0), col(hb), col(2 * hb), col(3 * hb), ccol(0), ccol(hb), ccol(2 * hb),
                  wcol(0), wcol(hb), wcol(2 * hb), gspec, gspec,
                  pl.BlockSpec((1, DN_D), lambda i, h: (0, 0)), sspec],
        out_specs=[pl.BlockSpec((bb, DN_D), lambda i, h: (i, h)), sspec],
        out_shape=[jax.ShapeDtypeStruct((nb, DN_QK), BF16), jax.ShapeDtypeStruct(state.shape, F32)],
        compiler_params=_cparams("parallel", "parallel"),
        name="dn_sample",
    )(p, p, p, p, state_conv, state_conv, state_conv, conv_w, conv_w, conv_w, g_t, beta_t,
      norm_g.reshape(1, DN_D), state)


def _fox_sample_kernel(pt_ref, q_ref, kn_ref, vn_ref, lfn_ref, *refs, pp, scale):
    k_refs = refs[0:pp]
    v_refs = refs[pp:2 * pp]
    lf_refs = refs[2 * pp:3 * pp]
    o_ref = refs[3 * pp]
    m_ref, l_ref, acc_ref, carry_ref = refs[3 * pp + 1:]
    j = pl.program_id(1)
    qs = q_ref[0] * scale

    @pl.when(j == 0)
    def _():
        m_ref[...] = jnp.sum(qs * kn_ref[0], axis=1, keepdims=True)
        l_ref[...] = jnp.ones_like(l_ref)
        acc_ref[...] = vn_ref[0]
        carry_ref[...] = lfn_ref[0]

    ra = lax.broadcasted_iota(jnp.int32, (PAGE_SIZE, PAGE_SIZE), 0)
    rb = lax.broadcasted_iota(jnp.int32, (PAGE_SIZE, PAGE_SIZE), 1)
    later_f = (ra > rb).astype(F32)
    shape3 = (PAGE_SIZE, FOX_HEADS, FOX_HD)
    diag3 = lax.broadcasted_iota(jnp.int32, shape3, 0) == lax.broadcasted_iota(jnp.int32, shape3, 2)
    carry = carry_ref[...]
    logits = []
    for i in range(pp):
        lf = lf_refs[i][0]
        bias = _dot_hi(lf, later_f) + carry
        carry = carry + jnp.sum(lf, axis=1, keepdims=True)
        logits.append(jnp.sum(k_refs[i][0] * qs[None] + jnp.where(diag3, bias[None], 0.0),
                              axis=2, keepdims=True))
    carry_ref[...] = carry
    m_old = m_ref[...]
    m_new = m_old
    for s3 in logits:
        m_new = jnp.maximum(m_new, jnp.max(s3, axis=0))
    alpha = jnp.exp(m_old - m_new)
    l_new = l_ref[...] * alpha
    acc = acc_ref[...] * alpha
    for i, s3 in enumerate(logits):
        p3 = jnp.exp(s3 - m_new[None])
        l_new = l_new + jnp.sum(p3, axis=0)
        acc = acc + jnp.sum(p3 * v_refs[i][0], axis=0)
    l_ref[...] = l_new
    acc_ref[...] = acc
    m_ref[...] = m_new

    @pl.when(j == pl.num_programs(1) - 1)
    def _():
        o_ref[0] = (acc_ref[...] / l_ref[...]).astype(o_ref.dtype)


def fox_sample(q, k_new, v_new, lf_new, k_pool, v_pool, lf_pool_t, page_table, pp):
    nb, n_pages = page_table.shape
    assert n_pages % pp == 0
    steps = n_pages // pp

    def page_idx4(i):
        return lambda b, j, pt: (pt[b, n_pages - 1 - (j * pp + i)], 0, 0, 0)

    def page_idx3(i):
        return lambda b, j, pt: (pt[b, n_pages - 1 - (j * pp + i)], 0, 0)

    tok = lambda b, j, pt: (b, 0, 0)
    in_specs = [pl.BlockSpec((1, FOX_HEADS, FOX_HD), tok)] * 3 + [pl.BlockSpec((1, FOX_HEADS, 1), tok)]
    in_specs += [pl.BlockSpec((1, PAGE_SIZE, FOX_HEADS, FOX_HD), page_idx4(i)) for i in range(pp)]
    in_specs += [pl.BlockSpec((1, PAGE_SIZE, FOX_HEADS, FOX_HD), page_idx4(i)) for i in range(pp)]
    in_specs += [pl.BlockSpec((1, FOX_HEADS, PAGE_SIZE), page_idx3(i)) for i in range(pp)]
    grid_spec = pltpu.PrefetchScalarGridSpec(
        num_scalar_prefetch=1,
        grid=(nb, steps),
        in_specs=in_specs,
        out_specs=pl.BlockSpec((1, FOX_HEADS, FOX_HD), tok),
        scratch_shapes=[pltpu.VMEM((FOX_HEADS, 1), F32), pltpu.VMEM((FOX_HEADS, 1), F32),
                        pltpu.VMEM((FOX_HEADS, FOX_HD), F32), pltpu.VMEM((FOX_HEADS, 1), F32)],
    )
    return pl.pallas_call(
        functools.partial(_fox_sample_kernel, pp=pp, scale=FOX_HD ** -0.5),
        grid_spec=grid_spec,
        out_shape=jax.ShapeDtypeStruct((nb, FOX_HEADS, FOX_HD), BF16),
        compiler_params=_cparams("parallel", "arbitrary"),
        name="fox_sample",
    )(page_table, q, k_new, v_new, lf_new, *([k_pool] * pp), *([v_pool] * pp), *([lf_pool_t] * pp))


def _mem_prompt_kernel(q_ref, k_ref, v_ref, g_ref, o_ref, *, scale):
    g = g_ref[...]
    for h in range(MEM_HEADS):
        cs = slice(h * MEM_HD, (h + 1) * MEM_HD)
        q = q_ref[:, cs]
        qn = (q * lax.rsqrt(jnp.mean(q * q, axis=-1, keepdims=True) + EPS) * g).astype(BF16)
        s = _dot_nt(qn, k_ref[:, cs]) * scale
        p = jnp.exp(s - jnp.max(s, axis=1, keepdims=True))
        p = p / jnp.sum(p, axis=1, keepdims=True)
        o_ref[:, cs] = jnp.dot(p.astype(BF16), v_ref[:, cs], preferred_element_type=F32).astype(o_ref.dtype)


def mem_attend_prompt(p, mk, mv, qn_g, batch, seq, tq):
    nq = seq // tq
    kv = pl.BlockSpec((MEM_TOKENS, MEM_W), lambda b, i: (b, 0))
    return pl.pallas_call(
        functools.partial(_mem_prompt_kernel, scale=MEM_HD ** -0.5),
        grid=(batch, nq),
        in_specs=[pl.BlockSpec((tq, MEM_W), lambda b, i: (b * nq + i, 0)), kv, kv,
                  pl.BlockSpec((1, MEM_HD), lambda b, i: (0, 0))],
        out_specs=pl.BlockSpec((tq, MEM_W), lambda b, i: (b * nq + i, 0)),
        out_shape=jax.ShapeDtypeStruct((batch * seq, MEM_W), BF16),
        compiler_params=_cparams("parallel", "parallel"),
        name="mem_prompt",
    )(p, mk, mv, qn_g.reshape(1, MEM_HD))


def _mem_sample_kernel(q_ref, k_ref, v_ref, g_ref, o_ref, *, scale):
    q = q_ref[0]
    qn = q * lax.rsqrt(jnp.mean(q * q, axis=-1, keepdims=True) + EPS) * g_ref[...] * scale
    s3 = jnp.sum(k_ref[0] * qn[None], axis=2, keepdims=True)
    p3 = jnp.exp(s3 - jnp.max(s3, axis=0)[None])
    o = jnp.sum(p3 * v_ref[0], axis=0) / jnp.sum(p3, axis=0)
    o_ref[0] = o.astype(o_ref.dtype)


def mem_sample(q, mk, mv, qn_g):
    nb = q.shape[0]
    tok = pl.BlockSpec((1, MEM_HEADS, MEM_HD), lambda b: (b, 0, 0))
    kv = pl.BlockSpec((1, MEM_TOKENS, MEM_HEADS, MEM_HD), lambda b: (b, 0, 0, 0))
    return pl.pallas_call(
        functools.partial(_mem_sample_kernel, scale=MEM_HD ** -0.5),
        grid=(nb,),
        in_specs=[tok, kv, kv, pl.BlockSpec((1, MEM_HD), lambda b: (0, 0))],
        out_specs=tok,
        out_shape=jax.ShapeDtypeStruct((nb, MEM_HEADS, MEM_HD), BF16),
        compiler_params=_cparams("parallel"),
        name="mem_sample",
    )(q, mk, mv, qn_g.reshape(1, MEM_HD))


PEER_NO_RANK = float(PEER_NKEYS)


def _top_values(s, count, with_rank=False):
    rows = s.shape[0]
    idx = lax.broadcasted_iota(jnp.int32, s.shape, 0)
    rank = jnp.full(s.shape, PEER_NO_RANK, F32) if with_rank else None
    vals = []
    for r in range(count):
        m = jnp.max(s, axis=0, keepdims=True)
        picked = idx == jnp.min(jnp.where(s == m, idx, rows), axis=0, keepdims=True)
        s = jnp.where(picked, NEG_INF, s)
        if with_rank:
            rank = jnp.where(picked, float(r), rank)
        vals.append(m)
    return jnp.concatenate(vals, axis=0), rank


def _peer_stats_kernel(hn_ref, wq_ref, keys_ref, cnt_ref, g1_ref, r2_ref, e2_ref, st_ref, top_ref, rank_ref,
                       cnt_scr, g1_scr):
    nk = PEER_NKEYS
    kk = PEER_TOPK
    q = jnp.dot(hn_ref[...], wq_ref[...], preferred_element_type=F32)
    st_ref[...] = _dot_nt(keys_ref[...], q.astype(BF16))

    def half_body(c, carry):
        r0 = pl.multiple_of(c * nk, nk)
        t0 = pl.multiple_of(c * kk, kk)
        vals, rank = _top_values(st_ref[pl.ds(r0, nk), :], kk, with_rank=True)
        top_ref[pl.ds(t0, kk), :] = vals
        rank_ref[pl.ds(r0, nk), :] = rank
        return carry

    lax.fori_loop(0, 2 * PEER_HEADS, half_body, 0)

    def head_body(h, carry):
        t0 = pl.multiple_of(h * 2 * kk, 2 * kk)
        v1 = top_ref[pl.ds(t0, kk), :]
        v2 = top_ref[pl.ds(t0 + kk, kk), :]
        cand = jnp.concatenate([v1[0:1, :] + v2] + [v1[a:a + 1, :] + v2[0:8, :] for a in range(1, 8)]
                               + [v1[8:kk, :] + v2[0:1, :]], axis=0)
        cv, _ = _top_values(cand, kk)
        z = jnp.sum(jnp.exp(cv - cv[0:1, :]), axis=0, keepdims=True)
        tau = cv[kk - 1:kk, :]
        r0 = pl.multiple_of(h * 2 * nk, 2 * nk)
        s1 = st_ref[pl.ds(r0, nk), :]
        s2 = st_ref[pl.ds(r0 + nk, nk), :]
        rank1 = rank_ref[pl.ds(r0, nk), :]
        cnt = jnp.zeros_like(s1)
        for a in range(kk):
            n_hit = jnp.sum(((v1[a:a + 1, :] + v2) >= tau).astype(F32), axis=0, keepdims=True)
            cnt = jnp.where(rank1 == float(a), n_hit, cnt)
        cnt_scr[h] = cnt
        g1_scr[h] = jnp.exp(s1 - v1[0:1, :]) / z
        r2_ref[h] = rank_ref[pl.ds(r0 + nk, nk), :].astype(BF16)
        e2_ref[h] = jnp.exp(s2 - v2[0:1, :]).astype(BF16)
        return carry

    lax.fori_loop(0, PEER_HEADS, head_body, 0)
    for h in range(PEER_HEADS):
        cnt_ref[:, h, :] = cnt_scr[h]
        g1_ref[:, h, :] = g1_scr[h]


def peer_stats(hn, w_q, keys_t, mt):
    m = hn.shape[0]
    nrow = PEER_HEADS * 2 * PEER_NKEYS
    hspec = pl.BlockSpec((PEER_HEADS, PEER_NKEYS, mt), lambda i: (0, 0, i))
    kspec = pl.BlockSpec((PEER_NKEYS, PEER_HEADS, mt), lambda i: (0, 0, i))
    f32_shape = jax.ShapeDtypeStruct((PEER_NKEYS, PEER_HEADS, m), F32)
    bf16_shape = jax.ShapeDtypeStruct((PEER_HEADS, PEER_NKEYS, m), BF16)
    return pl.pallas_call(
        _peer_stats_kernel,
        grid=(m // mt,),
        in_specs=[pl.BlockSpec((mt, D_MODEL), lambda i: (i, 0)),
                  pl.BlockSpec(w_q.shape, lambda i: (0, 0)),
                  pl.BlockSpec(keys_t.shape, lambda i: (0, 0))],
        out_specs=[kspec, kspec, hspec, hspec],
        out_shape=[f32_shape, f32_shape, bf16_shape, bf16_shape],
        scratch_shapes=[pltpu.VMEM((nrow, mt), F32), pltpu.VMEM((2 * PEER_HEADS * PEER_TOPK, mt), F32),
                        pltpu.VMEM((nrow, mt), F32), pltpu.VMEM((PEER_HEADS, PEER_NKEYS, mt), F32),
                        pltpu.VMEM((PEER_HEADS, PEER_NKEYS, mt), F32)],
        compiler_params=_cparams("parallel"),
        name="peer_stats",
    )(hn, w_q, keys_t)


def _peer_gates(cnt_ref, g1_ref, r2_ref, e2_ref, gate_ref, n_a):
    mt = gate_ref.shape[1]
    rows = 16
    heads = range(PEER_HEADS)
    for a in range(n_a):
        for c in range(mt // LANE):
            cs = slice(c * LANE, (c + 1) * LANE)
            cnt = [jnp.broadcast_to(cnt_ref[a, h:h + 1, cs], (rows, LANE)).astype(BF16) for h in heads]
            g1 = [jnp.broadcast_to(g1_ref[a, h:h + 1, cs], (rows, LANE)).astype(BF16) for h in heads]
            for r in range(PEER_NKEYS // rows):
                rs = slice(r * rows, (r + 1) * rows)
                w = None
                for h in heads:
                    term = jnp.where(r2_ref[h, rs, cs] < cnt[h], e2_ref[h, rs, cs], 0.0) * g1[h]
                    w = term if w is None else w + term
                gate_ref[a * PEER_NKEYS + r * rows:a * PEER_NKEYS + (r + 1) * rows, cs] = w


def _peer_mix_kernel(hn_ref, res_ref, u_ref, v_ref, cnt_ref, g1_ref, cntn_ref, g1n_ref, r2_ref, e2_ref,
                     o_ref, gate_a_ref, gate_b_ref, *, et):
    e = pl.program_id(1)
    n_a = et // PEER_NKEYS

    @pl.when(e == 0)
    def _():
        o_ref[...] = res_ref[...]
        _peer_gates(cnt_ref, g1_ref, r2_ref, e2_ref, gate_a_ref, n_a)

    def step(gate_cur_ref, gate_next_ref):
        _peer_gates(cntn_ref, g1n_ref, r2_ref, e2_ref, gate_next_ref, n_a)
        act_t = _dot_nt(u_ref[...], hn_ref[...])
        gel_t = 0.5 * act_t * (1.0 + lax.erf(act_t * (0.5 ** 0.5)))
        hmat = (gate_cur_ref[...].astype(F32) * gel_t).astype(BF16).T
        o_ref[...] += jnp.dot(hmat, v_ref[...], preferred_element_type=F32)

    @pl.when(e % 2 == 0)
    def _():
        step(gate_a_ref, gate_b_ref)

    @pl.when(e % 2 == 1)
    def _():
        step(gate_b_ref, gate_a_ref)


def peer_mix(hn, res, u_tab, v_tab, cnt_t, g1_t, r2, e2, mt, et):
    m = hn.shape[0]
    n_exp = u_tab.shape[0]
    n_e = n_exp // et
    a_per = et // PEER_NKEYS
    once = pl.Buffered(1)
    sel1 = pl.BlockSpec((a_per, PEER_HEADS, mt), lambda i, e: (e, 0, i))
    sel1_next = pl.BlockSpec((a_per, PEER_HEADS, mt), lambda i, e: (jnp.minimum(e + 1, n_e - 1), 0, i))
    sel2 = pl.BlockSpec((PEER_HEADS, PEER_NKEYS, mt), lambda i, e: (0, 0, i), pipeline_mode=once)
    tab = pl.BlockSpec((et, D_MODEL), lambda i, e: (e, 0))
    return pl.pallas_call(
        functools.partial(_peer_mix_kernel, et=et),
        grid=(m // mt, n_e),
        in_specs=[pl.BlockSpec((mt, D_MODEL), lambda i, e: (i, 0), pipeline_mode=once),
                  pl.BlockSpec((mt, D_MODEL), lambda i, e: (i, 0), pipeline_mode=once), tab, tab,
                  sel1, sel1, sel1_next, sel1_next, sel2, sel2],
        out_specs=pl.BlockSpec((mt, D_MODEL), lambda i, e: (i, 0)),
        out_shape=jax.ShapeDtypeStruct((m, D_MODEL), F32),
        scratch_shapes=[pltpu.VMEM((et, mt), BF16), pltpu.VMEM((et, mt), BF16)],
        compiler_params=_cparams("parallel", "arbitrary"),
        name="peer_mix",
    )(hn, res, u_tab, v_tab, cnt_t, g1_t, cnt_t, g1_t, r2, e2)


def _prep_weights(w_in, w_mem_kv, w_out, peer_w_q, peer_sub_keys, peer_u, peer_v):
    w_dn = cast_columns(w_in, 0, P_DN_COLS, 1024)
    w_fox = cast_columns(w_in, _OFF_FQ, 3 * FOX_W, 1024)
    w_mq = cast_columns(w_in, _OFF_MQ, MEM_W, 1024)
    w_small = jnp.concatenate(
        [w_in[:, _OFF_B:_OFF_FQ], w_in[:, _OFF_FF:_OFF_MQ],
         jnp.zeros((D_MODEL, SMALL_COLS - 2 * DN_HEADS - FOX_HEADS), F32)], axis=1).astype(BF16)
    half = PEER_DKEY // 2
    pairs = 2 * PEER_HEADS
    sk = peer_sub_keys.reshape(pairs, PEER_NKEYS, half)
    eye = jnp.eye(pairs, dtype=F32)
    keys_t = (eye[:, None, :, None] * sk[:, :, None, :]).reshape(pairs * PEER_NKEYS, pairs * half).astype(BF16)
    return dict(w_dn=w_dn, w_fox=w_fox, w_mq=w_mq, w_small=w_small, w_mem_kv=w_mem_kv.astype(BF16), w_out=w_out.astype(BF16),
                peer_w_q=peer_w_q.astype(BF16), keys_t=keys_t, peer_u=peer_u.astype(BF16),
                peer_v=peer_v.astype(BF16))


def _tile(m, pref):
    return pref if m % pref == 0 else m


def _project(x2, ln_g, wts):
    m = x2.shape[0]
    xn = rmsnorm_rows(x2, ln_g, _tile(m, 256))
    tm = _tile(m, 1024)
    p_dn = matmul([xn], wts['w_dn'], tm, 512)
    p_fox = matmul([xn], wts['w_fox'], tm, 512)
    p_mq = matmul([xn], wts['w_mq'], tm, 512)
    ps = matmul([xn], wts['w_small'], tm, SMALL_COLS)
    return p_dn, p_fox, p_mq, ps


def _channel_mix(x2, o_dn, o_fox, o_mem, wts, ln_ffn_g):
    m = x2.shape[0]
    h = matmul([o_dn, o_fox, o_mem], wts['w_out'], _tile(m, 1024), 512, residual=x2)
    hn = rmsnorm_rows(h, ln_ffn_g, _tile(m, 256))
    cnt_t, g1_t, r2, e2 = peer_stats(hn, wts['peer_w_q'], wts['keys_t'], _tile(m, 256))
    return peer_mix(hn, h, wts['peer_u'], wts['peer_v'], cnt_t, g1_t, r2, e2, _tile(m, 512), 512)


def kernel(x_prompt, x_sample, cache_fox_k, cache_fox_v, cache_fox_logf, state_delta, state_conv, cache_mem_k, cache_mem_v, page_table, mem_prompt, ln_mix_g, w_in, conv_w, dn_a_log, dn_dt_bias, dn_norm_g, fox_f_bias, fox_qn_g, fox_kn_g, ln_mem_g, w_mem_kv, mem_qn_g, mem_kn_g, w_out, ln_ffn_g, peer_w_q, peer_sub_keys, peer_u, peer_v):
    depth = w_in.shape[0]
    assert depth == 1
    l = 0
    batch, seq, _ = x_prompt.shape
    nb = x_sample.shape[0]
    m_p = batch * seq
    wts = _prep_weights(w_in[l], w_mem_kv[l], w_out[l], peer_w_q[l], peer_sub_keys[l], peer_u[l], peer_v[l])

    x2 = x_prompt.reshape(m_p, D_MODEL)
    p, p_fox, p_mq, ps = _project(x2, ln_mix_g[l], wts)
    gt = gates(ps, dn_a_log[l], dn_dt_bias[l], fox_f_bias[l], 1024)
    beta = gt[:, :DN_HEADS]
    gdec = gt[:, DN_HEADS:2 * DN_HEADS]
    logf = gt[:, 2 * DN_HEADS:2 * DN_HEADS + FOX_HEADS]

    def to_chunks(a):
        return a.reshape(batch, seq, DN_HEADS).transpose(0, 2, 1).reshape(batch * DN_HEADS, seq // DN_CHUNK, DN_CHUNK)

    o_dn, dn_state = dn_prompt(p, conv_w[l], to_chunks(gdec), to_chunks(beta), dn_norm_g[l], batch, seq, 512, 8)

    fqn = head_rmsnorm(p_fox, 0, fox_qn_g[l], FOX_HEADS, FOX_HD, 512, out_dtype=BF16)
    fkn = head_rmsnorm(p_fox, 1, fox_kn_g[l], FOX_HEADS, FOX_HD, 512)
    fv = p_fox[:, 2 * FOX_W:]
    lf_rows = logf.reshape(batch, seq, FOX_HEADS).transpose(0, 2, 1).reshape(batch * FOX_HEADS * (seq // LANE), LANE)
    c_row = cumsum_time(lf_rows, seq // LANE).reshape(batch * FOX_HEADS, 1, seq)
    o_fox = fox_prompt(fqn, fkn.astype(BF16), fv.astype(BF16), c_row, batch, seq, 512, 2)

    mem2 = mem_prompt.reshape(batch * MEM_TOKENS, D_MODEL)
    memn = rmsnorm_rows(mem2, ln_mem_g[l], 256)
    mkv = matmul([memn], wts['w_mem_kv'], batch * MEM_TOKENS, 512)
    mk = head_rmsnorm(mkv, 0, mem_kn_g[l], MEM_HEADS, MEM_HD, 256)
    mv = mkv[:, MEM_W:]
    o_mem = mem_attend_prompt(p_mq, mk.astype(BF16), mv.astype(BF16), mem_qn_g[l], batch, seq, 512)

    y_p = _channel_mix(x2, o_dn, o_fox, o_mem, wts, ln_ffn_g[l])

    xs = x_sample.reshape(nb, D_MODEL)
    sp, sp_fox, sp_mq, sps = _project(xs, ln_mix_g[l], wts)
    sgt = gates(sps, dn_a_log[l], dn_dt_bias[l], fox_f_bias[l], nb)
    s_beta = sgt[:, :DN_HEADS].T.reshape(DN_HEADS, nb, 1)
    s_g = sgt[:, DN_HEADS:2 * DN_HEADS].T.reshape(DN_HEADS, nb, 1)
    s_logf = sgt[:, 2 * DN_HEADS:2 * DN_HEADS + FOX_HEADS]
    so_dn, s_state = dn_sample(sp, state_conv[l], conv_w[l], s_g, s_beta, dn_norm_g[l], state_delta[l], 32)
    conv_s = jnp.concatenate([state_conv[l][:, 1:, :], sp[:, None, :CONV_CH]], axis=1)

    sfq = head_rmsnorm(sp_fox, 0, fox_qn_g[l], FOX_HEADS, FOX_HD, nb)
    sfk = head_rmsnorm(sp_fox, 1, fox_kn_g[l], FOX_HEADS, FOX_HD, nb)
    sfv = sp_fox[:, 2 * FOX_W:]
    so_fox = fox_sample(sfq.reshape(nb, FOX_HEADS, FOX_HD), sfk.reshape(nb, FOX_HEADS, FOX_HD),
                        sfv.reshape(nb, FOX_HEADS, FOX_HD), s_logf.reshape(nb, FOX_HEADS, 1),
                        cache_fox_k[l], cache_fox_v[l], jnp.swapaxes(cache_fox_logf[l], 1, 2),
                        page_table, 8).reshape(nb, FOX_W)
    smq = sp_mq.reshape(nb, MEM_HEADS, MEM_HD)
    so_mem = mem_sample(smq, cache_mem_k[l], cache_mem_v[l], mem_qn_g[l]).reshape(nb, MEM_W)
    y_s = _channel_mix(xs, so_dn, so_fox, so_mem, wts, ln_ffn_g[l])

    return (
        y_p.reshape(batch, seq, D_MODEL),
        y_s.reshape(nb, 1, D_MODEL),
        fkn.reshape(1, batch, seq, FOX_HEADS, FOX_HD),
        fv.reshape(1, batch, seq, FOX_HEADS, FOX_HD),
        logf.reshape(1, batch, seq, FOX_HEADS),
        dn_state.reshape(1, batch, DN_HEADS, DN_D, DN_D),
        p.reshape(batch, seq, P_DN_COLS)[:, seq - (DN_CONV - 1):, :CONV_CH][None],
        mk.reshape(1, batch, MEM_TOKENS, MEM_HEADS, MEM_HD),
        mv.reshape(1, batch, MEM_TOKENS, MEM_HEADS, MEM_HD),
        sfk.reshape(1, nb, 1, FOX_HEADS, FOX_HD),
        sfv.reshape(1, nb, 1, FOX_HEADS, FOX_HD),
        s_logf.reshape(1, nb, 1, FOX_HEADS),
        s_state[None],
        conv_s[None],
    )
```

```python
import functools

import jax
import jax.numpy as jnp
from jax import lax
from jax.experimental import pallas as pl
from jax.experimental.pallas import tpu as pltpu

F32 = jnp.float32
BF16 = jnp.bfloat16
HIGHEST = lax.Precision.HIGHEST
EPS = 1e-6
NEG_INF = float("-inf")

D_MODEL = 4096
DN_HEADS = 16
DN_D = 128
DN_CONV = 4
DN_CHUNK = 64
FOX_HEADS = 8
FOX_HD = 128
MEM_TOKENS = 256
MEM_HEADS = 4
MEM_HD = 256
PEER_HEADS = 8
PEER_NKEYS = 128
PEER_DKEY = 128
PEER_TOPK = 16
PAGE_SIZE = 128

DN_QK = DN_HEADS * DN_D
CONV_CH = 3 * DN_QK
FOX_W = FOX_HEADS * FOX_HD
MEM_W = MEM_HEADS * MEM_HD
_OFF_Z = CONV_CH
_OFF_B = _OFF_Z + DN_QK
_OFF_A = _OFF_B + DN_HEADS
_OFF_FQ = _OFF_A + DN_HEADS
_OFF_FK = _OFF_FQ + FOX_W
_OFF_FV = _OFF_FK + FOX_W
_OFF_FF = _OFF_FV + FOX_W
_OFF_MQ = _OFF_FF + FOX_HEADS
_IN_COLS = _OFF_MQ + MEM_W
P_DN_COLS = CONV_CH + DN_QK
LANE = 128
SMALL_COLS = LANE

VMEM_LIMIT_BYTES = 56 * 1024 * 1024


def _cparams(*sem):
    return pltpu.CompilerParams(dimension_semantics=sem, vmem_limit_bytes=VMEM_LIMIT_BYTES)


def _sigmoid(x):
    return 1.0 / (1.0 + jnp.exp(-x))


def _softplus(x):
    return jnp.maximum(x, 0.0) + jnp.log1p(jnp.exp(-jnp.abs(x)))


def _dot_hi(a, b):
    return jnp.dot(a, b, precision=HIGHEST, preferred_element_type=F32)


def _split_bf16(x):
    hi = x.astype(BF16)
    return hi, (x - hi.astype(F32)).astype(BF16)


def _dot_split(a, b):
    (ah, al), (bh, bl) = a, b
    small = jnp.dot(ah, bl, preferred_element_type=F32) + jnp.dot(al, bh, preferred_element_type=F32)
    return small + jnp.dot(ah, bh, preferred_element_type=F32)


def _dot_nt(a, b):
    return lax.dot_general(a, b, (((1,), (1,)), ((), ())), preferred_element_type=F32)


def _row_to_col(row, n):
    a = lax.broadcasted_iota(jnp.int32, (n, n), 0)
    b = lax.broadcasted_iota(jnp.int32, (n, n), 1)
    return jnp.sum(jnp.where(a == b, row, 0.0), axis=1, keepdims=True)


def _rmsnorm_kernel(x_ref, g_ref, o_ref):
    x = x_ref[...]
    y = x * lax.rsqrt(jnp.mean(x * x, axis=-1, keepdims=True) + EPS) * g_ref[...]
    o_ref[...] = y.astype(o_ref.dtype)


def rmsnorm_rows(x, g, tm, out_dtype=BF16):
    m, k = x.shape
    return pl.pallas_call(
        _rmsnorm_kernel,
        grid=(m // tm,),
        in_specs=[pl.BlockSpec((tm, k), lambda i: (i, 0)), pl.BlockSpec((1, k), lambda i: (0, 0))],
        out_specs=pl.BlockSpec((tm, k), lambda i: (i, 0)),
        out_shape=jax.ShapeDtypeStruct((m, k), out_dtype),
        compiler_params=_cparams("parallel"),
        name="rmsnorm_rows",
    )(x, g.reshape(1, k))


def _head_rmsnorm_kernel(x_ref, g_ref, o_ref, *, heads, hd):
    g = g_ref[...]
    for h in range(heads):
        x = x_ref[:, h * hd:(h + 1) * hd]
        y = x * lax.rsqrt(jnp.mean(x * x, axis=-1, keepdims=True) + EPS) * g
        o_ref[:, h * hd:(h + 1) * hd] = y.astype(o_ref.dtype)


def head_rmsnorm(x, col_block, g, heads, hd, tm, out_dtype=F32):
    m = x.shape[0]
    w = heads * hd
    return pl.pallas_call(
        functools.partial(_head_rmsnorm_kernel, heads=heads, hd=hd),
        grid=(m // tm,),
        in_specs=[pl.BlockSpec((tm, w), lambda i: (i, col_block)), pl.BlockSpec((1, hd), lambda i: (0, 0))],
        out_specs=pl.BlockSpec((tm, w), lambda i: (i, 0)),
        out_shape=jax.ShapeDtypeStruct((m, w), out_dtype),
        compiler_params=_cparams("parallel"),
        name="head_rmsnorm",
    )(x, g.reshape(1, hd))


def _matmul_kernel(*refs, ksizes, has_res):
    n_a = len(ksizes)
    w_ref = refs[n_a]
    o_ref = refs[-1]
    acc = None
    off = 0
    for a_ref, ks in zip(refs[:n_a], ksizes):
        part = jnp.dot(a_ref[...], w_ref[off:off + ks, :], preferred_element_type=F32)
        acc = part if acc is None else acc + part
        off += ks
    if has_res:
        acc = acc + refs[n_a + 1][...]
    o_ref[...] = acc


def matmul(a_list, w, tm, tn, residual=None):
    m = a_list[0].shape[0]
    k, n = w.shape
    ksizes = tuple(a.shape[1] for a in a_list)
    assert sum(ksizes) == k and m % tm == 0 and n % tn == 0
    in_specs = [pl.BlockSpec((tm, ks), lambda i, j: (i, 0)) for ks in ksizes]
    in_specs.append(pl.BlockSpec((k, tn), lambda i, j: (0, j)))
    args = list(a_list) + [w]
    if residual is not None:
        in_specs.append(pl.BlockSpec((tm, tn), lambda i, j: (i, j)))
        args.append(residual)
    return pl.pallas_call(
        functools.partial(_matmul_kernel, ksizes=ksizes, has_res=residual is not None),
        grid=(m // tm, n // tn),
        in_specs=in_specs,
        out_specs=pl.BlockSpec((tm, tn), lambda i, j: (i, j)),
        out_shape=jax.ShapeDtypeStruct((m, n), F32),
        compiler_params=_cparams("parallel", "parallel"),
        name="matmul",
    )(*args)


def _split_w_in_kernel(x_ref, dn_ref, fox_ref, mq_ref, small_ref):
    def aligned_down(c):
        return c - c % LANE

    dn_ref[...] = x_ref[:, 0:P_DN_COLS].astype(BF16)
    f0 = aligned_down(_OFF_FQ)
    fox = x_ref[:, f0:f0 + 3 * FOX_W + LANE]
    fox_ref[...] = fox[:, _OFF_FQ - f0:_OFF_FQ - f0 + 3 * FOX_W].astype(BF16)
    m0 = aligned_down(_OFF_MQ)
    mq = x_ref[:, m0:_IN_COLS]
    mq_ref[...] = mq[:, _OFF_MQ - m0:_OFF_MQ - m0 + MEM_W].astype(BF16)
    assert _OFF_B % LANE == 0 and _OFF_FF - aligned_down(_OFF_FF) == 2 * DN_HEADS
    ba = x_ref[:, _OFF_B:_OFF_B + LANE]
    ff = x_ref[:, aligned_down(_OFF_FF):aligned_down(_OFF_FF) + LANE]
    lane = lax.broadcasted_iota(jnp.int32, ba.shape, 1)
    small = jnp.where(lane < 2 * DN_HEADS, ba, jnp.where(lane < 2 * DN_HEADS + FOX_HEADS, ff, 0.0))
    small_ref[...] = small.astype(BF16)


def split_w_in(w_in3, layer, tr):
    k = w_in3.shape[1]
    widths = (P_DN_COLS, 3 * FOX_W, MEM_W, SMALL_COLS)
    return pl.pallas_call(
        _split_w_in_kernel,
        grid=(k // tr,),
        in_specs=[pl.BlockSpec((None, tr, _IN_COLS), lambda i: (layer, i, 0))],
        out_specs=[pl.BlockSpec((tr, w), lambda i: (i, 0)) for w in widths],
        out_shape=[jax.ShapeDtypeStruct((k, w), BF16) for w in widths],
        compiler_params=_cparams("parallel"),
        name="split_w_in",
    )(w_in3)


def _gates_kernel(p_ref, alog_ref, dtb_ref, fb_ref, o_ref):
    x = p_ref[...]
    lane = lax.broadcasted_iota(jnp.int32, x.shape, 1)
    beta = _sigmoid(x)
    g = -jnp.exp(alog_ref[...]) * _softplus(x + dtb_ref[...])
    logf = -_softplus(-(x + fb_ref[...]))
    out = jnp.where(lane < DN_HEADS, beta,
                    jnp.where(lane < 2 * DN_HEADS, g,
                              jnp.where(lane < 2 * DN_HEADS + FOX_HEADS, logf, 0.0)))
    o_ref[...] = out


def gates(p_small, a_log, dt_bias, f_bias, tm):
    m = p_small.shape[0]

    def pad(v, off):
        return jnp.zeros((1, SMALL_COLS), F32).at[0, off:off + v.shape[0]].set(v)

    row = pl.BlockSpec((1, SMALL_COLS), lambda i: (0, 0))
    return pl.pallas_call(
        _gates_kernel,
        grid=(m // tm,),
        in_specs=[pl.BlockSpec((tm, SMALL_COLS), lambda i: (i, 0)), row, row, row],
        out_specs=pl.BlockSpec((tm, SMALL_COLS), lambda i: (i, 0)),
        out_shape=jax.ShapeDtypeStruct((m, SMALL_COLS), F32),
        compiler_params=_cparams("parallel"),
        name="gates",
    )(p_small, pad(a_log, DN_HEADS), pad(dt_bias, DN_HEADS), pad(f_bias, 2 * DN_HEADS))


def _cumsum_kernel(x_ref, o_ref, *, blocks_per_group):
    x = x_ref[...]
    r, n = x.shape
    a = lax.broadcasted_iota(jnp.int32, (n, n), 0)
    b = lax.broadcasted_iota(jnp.int32, (n, n), 1)
    local = _dot_hi(x, (a <= b).astype(F32))
    tot = jnp.broadcast_to(local[:, n - 1:n], (r, n))
    ra = lax.broadcasted_iota(jnp.int32, (r, r), 0)
    rb = lax.broadcasted_iota(jnp.int32, (r, r), 1)
    earlier = jnp.logical_and(rb < ra, (ra // blocks_per_group) == (rb // blocks_per_group)).astype(F32)
    o_ref[...] = local + _dot_hi(earlier, tot)


def cumsum_time(x, blocks_per_group):
    return pl.pallas_call(
        functools.partial(_cumsum_kernel, blocks_per_group=blocks_per_group),
        out_shape=jax.ShapeDtypeStruct(x.shape, F32),
        compiler_params=pltpu.CompilerParams(vmem_limit_bytes=VMEM_LIMIT_BYTES),
        name="cumsum_time",
    )(x)


def _fox_prompt_kernel(q_ref, k_ref, v_ref, c_ref, o_ref, *, blk, hg, scale):
    i = pl.program_id(2)
    q0 = pl.multiple_of(i * blk, blk)
    hd = FOX_HD
    qs = [q_ref[:, h * hd:(h + 1) * hd] for h in range(hg)]
    cqs = [_row_to_col(c_ref[h, :, pl.ds(q0, blk)], blk) for h in range(hg)]

    def scores(h, k0):
        k = k_ref[pl.ds(k0, blk), h * hd:(h + 1) * hd]
        return _dot_nt(qs[h], k) * scale + (cqs[h] - c_ref[h, :, pl.ds(k0, blk)])

    def update(h, state, s, k0):
        m, l, acc = state
        m_new = jnp.maximum(m, jnp.max(s, axis=1, keepdims=True))
        alpha = jnp.exp(m - m_new)
        p = jnp.exp(s - m_new)
        l = l * alpha + jnp.sum(p, axis=1, keepdims=True)
        v = v_ref[pl.ds(k0, blk), h * hd:(h + 1) * hd]
        acc = acc * alpha + jnp.dot(p.astype(BF16), v, preferred_element_type=F32)
        return m_new, l, acc

    def body(j, carry):
        k0 = pl.multiple_of(j * blk, blk)
        ss = [scores(h, k0) for h in range(hg)]
        return tuple(update(h, carry[h], ss[h], k0) for h in range(hg))

    init = tuple((jnp.full((blk, 1), NEG_INF, F32), jnp.zeros((blk, 1), F32), jnp.zeros((blk, hd), F32))
                 for _ in range(hg))
    carry = lax.fori_loop(0, i, body, init)
    causal = (lax.broadcasted_iota(jnp.int32, (blk, blk), 1) <= lax.broadcasted_iota(jnp.int32, (blk, blk), 0))
    for h in range(hg):
        s = jnp.where(causal, scores(h, q0), NEG_INF)
        _, l, acc = update(h, carry[h], s, q0)
        o_ref[:, h * hd:(h + 1) * hd] = (acc / l).astype(o_ref.dtype)


def fox_prompt(qn, kn, v, c_row, batch, seq, blk, hg):
    nq = seq // blk
    ng = FOX_HEADS // hg
    w = hg * FOX_HD
    kv_spec = pl.BlockSpec((seq, w), lambda b, h, i: (b, h))
    return pl.pallas_call(
        functools.partial(_fox_prompt_kernel, blk=blk, hg=hg, scale=FOX_HD ** -0.5),
        grid=(batch, ng, nq),
        in_specs=[pl.BlockSpec((blk, w), lambda b, h, i: (b * nq + i, h)), kv_spec, kv_spec,
                  pl.BlockSpec((hg, 1, seq), lambda b, h, i: (b * ng + h, 0, 0))],
        out_specs=pl.BlockSpec((blk, w), lambda b, h, i: (b * nq + i, h)),
        out_shape=jax.ShapeDtypeStruct((batch * seq, FOX_W), BF16),
        compiler_params=_cparams("parallel", "parallel", "arbitrary"),
        name="fox_prompt",
    )(qn, kn, v, c_row)


def _dn_prompt_kernel(q_ref, k_ref, v_ref, z_ref, wq_ref, wk_ref, wv_ref, g_ref, b_ref, ng_ref,
                      o_ref, s_out_ref, s_ref, xq_ref, xk_ref, xv_ref, *, tc, chunk, hg):
    t = pl.program_id(2)
    nc = tc // chunk
    pad = 8

    @pl.when(t == 0)
    def _():
        s_ref[...] = jnp.zeros_like(s_ref)
        for buf in (xq_ref, xk_ref, xv_ref):
            buf[0:pad, :] = jnp.zeros((pad, hg * DN_D), F32)

    def conv_silu(x_ref, w_ref, buf_ref):
        u = x_ref[...]
        buf_ref[pad:pad + tc, :] = u
        w = w_ref[...]
        out = buf_ref[pad - 3:pad - 3 + tc, :] * w[0:1, :]
        out = out + buf_ref[pad - 2:pad - 2 + tc, :] * w[1:2, :]
        out = out + buf_ref[pad - 1:pad - 1 + tc, :] * w[2:3, :]
        out = out + u * w[3:4, :]
        buf_ref[0:pad, :] = u[tc - pad:tc, :]
        return out * _sigmoid(out)

    def l2norm(x):
        return x * lax.rsqrt(jnp.sum(x * x, axis=-1, keepdims=True) + EPS)

    qc = conv_silu(q_ref, wq_ref, xq_ref)
    kc = conv_silu(k_ref, wk_ref, xk_ref)
    vc = conv_silu(v_ref, wv_ref, xv_ref)

    ra = lax.broadcasted_iota(jnp.int32, (chunk, chunk), 0)
    rb = lax.broadcasted_iota(jnp.int32, (chunk, chunk), 1)
    tri = ra >= rb
    strict = ra > rb
    eye_f = (ra == rb).astype(F32)
    upper_f = (ra <= rb).astype(F32)

    heads = []
    for hh in range(hg):
        cs = slice(hh * DN_D, (hh + 1) * DN_D)
        heads.append(dict(
            q=l2norm(qc[:, cs]) * (DN_D ** -0.5), k=l2norm(kc[:, cs]), v=vc[:, cs],
            gc=_dot_hi(g_ref[hh], upper_f),
            beta=b_ref[hh], s=s_ref[hh], outs=[]))

    work = []
    for c in range(nc):
        sl = slice(c * chunk, (c + 1) * chunk)
        for hd in heads:
            q, k, v = hd['q'][sl], hd['k'][sl], hd['v'][sl]
            gc_row = hd['gc'][c:c + 1, :]
            gc = _row_to_col(gc_row, chunk)
            beta = _row_to_col(hd['beta'][c:c + 1, :], chunk)
            decay = jnp.where(tri, jnp.exp(jnp.where(tri, gc - gc_row, 0.0)), 0.0)
            qk_b = jnp.concatenate([q, k], axis=0).astype(BF16)
            work.append(dict(hd=hd, q=q, k=k, v=v, gc=gc, beta=beta, decay=decay, qk_b=qk_b,
                             gc_last=gc_row[:, chunk - 1:chunk]))
    for wk in work:
        wk['gram'] = _dot_nt(wk['qk_b'], wk['qk_b'][chunk:, :])
    for wk in work:
        lower = jnp.where(strict, wk['beta'] * wk['gram'][chunk:, :] * wk['decay'], 0.0)
        wk['inv'] = eye_f - lower
        wk['power'] = _split_bf16(lower)
    for wk in work:
        wk['power'] = _split_bf16(_dot_split(wk['power'], wk['power']))
    span = 2
    while span < chunk:
        for wk in work:
            wk['inv'] = wk['inv'] + _dot_split(_split_bf16(wk['inv']), wk['power'])
        span *= 2
        if span < chunk:
            for wk in work:
                wk['power'] = _split_bf16(_dot_split(wk['power'], wk['power']))
    for wk in work:
        egc = jnp.exp(wk['gc'])
        k, beta = wk['k'], wk['beta']
        uw = _dot_split(_split_bf16(wk['inv']),
                        _split_bf16(jnp.concatenate([wk['v'] * beta, k * (beta * egc)], axis=1)))
        wk['u'] = uw[:, :DN_D]
        wk['lhs1'] = jnp.concatenate([uw[:, DN_D:], wk['q'] * egc], axis=0).astype(BF16)
        qk = jnp.where(tri, wk['gram'][:chunk, :] * wk['decay'], 0.0)
        k_dec = k * jnp.exp(wk['gc_last'] - wk['gc'])
        wk['lhs2'] = jnp.concatenate([qk, k_dec.T], axis=0).astype(BF16)
        wk['g_last'] = jnp.exp(wk['gc_last'])
    for wk in work:
        hd = wk['hd']
        s = hd['s']
        ws = jnp.dot(wk['lhs1'], s.astype(BF16), preferred_element_type=F32)
        v_new = wk['u'] - ws[:chunk, :]
        upd = jnp.dot(wk['lhs2'], v_new.astype(BF16), preferred_element_type=F32)
        hd['outs'].append(ws[chunk:, :] + upd[:chunk, :])
        hd['s'] = s * wk['g_last'] + upd[chunk:, :]

    ng = ng_ref[...]
    for hh, hd in enumerate(heads):
        cs = slice(hh * DN_D, (hh + 1) * DN_D)
        s_ref[hh] = hd['s']
        s_out_ref[hh] = hd['s']
        o = jnp.concatenate(hd['outs'], axis=0)
        o = o * lax.rsqrt(jnp.mean(o * o, axis=-1, keepdims=True) + EPS) * ng
        z = z_ref[:, cs]
        o_ref[:, cs] = (o * (z * _sigmoid(z))).astype(o_ref.dtype)


def dn_prompt(p, conv_w, g_chunks, beta_chunks, norm_g, batch, seq, tc, hg):
    nt = seq // tc
    nc = tc // DN_CHUNK
    ng = DN_HEADS // hg
    w = hg * DN_D

    def col(off):
        return pl.BlockSpec((tc, w), lambda b, h, t: (b * nt + t, off * ng + h))

    def wcol(off):
        return pl.BlockSpec((DN_CONV, w), lambda b, h, t: (0, off * ng + h))

    gspec = pl.BlockSpec((hg, nc, DN_CHUNK), lambda b, h, t: (b * ng + h, t, 0))
    return pl.pallas_call(
        functools.partial(_dn_prompt_kernel, tc=tc, chunk=DN_CHUNK, hg=hg),
        grid=(batch, ng, nt),
        in_specs=[col(0), col(1), col(2), col(3), wcol(0), wcol(1), wcol(2), gspec, gspec,
                  pl.BlockSpec((1, DN_D), lambda b, h, t: (0, 0))],
        out_specs=[pl.BlockSpec((tc, w), lambda b, h, t: (b * nt + t, h)),
                   pl.BlockSpec((hg, DN_D, DN_D), lambda b, h, t: (b * ng + h, 0, 0))],
        out_shape=[jax.ShapeDtypeStruct((batch * seq, DN_QK), BF16),
                   jax.ShapeDtypeStruct((batch * DN_HEADS, DN_D, DN_D), F32)],
        scratch_shapes=[pltpu.VMEM((hg, DN_D, DN_D), F32)] + [pltpu.VMEM((tc + 8, w), F32)] * 3,
        compiler_params=_cparams("parallel", "parallel", "arbitrary"),
        name="dn_prompt",
    )(p, p, p, p, conv_w, conv_w, conv_w, g_chunks, beta_chunks, norm_g.reshape(1, DN_D))


def _dn_sample_kernel(q_ref, k_ref, v_ref, z_ref, cq_ref, ck_ref, cv_ref, wq_ref, wk_ref, wv_ref,
                      g_ref, b_ref, ng_ref, s_ref, o_ref, s_out_ref, *, bb):
    def conv_silu(x_ref, c_ref, w_ref):
        w = w_ref[...]
        out = c_ref[:, 0, :] * w[0:1, :]
        out = out + c_ref[:, 1, :] * w[1:2, :]
        out = out + c_ref[:, 2, :] * w[2:3, :]
        out = out + x_ref[...] * w[3:4, :]
        return out * _sigmoid(out)

    def l2norm(x):
        return x * lax.rsqrt(jnp.sum(x * x, axis=-1, keepdims=True) + EPS)

    q = l2norm(conv_silu(q_ref, cq_ref, wq_ref)) * (DN_D ** -0.5)
    k = l2norm(conv_silu(k_ref, ck_ref, wk_ref))
    v = conv_silu(v_ref, cv_ref, wv_ref)
    q_t = q.T
    k_t = k.T
    decay = jnp.exp(g_ref[0])
    beta = b_ref[0]
    rows = []
    for b in range(bb):
        s = s_ref[b, 0] * decay[b:b + 1, :]
        k_col = k_t[:, b:b + 1]
        kv = jnp.sum(k_col * s, axis=0, keepdims=True)
        s = s + k_col * ((v[b:b + 1, :] - kv) * beta[b:b + 1, :])
        s_out_ref[b, 0] = s
        rows.append(jnp.sum(q_t[:, b:b + 1] * s, axis=0, keepdims=True))
    o = jnp.concatenate(rows, axis=0)
    o = o * lax.rsqrt(jnp.mean(o * o, axis=-1, keepdims=True) + EPS) * ng_ref[...]
    z = z_ref[...]
    o_ref[...] = (o * (z * _sigmoid(z))).astype(o_ref.dtype)


def dn_sample(p, state_conv, conv_w, g_t, beta_t, norm_g, state, bb):
    nb = p.shape[0]
    hb = DN_HEADS

    def col(off):
        return pl.BlockSpec((bb, DN_D), lambda i, h: (i, off + h))

    def ccol(off):
        return pl.BlockSpec((bb, DN_CONV - 1, DN_D), lambda i, h: (i, 0, off + h))

    def wcol(off):
        return pl.BlockSpec((DN_CONV, DN_D), lambda i, h: (0, off + h))

    gspec = pl.BlockSpec((1, bb, 1), lambda i, h: (h, i, 0))
    sspec = pl.BlockSpec((bb, 1, DN_D, DN_D), lambda i, h: (i, h, 0, 0))
    return pl.pallas_call(
        functools.partial(_dn_sample_kernel, bb=bb),
        grid=(nb // bb, DN_HEADS),
        in_specs=[col(0), col(hb), col(2 * hb), col(3 * hb), ccol(0), ccol(hb), ccol(2 * hb),
                  wcol(0), wcol(hb), wcol(2 * hb), gspec, gspec,
                  pl.BlockSpec((1, DN_D), lambda i, h: (0, 0)), sspec],
        out_specs=[pl.BlockSpec((bb, DN_D), lambda i, h: (i, h)), sspec],
        out_shape=[jax.ShapeDtypeStruct((nb, DN_QK), BF16), jax.ShapeDtypeStruct(state.shape, F32)],
        compiler_params=_cparams("parallel", "parallel"),
        name="dn_sample",
    )(p, p, p, p, state_conv, state_conv, state_conv, conv_w, conv_w, conv_w, g_t, beta_t,
      norm_g.reshape(1, DN_D), state)


def _fox_sample_kernel(pt_ref, q_ref, kn_ref, vn_ref, lfn_ref, *refs, pp, scale):
    k_refs = refs[0:pp]
    v_refs = refs[pp:2 * pp]
    lf_refs = refs[2 * pp:3 * pp]
    o_ref = refs[3 * pp]
    m_ref, l_ref, acc_ref, carry_ref = refs[3 * pp + 1:]
    j = pl.program_id(1)
    qs = q_ref[0] * scale

    @pl.when(j == 0)
    def _():
        m_ref[...] = jnp.sum(qs * kn_ref[0], axis=1, keepdims=True)
        l_ref[...] = jnp.ones_like(l_ref)
        acc_ref[...] = vn_ref[0]
        carry_ref[...] = lfn_ref[0]

    ra = lax.broadcasted_iota(jnp.int32, (PAGE_SIZE, PAGE_SIZE), 0)
    rb = lax.broadcasted_iota(jnp.int32, (PAGE_SIZE, PAGE_SIZE), 1)
    later_f = (ra > rb).astype(F32)
    shape3 = (PAGE_SIZE, FOX_HEADS, FOX_HD)
    diag3 = lax.broadcasted_iota(jnp.int32, shape3, 0) == lax.broadcasted_iota(jnp.int32, shape3, 2)
    carry = carry_ref[...]
    logits = []
    for i in range(pp):
        lf = lf_refs[i][0]
        bias = _dot_hi(lf, later_f) + carry
        carry = carry + jnp.sum(lf, axis=1, keepdims=True)
        logits.append(jnp.sum(k_refs[i][0] * qs[None] + jnp.where(diag3, bias[None], 0.0),
                              axis=2, keepdims=True))
    carry_ref[...] = carry
    m_old = m_ref[...]
    m_new = m_old
    for s3 in logits:
        m_new = jnp.maximum(m_new, jnp.max(s3, axis=0))
    alpha = jnp.exp(m_old - m_new)
    l_new = l_ref[...] * alpha
    acc = acc_ref[...] * alpha
    for i, s3 in enumerate(logits):
        p3 = jnp.exp(s3 - m_new[None])
        l_new = l_new + jnp.sum(p3, axis=0)
        acc = acc + jnp.sum(p3 * v_refs[i][0], axis=0)
    l_ref[...] = l_new
    acc_ref[...] = acc
    m_ref[...] = m_new

    @pl.when(j == pl.num_programs(1) - 1)
    def _():
        o_ref[0] = (acc_ref[...] / l_ref[...]).astype(o_ref.dtype)


def fox_sample(q, k_new, v_new, lf_new, k_pool, v_pool, lf_pool_t, page_table, pp):
    nb, n_pages = page_table.shape
    assert n_pages % pp == 0
    steps = n_pages // pp

    def page_idx4(i):
        return lambda b, j, pt: (pt[b, n_pages - 1 - (j * pp + i)], 0, 0, 0)

    def page_idx3(i):
        return lambda b, j, pt: (pt[b, n_pages - 1 - (j * pp + i)], 0, 0)

    tok = lambda b, j, pt: (b, 0, 0)
    in_specs = [pl.BlockSpec((1, FOX_HEADS, FOX_HD), tok)] * 3 + [pl.BlockSpec((1, FOX_HEADS, 1), tok)]
    in_specs += [pl.BlockSpec((1, PAGE_SIZE, FOX_HEADS, FOX_HD), page_idx4(i)) for i in range(pp)]
    in_specs += [pl.BlockSpec((1, PAGE_SIZE, FOX_HEADS, FOX_HD), page_idx4(i)) for i in range(pp)]
    in_specs += [pl.BlockSpec((1, FOX_HEADS, PAGE_SIZE), page_idx3(i)) for i in range(pp)]
    grid_spec = pltpu.PrefetchScalarGridSpec(
        num_scalar_prefetch=1,
        grid=(nb, steps),
        in_specs=in_specs,
        out_specs=pl.BlockSpec((1, FOX_HEADS, FOX_HD), tok),
        scratch_shapes=[pltpu.VMEM((FOX_HEADS, 1), F32), pltpu.VMEM((FOX_HEADS, 1), F32),
                        pltpu.VMEM((FOX_HEADS, FOX_HD), F32), pltpu.VMEM((FOX_HEADS, 1), F32)],
    )
    return pl.pallas_call(
        functools.partial(_fox_sample_kernel, pp=pp, scale=FOX_HD ** -0.5),
        grid_spec=grid_spec,
        out_shape=jax.ShapeDtypeStruct((nb, FOX_HEADS, FOX_HD), BF16),
        compiler_params=_cparams("parallel", "arbitrary"),
        name="fox_sample",
    )(page_table, q, k_new, v_new, lf_new, *([k_pool] * pp), *([v_pool] * pp), *([lf_pool_t] * pp))


def _mem_prompt_kernel(q_ref, k_ref, v_ref, g_ref, o_ref, *, scale):
    g = g_ref[...]
    for h in range(MEM_HEADS):
        cs = slice(h * MEM_HD, (h + 1) * MEM_HD)
        q = q_ref[:, cs]
        qn = (q * lax.rsqrt(jnp.mean(q * q, axis=-1, keepdims=True) + EPS) * g).astype(BF16)
        s = _dot_nt(qn, k_ref[:, cs]) * scale
        p = jnp.exp(s - jnp.max(s, axis=1, keepdims=True))
        p = p / jnp.sum(p, axis=1, keepdims=True)
        o_ref[:, cs] = jnp.dot(p.astype(BF16), v_ref[:, cs], preferred_element_type=F32).astype(o_ref.dtype)


def mem_attend_prompt(p, mk, mv, qn_g, batch, seq, tq):
    nq = seq // tq
    kv = pl.BlockSpec((MEM_TOKENS, MEM_W), lambda b, i: (b, 0))
    return pl.pallas_call(
        functools.partial(_mem_prompt_kernel, scale=MEM_HD ** -0.5),
        grid=(batch, nq),
        in_specs=[pl.BlockSpec((tq, MEM_W), lambda b, i: (b * nq + i, 0)), kv, kv,
                  pl.BlockSpec((1, MEM_HD), lambda b, i: (0, 0))],
        out_specs=pl.BlockSpec((tq, MEM_W), lambda b, i: (b * nq + i, 0)),
        out_shape=jax.ShapeDtypeStruct((batch * seq, MEM_W), BF16),
        compiler_params=_cparams("parallel", "parallel"),
        name="mem_prompt",
    )(p, mk, mv, qn_g.reshape(1, MEM_HD))


def _mem_sample_kernel(q_ref, k_ref, v_ref, g_ref, o_ref, *, scale):
    q = q_ref[0]
    qn = q * lax.rsqrt(jnp.mean(q * q, axis=-1, keepdims=True) + EPS) * g_ref[...] * scale
    s3 = jnp.sum(k_ref[0] * qn[None], axis=2, keepdims=True)
    p3 = jnp.exp(s3 - jnp.max(s3, axis=0)[None])
    o = jnp.sum(p3 * v_ref[0], axis=0) / jnp.sum(p3, axis=0)
    o_ref[0] = o.astype(o_ref.dtype)


def mem_sample(q, mk, mv, qn_g):
    nb = q.shape[0]
    tok = pl.BlockSpec((1, MEM_HEADS, MEM_HD), lambda b: (b, 0, 0))
    kv = pl.BlockSpec((1, MEM_TOKENS, MEM_HEADS, MEM_HD), lambda b: (b, 0, 0, 0))
    return pl.pallas_call(
        functools.partial(_mem_sample_kernel, scale=MEM_HD ** -0.5),
        grid=(nb,),
        in_specs=[tok, kv, kv, pl.BlockSpec((1, MEM_HD), lambda b: (0, 0))],
        out_specs=tok,
        out_shape=jax.ShapeDtypeStruct((nb, MEM_HEADS, MEM_HD), BF16),
        compiler_params=_cparams("parallel"),
        name="mem_sample",
    )(q, mk, mv, qn_g.reshape(1, MEM_HD))


PEER_NO_RANK = float(PEER_NKEYS)


def _top_values(s, count, with_rank=False):
    rows = s.shape[0]
    idx = lax.broadcasted_iota(jnp.int32, s.shape, 0).astype(F32)
    rank = jnp.full(s.shape, PEER_NO_RANK, F32) if with_rank else None
    vals = []
    for r in range(count):
        m = jnp.max(s, axis=0, keepdims=True)
        picked = idx == jnp.min(jnp.where(s == m, idx, float(rows)), axis=0, keepdims=True)
        s = jnp.where(picked, NEG_INF, s)
        if with_rank:
            rank = jnp.where(picked, float(r), rank)
        vals.append(m)
    return jnp.concatenate(vals, axis=0), rank


def _peer_stats_kernel(hn_ref, wq_ref, keys_ref, cnt_ref, g1_ref, r2_ref, e2_ref, st_ref, top_ref, rank_ref,
                       cnt_scr, g1_scr):
    nk = PEER_NKEYS
    kk = PEER_TOPK
    q = jnp.dot(hn_ref[...], wq_ref[...], preferred_element_type=F32)
    st_ref[...] = _dot_nt(keys_ref[...], q.astype(BF16))

    def half_body(c, carry):
        r0 = pl.multiple_of(c * nk, nk)
        t0 = pl.multiple_of(c * kk, kk)
        vals, rank = _top_values(st_ref[pl.ds(r0, nk), :], kk, with_rank=True)
        top_ref[pl.ds(t0, kk), :] = vals
        rank_ref[pl.ds(r0, nk), :] = rank
        return carry

    lax.fori_loop(0, 2 * PEER_HEADS, half_body, 0)

    def head_body(h, carry):
        t0 = pl.multiple_of(h * 2 * kk, 2 * kk)
        v1 = top_ref[pl.ds(t0, kk), :]
        v2 = top_ref[pl.ds(t0 + kk, kk), :]
        cand = jnp.concatenate([v1[0:1, :] + v2] + [v1[a:a + 1, :] + v2[0:8, :] for a in range(1, 8)]
                               + [v1[8:kk, :] + v2[0:1, :]], axis=0)
        cv, _ = _top_values(cand, kk)
        z = jnp.sum(jnp.exp(cv - cv[0:1, :]), axis=0, keepdims=True)
        tau = cv[kk - 1:kk, :]
        r0 = pl.multiple_of(h * 2 * nk, 2 * nk)
        s1 = st_ref[pl.ds(r0, nk), :]
        s2 = st_ref[pl.ds(r0 + nk, nk), :]
        rank1 = rank_ref[pl.ds(r0, nk), :]
        cnt = jnp.zeros_like(s1)
        for a in range(kk):
            n_hit = jnp.sum(((v1[a:a + 1, :] + v2) >= tau).astype(F32), axis=0, keepdims=True)
            cnt = jnp.where(rank1 == float(a), n_hit, cnt)
        cnt_scr[h] = cnt
        g1_scr[h] = jnp.exp(s1 - v1[0:1, :]) / z
        r2_ref[h] = rank_ref[pl.ds(r0 + nk, nk), :].astype(BF16)
        e2_ref[h] = jnp.exp(s2 - v2[0:1, :]).astype(BF16)
        return carry

    lax.fori_loop(0, PEER_HEADS, head_body, 0)
    for h in range(PEER_HEADS):
        cnt_ref[:, h, :] = cnt_scr[h]
        g1_ref[:, h, :] = g1_scr[h]


def peer_stats(hn, w_q, keys_t, mt):
    m = hn.shape[0]
    nrow = PEER_HEADS * 2 * PEER_NKEYS
    hspec = pl.BlockSpec((PEER_HEADS, PEER_NKEYS, mt), lambda i: (0, 0, i))
    kspec = pl.BlockSpec((PEER_NKEYS, PEER_HEADS, mt), lambda i: (0, 0, i))
    f32_shape = jax.ShapeDtypeStruct((PEER_NKEYS, PEER_HEADS, m), F32)
    bf16_shape = jax.ShapeDtypeStruct((PEER_HEADS, PEER_NKEYS, m), BF16)
    return pl.pallas_call(
        _peer_stats_kernel,
        grid=(m // mt,),
        in_specs=[pl.BlockSpec((mt, D_MODEL), lambda i: (i, 0)),
                  pl.BlockSpec(w_q.shape, lambda i: (0, 0)),
                  pl.BlockSpec(keys_t.shape, lambda i: (0, 0))],
        out_specs=[kspec, kspec, hspec, hspec],
        out_shape=[f32_shape, f32_shape, bf16_shape, bf16_shape],
        scratch_shapes=[pltpu.VMEM((nrow, mt), F32), pltpu.VMEM((2 * PEER_HEADS * PEER_TOPK, mt), F32),
                        pltpu.VMEM((nrow, mt), F32), pltpu.VMEM((PEER_HEADS, PEER_NKEYS, mt), F32),
                        pltpu.VMEM((PEER_HEADS, PEER_NKEYS, mt), F32)],
        compiler_params=_cparams("parallel"),
        name="peer_stats",
    )(hn, w_q, keys_t)


def _peer_gates(cnt_ref, g1_ref, r2_ref, e2_ref, gate_ref, n_a):
    mt = gate_ref.shape[1]
    rows = 16
    heads = range(PEER_HEADS)
    for a in range(n_a):
        for c in range(mt // LANE):
            cs = slice(c * LANE, (c + 1) * LANE)
            cnt = [jnp.broadcast_to(cnt_ref[a, h:h + 1, cs], (rows, LANE)).astype(BF16) for h in heads]
            g1 = [jnp.broadcast_to(g1_ref[a, h:h + 1, cs], (rows, LANE)).astype(BF16) for h in heads]
            for r in range(PEER_NKEYS // rows):
                rs = slice(r * rows, (r + 1) * rows)
                w = None
                for h in heads:
                    term = jnp.where(r2_ref[h, rs, cs] < cnt[h], e2_ref[h, rs, cs], 0.0) * g1[h]
                    w = term if w is None else w + term
                gate_ref[a * PEER_NKEYS + r * rows:a * PEER_NKEYS + (r + 1) * rows, cs] = w


def _peer_mix_kernel(hn_ref, res_ref, u_ref, v_ref, cnt_ref, g1_ref, cntn_ref, g1n_ref, r2_ref, e2_ref,
                     o_ref, gate_a_ref, gate_b_ref, *, et):
    e = pl.program_id(1)
    n_a = et // PEER_NKEYS

    @pl.when(e == 0)
    def _():
        o_ref[...] = res_ref[...]
        _peer_gates(cnt_ref, g1_ref, r2_ref, e2_ref, gate_a_ref, n_a)

    def step(gate_cur_ref, gate_next_ref):
        _peer_gates(cntn_ref, g1n_ref, r2_ref, e2_ref, gate_next_ref, n_a)
        act_t = _dot_nt(u_ref[...], hn_ref[...])
        gel_t = 0.5 * act_t * (1.0 + lax.erf(act_t * (0.5 ** 0.5)))
        hmat = (gate_cur_ref[...].astype(F32) * gel_t).astype(BF16).T
        o_ref[...] += jnp.dot(hmat, v_ref[...], preferred_element_type=F32)

    @pl.when(e % 2 == 0)
    def _():
        step(gate_a_ref, gate_b_ref)

    @pl.when(e % 2 == 1)
    def _():
        step(gate_b_ref, gate_a_ref)


def peer_mix(hn, res, u_tab, v_tab, cnt_t, g1_t, r2, e2, mt, et):
    m = hn.shape[0]
    n_exp = u_tab.shape[0]
    n_e = n_exp // et
    a_per = et // PEER_NKEYS
    once = pl.Buffered(1)
    sel1 = pl.BlockSpec((a_per, PEER_HEADS, mt), lambda i, e: (e, 0, i))
    sel1_next = pl.BlockSpec((a_per, PEER_HEADS, mt), lambda i, e: (jnp.minimum(e + 1, n_e - 1), 0, i))
    sel2 = pl.BlockSpec((PEER_HEADS, PEER_NKEYS, mt), lambda i, e: (0, 0, i), pipeline_mode=once)
    tab = pl.BlockSpec((et, D_MODEL), lambda i, e: (e, 0))
    return pl.pallas_call(
        functools.partial(_peer_mix_kernel, et=et),
        grid=(m // mt, n_e),
        in_specs=[pl.BlockSpec((mt, D_MODEL), lambda i, e: (i, 0), pipeline_mode=once),
                  pl.BlockSpec((mt, D_MODEL), lambda i, e: (i, 0), pipeline_mode=once), tab, tab,
                  sel1, sel1, sel1_next, sel1_next, sel2, sel2],
        out_specs=pl.BlockSpec((mt, D_MODEL), lambda i, e: (i, 0)),
        out_shape=jax.ShapeDtypeStruct((m, D_MODEL), F32),
        scratch_shapes=[pltpu.VMEM((et, mt), BF16), pltpu.VMEM((et, mt), BF16)],
        compiler_params=_cparams("parallel", "arbitrary"),
        name="peer_mix",
    )(hn, res, u_tab, v_tab, cnt_t, g1_t, cnt_t, g1_t, r2, e2)


def _prep_weights(w_in3, layer, w_mem_kv, w_out, peer_w_q, peer_sub_keys, peer_u, peer_v):
    w_dn, w_fox, w_mq, w_small = split_w_in(w_in3, layer, 128)
    half = PEER_DKEY // 2
    pairs = 2 * PEER_HEADS
    sk = peer_sub_keys.reshape(pairs, PEER_NKEYS, half)
    eye = jnp.eye(pairs, dtype=F32)
    keys_t = (eye[:, None, :, None] * sk[:, :, None, :]).reshape(pairs * PEER_NKEYS, pairs * half).astype(BF16)
    return dict(w_dn=w_dn, w_fox=w_fox, w_mq=w_mq, w_small=w_small, w_mem_kv=w_mem_kv.astype(BF16), w_out=w_out.astype(BF16),
                peer_w_q=peer_w_q.astype(BF16), keys_t=keys_t, peer_u=peer_u.astype(BF16),
                peer_v=peer_v.astype(BF16))


def _tile(m, pref):
    return pref if m % pref == 0 else m


def _project(x2, ln_g, wts):
    m = x2.shape[0]
    xn = rmsnorm_rows(x2, ln_g, _tile(m, 256))
    tm = _tile(m, 1024)
    p_dn = matmul([xn], wts['w_dn'], tm, 512)
    p_fox = matmul([xn], wts['w_fox'], tm, 512)
    p_mq = matmul([xn], wts['w_mq'], tm, 512)
    ps = matmul([xn], wts['w_small'], tm, SMALL_COLS)
    return p_dn, p_fox, p_mq, ps


def _channel_mix(x2, o_dn, o_fox, o_mem, wts, ln_ffn_g):
    m = x2.shape[0]
    h = matmul([o_dn, o_fox, o_mem], wts['w_out'], _tile(m, 1024), 512, residual=x2)
    hn = rmsnorm_rows(h, ln_ffn_g, _tile(m, 256))
    cnt_t, g1_t, r2, e2 = peer_stats(hn, wts['peer_w_q'], wts['keys_t'], _tile(m, 512))
    return peer_mix(hn, h, wts['peer_u'], wts['peer_v'], cnt_t, g1_t, r2, e2, _tile(m, 512), 512)


def kernel(x_prompt, x_sample, cache_fox_k, cache_fox_v, cache_fox_logf, state_delta, state_conv, cache_mem_k, cache_mem_v, page_table, mem_prompt, ln_mix_g, w_in, conv_w, dn_a_log, dn_dt_bias, dn_norm_g, fox_f_bias, fox_qn_g, fox_kn_g, ln_mem_g, w_mem_kv, mem_qn_g, mem_kn_g, w_out, ln_ffn_g, peer_w_q, peer_sub_keys, peer_u, peer_v):
    depth = w_in.shape[0]
    assert depth == 1
    l = 0
    batch, seq, _ = x_prompt.shape
    nb = x_sample.shape[0]
    m_p = batch * seq
    wts = _prep_weights(w_in, l, w_mem_kv[l], w_out[l], peer_w_q[l], peer_sub_keys[l], peer_u[l], peer_v[l])

    x2 = x_prompt.reshape(m_p, D_MODEL)
    p, p_fox, p_mq, ps = _project(x2, ln_mix_g[l], wts)
    gt = gates(ps, dn_a_log[l], dn_dt_bias[l], fox_f_bias[l], 1024)
    beta = gt[:, :DN_HEADS]
    gdec = gt[:, DN_HEADS:2 * DN_HEADS]
    logf = gt[:, 2 * DN_HEADS:2 * DN_HEADS + FOX_HEADS]

    def to_chunks(a):
        return a.reshape(batch, seq, DN_HEADS).transpose(0, 2, 1).reshape(batch * DN_HEADS, seq // DN_CHUNK, DN_CHUNK)

    o_dn, dn_state = dn_prompt(p, conv_w[l], to_chunks(gdec), to_chunks(beta), dn_norm_g[l], batch, seq, 512, 8)

    fqn = head_rmsnorm(p_fox, 0, fox_qn_g[l], FOX_HEADS, FOX_HD, 512, out_dtype=BF16)
    fkn = head_rmsnorm(p_fox, 1, fox_kn_g[l], FOX_HEADS, FOX_HD, 512)
    fv = p_fox[:, 2 * FOX_W:]
    lf_rows = logf.reshape(batch, seq, FOX_HEADS).transpose(0, 2, 1).reshape(batch * FOX_HEADS * (seq // LANE), LANE)
    c_row = cumsum_time(lf_rows, seq // LANE).reshape(batch * FOX_HEADS, 1, seq)
    o_fox = fox_prompt(fqn, fkn.astype(BF16), fv.astype(BF16), c_row, batch, seq, 512, 2)

    mem2 = mem_prompt.reshape(batch * MEM_TOKENS, D_MODEL)
    memn = rmsnorm_rows(mem2, ln_mem_g[l], 256)
    mkv = matmul([memn], wts['w_mem_kv'], batch * MEM_TOKENS, 512)
    mk = head_rmsnorm(mkv, 0, mem_kn_g[l], MEM_HEADS, MEM_HD, 256)
    mv = mkv[:, MEM_W:]
    o_mem = mem_attend_prompt(p_mq, mk.astype(BF16), mv.astype(BF16), mem_qn_g[l], batch, seq, 512)

    y_p = _channel_mix(x2, o_dn, o_fox, o_mem, wts, ln_ffn_g[l])

    xs = x_sample.reshape(nb, D_MODEL)
    sp, sp_fox, sp_mq, sps = _project(xs, ln_mix_g[l], wts)
    sgt = gates(sps, dn_a_log[l], dn_dt_bias[l], fox_f_bias[l], nb)
    s_beta = sgt[:, :DN_HEADS].T.reshape(DN_HEADS, nb, 1)
    s_g = sgt[:, DN_HEADS:2 * DN_HEADS].T.reshape(DN_HEADS, nb, 1)
    s_logf = sgt[:, 2 * DN_HEADS:2 * DN_HEADS + FOX_HEADS]
    so_dn, s_state = dn_sample(sp, state_conv[l], conv_w[l], s_g, s_beta, dn_norm_g[l], state_delta[l], 32)
    conv_s = jnp.concatenate([state_conv[l][:, 1:, :], sp[:, None, :CONV_CH]], axis=1)

    sfq = head_rmsnorm(sp_fox, 0, fox_qn_g[l], FOX_HEADS, FOX_HD, nb)
    sfk = head_rmsnorm(sp_fox, 1, fox_kn_g[l], FOX_HEADS, FOX_HD, nb)
    sfv = sp_fox[:, 2 * FOX_W:]
    so_fox = fox_sample(sfq.reshape(nb, FOX_HEADS, FOX_HD), sfk.reshape(nb, FOX_HEADS, FOX_HD),
                        sfv.reshape(nb, FOX_HEADS, FOX_HD), s_logf.reshape(nb, FOX_HEADS, 1),
                        cache_fox_k[l], cache_fox_v[l], jnp.swapaxes(cache_fox_logf[l], 1, 2),
                        page_table, 8).reshape(nb, FOX_W)
    smq = sp_mq.reshape(nb, MEM_HEADS, MEM_HD)
    so_mem = mem_sample(smq, cache_mem_k[l], cache_mem_v[l], mem_qn_g[l]).reshape(nb, MEM_W)
    y_s = _channel_mix(xs, so_dn, so_fox, so_mem, wts, ln_ffn_g[l])

    return (
        y_p.reshape(batch, seq, D_MODEL),
        y_s.reshape(nb, 1, D_MODEL),
        fkn.reshape(1, batch, seq, FOX_HEADS, FOX_HD),
        fv.reshape(1, batch, seq, FOX_HEADS, FOX_HD),
        logf.reshape(1, batch, seq, FOX_HEADS),
        dn_state.reshape(1, batch, DN_HEADS, DN_D, DN_D),
        p.reshape(batch, seq, P_DN_COLS)[:, seq - (DN_CONV - 1):, :CONV_CH][None],
        mk.reshape(1, batch, MEM_TOKENS, MEM_HEADS, MEM_HD),
        mv.reshape(1, batch, MEM_TOKENS, MEM_HEADS, MEM_HD),
        sfk.reshape(1, nb, 1, FOX_HEADS, FOX_HD),
        sfv.reshape(1, nb, 1, FOX_HEADS, FOX_HD),
        s_logf.reshape(1, nb, 1, FOX_HEADS),
        s_state[None],
        conv_s[None],
    )
```

```python
import functools

import jax
import jax.numpy as jnp
from jax import lax
from jax.experimental import pallas as pl
from jax.experimental.pallas import tpu as pltpu

F32 = jnp.float32
BF16 = jnp.bfloat16
HIGHEST = lax.Precision.HIGHEST
EPS = 1e-6
NEG_INF = float("-inf")

D_MODEL = 4096
DN_HEADS = 16
DN_D = 128
DN_CONV = 4
DN_CHUNK = 64
FOX_HEADS = 8
FOX_HD = 128
MEM_TOKENS = 256
MEM_HEADS = 4
MEM_HD = 256
PEER_HEADS = 8
PEER_NKEYS = 128
PEER_DKEY = 128
PEER_TOPK = 16
PAGE_SIZE = 128

DN_QK = DN_HEADS * DN_D
CONV_CH = 3 * DN_QK
FOX_W = FOX_HEADS * FOX_HD
MEM_W = MEM_HEADS * MEM_HD
_OFF_Z = CONV_CH
_OFF_B = _OFF_Z + DN_QK
_OFF_A = _OFF_B + DN_HEADS
_OFF_FQ = _OFF_A + DN_HEADS
_OFF_FK = _OFF_FQ + FOX_W
_OFF_FV = _OFF_FK + FOX_W
_OFF_FF = _OFF_FV + FOX_W
_OFF_MQ = _OFF_FF + FOX_HEADS
_IN_COLS = _OFF_MQ + MEM_W
P_DN_COLS = CONV_CH + DN_QK
LANE = 128
SMALL_COLS = LANE

VMEM_LIMIT_BYTES = 56 * 1024 * 1024


def _cparams(*sem):
    return pltpu.CompilerParams(dimension_semantics=sem, vmem_limit_bytes=VMEM_LIMIT_BYTES)


def _sigmoid(x):
    return 1.0 / (1.0 + jnp.exp(-x))


def _softplus(x):
    return jnp.maximum(x, 0.0) + jnp.log1p(jnp.exp(-jnp.abs(x)))


def _dot_hi(a, b):
    return jnp.dot(a, b, precision=HIGHEST, preferred_element_type=F32)


def _split_bf16(x):
    hi = x.astype(BF16)
    return hi, (x - hi.astype(F32)).astype(BF16)


def _dot_split(a, b):
    (ah, al), (bh, bl) = a, b
    small = jnp.dot(ah, bl, preferred_element_type=F32) + jnp.dot(al, bh, preferred_element_type=F32)
    return small + jnp.dot(ah, bh, preferred_element_type=F32)


def _dot_nt(a, b):
    return lax.dot_general(a, b, (((1,), (1,)), ((), ())), preferred_element_type=F32)


def _row_to_col(row, n):
    a = lax.broadcasted_iota(jnp.int32, (n, n), 0)
    b = lax.broadcasted_iota(jnp.int32, (n, n), 1)
    return jnp.sum(jnp.where(a == b, row, 0.0), axis=1, keepdims=True)


def _rmsnorm_kernel(x_ref, g_ref, o_ref):
    x = x_ref[...]
    y = x * lax.rsqrt(jnp.mean(x * x, axis=-1, keepdims=True) + EPS) * g_ref[...]
    o_ref[...] = y.astype(o_ref.dtype)


def rmsnorm_rows(x, g, tm, out_dtype=BF16):
    m, k = x.shape
    return pl.pallas_call(
        _rmsnorm_kernel,
        grid=(m // tm,),
        in_specs=[pl.BlockSpec((tm, k), lambda i: (i, 0)), pl.BlockSpec((1, k), lambda i: (0, 0))],
        out_specs=pl.BlockSpec((tm, k), lambda i: (i, 0)),
        out_shape=jax.ShapeDtypeStruct((m, k), out_dtype),
        compiler_params=_cparams("parallel"),
        name="rmsnorm_rows",
    )(x, g.reshape(1, k))


def _head_rmsnorm_kernel(x_ref, g_ref, o_ref, *, heads, hd):
    g = g_ref[...]
    for h in range(heads):
        x = x_ref[:, h * hd:(h + 1) * hd]
        y = x * lax.rsqrt(jnp.mean(x * x, axis=-1, keepdims=True) + EPS) * g
        o_ref[:, h * hd:(h + 1) * hd] = y.astype(o_ref.dtype)


def head_rmsnorm(x, col_block, g, heads, hd, tm, out_dtype=F32):
    m = x.shape[0]
    w = heads * hd
    return pl.pallas_call(
        functools.partial(_head_rmsnorm_kernel, heads=heads, hd=hd),
        grid=(m // tm,),
        in_specs=[pl.BlockSpec((tm, w), lambda i: (i, col_block)), pl.BlockSpec((1, hd), lambda i: (0, 0))],
        out_specs=pl.BlockSpec((tm, w), lambda i: (i, 0)),
        out_shape=jax.ShapeDtypeStruct((m, w), out_dtype),
        compiler_params=_cparams("parallel"),
        name="head_rmsnorm",
    )(x, g.reshape(1, hd))


def _matmul_kernel(*refs, ksizes, has_res):
    n_a = len(ksizes)
    w_ref = refs[n_a]
    o_ref = refs[-1]
    acc = None
    off = 0
    for a_ref, ks in zip(refs[:n_a], ksizes):
        part = jnp.dot(a_ref[...], w_ref[off:off + ks, :], preferred_element_type=F32)
        acc = part if acc is None else acc + part
        off += ks
    if has_res:
        acc = acc + refs[n_a + 1][...]
    o_ref[...] = acc


def matmul(a_list, w, tm, tn, residual=None):
    m = a_list[0].shape[0]
    k, n = w.shape
    ksizes = tuple(a.shape[1] for a in a_list)
    assert sum(ksizes) == k and m % tm == 0 and n % tn == 0
    in_specs = [pl.BlockSpec((tm, ks), lambda i, j: (i, 0)) for ks in ksizes]
    in_specs.append(pl.BlockSpec((k, tn), lambda i, j: (0, j)))
    args = list(a_list) + [w]
    if residual is not None:
        in_specs.append(pl.BlockSpec((tm, tn), lambda i, j: (i, j)))
        args.append(residual)
    return pl.pallas_call(
        functools.partial(_matmul_kernel, ksizes=ksizes, has_res=residual is not None),
        grid=(m // tm, n // tn),
        in_specs=in_specs,
        out_specs=pl.BlockSpec((tm, tn), lambda i, j: (i, j)),
        out_shape=jax.ShapeDtypeStruct((m, n), F32),
        compiler_params=_cparams("parallel", "parallel"),
        name="matmul",
    )(*args)


def _split_w_in_kernel(x_ref, dn_ref, fox_ref, mq_ref, small_ref):
    def aligned_down(c):
        return c - c % LANE

    dn_ref[...] = x_ref[:, 0:P_DN_COLS].astype(BF16)
    f0 = aligned_down(_OFF_FQ)
    fox = x_ref[:, f0:f0 + 3 * FOX_W + LANE]
    fox_ref[...] = fox[:, _OFF_FQ - f0:_OFF_FQ - f0 + 3 * FOX_W].astype(BF16)
    m0 = aligned_down(_OFF_MQ)
    mq = x_ref[:, m0:_IN_COLS]
    mq_ref[...] = mq[:, _OFF_MQ - m0:_OFF_MQ - m0 + MEM_W].astype(BF16)
    assert _OFF_B % LANE == 0 and _OFF_FF - aligned_down(_OFF_FF) == 2 * DN_HEADS
    ba = x_ref[:, _OFF_B:_OFF_B + LANE]
    ff = x_ref[:, aligned_down(_OFF_FF):aligned_down(_OFF_FF) + LANE]
    lane = lax.broadcasted_iota(jnp.int32, ba.shape, 1)
    small = jnp.where(lane < 2 * DN_HEADS, ba, jnp.where(lane < 2 * DN_HEADS + FOX_HEADS, ff, 0.0))
    small_ref[...] = small.astype(BF16)


def split_w_in(w_in3, layer, tr):
    k = w_in3.shape[1]
    widths = (P_DN_COLS, 3 * FOX_W, MEM_W, SMALL_COLS)
    return pl.pallas_call(
        _split_w_in_kernel,
        grid=(k // tr,),
        in_specs=[pl.BlockSpec((None, tr, _IN_COLS), lambda i: (layer, i, 0))],
        out_specs=[pl.BlockSpec((tr, w), lambda i: (i, 0)) for w in widths],
        out_shape=[jax.ShapeDtypeStruct((k, w), BF16) for w in widths],
        compiler_params=_cparams("parallel"),
        name="split_w_in",
    )(w_in3)


def _gates_kernel(p_ref, alog_ref, dtb_ref, fb_ref, o_ref):
    x = p_ref[...]
    lane = lax.broadcasted_iota(jnp.int32, x.shape, 1)
    beta = _sigmoid(x)
    g = -jnp.exp(alog_ref[...]) * _softplus(x + dtb_ref[...])
    logf = -_softplus(-(x + fb_ref[...]))
    out = jnp.where(lane < DN_HEADS, beta,
                    jnp.where(lane < 2 * DN_HEADS, g,
                              jnp.where(lane < 2 * DN_HEADS + FOX_HEADS, logf, 0.0)))
    o_ref[...] = out


def gates(p_small, a_log, dt_bias, f_bias, tm):
    m = p_small.shape[0]

    def pad(v, off):
        return jnp.zeros((1, SMALL_COLS), F32).at[0, off:off + v.shape[0]].set(v)

    row = pl.BlockSpec((1, SMALL_COLS), lambda i: (0, 0))
    return pl.pallas_call(
        _gates_kernel,
        grid=(m // tm,),
        in_specs=[pl.BlockSpec((tm, SMALL_COLS), lambda i: (i, 0)), row, row, row],
        out_specs=pl.BlockSpec((tm, SMALL_COLS), lambda i: (i, 0)),
        out_shape=jax.ShapeDtypeStruct((m, SMALL_COLS), F32),
        compiler_params=_cparams("parallel"),
        name="gates",
    )(p_small, pad(a_log, DN_HEADS), pad(dt_bias, DN_HEADS), pad(f_bias, 2 * DN_HEADS))


def _cumsum_kernel(x_ref, o_ref, *, blocks_per_group):
    x = x_ref[...]
    r, n = x.shape
    a = lax.broadcasted_iota(jnp.int32, (n, n), 0)
    b = lax.broadcasted_iota(jnp.int32, (n, n), 1)
    local = _dot_hi(x, (a <= b).astype(F32))
    tot = jnp.broadcast_to(local[:, n - 1:n], (r, n))
    ra = lax.broadcasted_iota(jnp.int32, (r, r), 0)
    rb = lax.broadcasted_iota(jnp.int32, (r, r), 1)
    earlier = jnp.logical_and(rb < ra, (ra // blocks_per_group) == (rb // blocks_per_group)).astype(F32)
    o_ref[...] = local + _dot_hi(earlier, tot)


def cumsum_time(x, blocks_per_group):
    return pl.pallas_call(
        functools.partial(_cumsum_kernel, blocks_per_group=blocks_per_group),
        out_shape=jax.ShapeDtypeStruct(x.shape, F32),
        compiler_params=pltpu.CompilerParams(vmem_limit_bytes=VMEM_LIMIT_BYTES),
        name="cumsum_time",
    )(x)


def _fox_prompt_kernel(q_ref, k_ref, v_ref, c_ref, o_ref, *, blk, hg, scale):
    i = pl.program_id(2)
    q0 = pl.multiple_of(i * blk, blk)
    hd = FOX_HD
    qs = [q_ref[:, h * hd:(h + 1) * hd] for h in range(hg)]
    cqs = [_row_to_col(c_ref[h, :, pl.ds(q0, blk)], blk) for h in range(hg)]

    def scores(h, k0):
        k = k_ref[pl.ds(k0, blk), h * hd:(h + 1) * hd]
        return _dot_nt(qs[h], k) * scale + (cqs[h] - c_ref[h, :, pl.ds(k0, blk)])

    def update(h, state, s, k0):
        m, l, acc = state
        m_new = jnp.maximum(m, jnp.max(s, axis=1, keepdims=True))
        alpha = jnp.exp(m - m_new)
        p = jnp.exp(s - m_new)
        l = l * alpha + jnp.sum(p, axis=1, keepdims=True)
        v = v_ref[pl.ds(k0, blk), h * hd:(h + 1) * hd]
        acc = acc * alpha + jnp.dot(p.astype(BF16), v, preferred_element_type=F32)
        return m_new, l, acc

    def body(j, carry):
        k0 = pl.multiple_of(j * blk, blk)
        ss = [scores(h, k0) for h in range(hg)]
        return tuple(update(h, carry[h], ss[h], k0) for h in range(hg))

    init = tuple((jnp.full((blk, 1), NEG_INF, F32), jnp.zeros((blk, 1), F32), jnp.zeros((blk, hd), F32))
                 for _ in range(hg))
    carry = lax.fori_loop(0, i, body, init)
    causal = (lax.broadcasted_iota(jnp.int32, (blk, blk), 1) <= lax.broadcasted_iota(jnp.int32, (blk, blk), 0))
    for h in range(hg):
        s = jnp.where(causal, scores(h, q0), NEG_INF)
        _, l, acc = update(h, carry[h], s, q0)
        o_ref[:, h * hd:(h + 1) * hd] = (acc / l).astype(o_ref.dtype)


def fox_prompt(qn, kn, v, c_row, batch, seq, blk, hg):
    nq = seq // blk
    ng = FOX_HEADS // hg
    w = hg * FOX_HD
    kv_spec = pl.BlockSpec((seq, w), lambda b, h, i: (b, h))
    return pl.pallas_call(
        functools.partial(_fox_prompt_kernel, blk=blk, hg=hg, scale=FOX_HD ** -0.5),
        grid=(batch, ng, nq),
        in_specs=[pl.BlockSpec((blk, w), lambda b, h, i: (b * nq + i, h)), kv_spec, kv_spec,
                  pl.BlockSpec((hg, 1, seq), lambda b, h, i: (b * ng + h, 0, 0))],
        out_specs=pl.BlockSpec((blk, w), lambda b, h, i: (b * nq + i, h)),
        out_shape=jax.ShapeDtypeStruct((batch * seq, FOX_W), BF16),
        compiler_params=_cparams("parallel", "parallel", "arbitrary"),
        name="fox_prompt",
    )(qn, kn, v, c_row)


def _dn_prompt_kernel(q_ref, k_ref, v_ref, z_ref, wq_ref, wk_ref, wv_ref, g_ref, b_ref, ng_ref,
                      o_ref, s_out_ref, s_ref, xq_ref, xk_ref, xv_ref, *, tc, chunk, hg):
    t = pl.program_id(2)
    nc = tc // chunk
    pad = 8

    @pl.when(t == 0)
    def _():
        s_ref[...] = jnp.zeros_like(s_ref)
        for buf in (xq_ref, xk_ref, xv_ref):
            buf[0:pad, :] = jnp.zeros((pad, hg * DN_D), F32)

    def conv_silu(x_ref, w_ref, buf_ref):
        u = x_ref[...]
        buf_ref[pad:pad + tc, :] = u
        w = w_ref[...]
        out = buf_ref[pad - 3:pad - 3 + tc, :] * w[0:1, :]
        out = out + buf_ref[pad - 2:pad - 2 + tc, :] * w[1:2, :]
        out = out + buf_ref[pad - 1:pad - 1 + tc, :] * w[2:3, :]
        out = out + u * w[3:4, :]
        buf_ref[0:pad, :] = u[tc - pad:tc, :]
        return out * _sigmoid(out)

    def l2norm(x):
        return x * lax.rsqrt(jnp.sum(x * x, axis=-1, keepdims=True) + EPS)

    qc = conv_silu(q_ref, wq_ref, xq_ref)
    kc = conv_silu(k_ref, wk_ref, xk_ref)
    vc = conv_silu(v_ref, wv_ref, xv_ref)

    ra = lax.broadcasted_iota(jnp.int32, (chunk, chunk), 0)
    rb = lax.broadcasted_iota(jnp.int32, (chunk, chunk), 1)
    tri = ra >= rb
    strict = ra > rb
    eye_f = (ra == rb).astype(F32)
    upper_f = (ra <= rb).astype(F32)

    heads = []
    for hh in range(hg):
        cs = slice(hh * DN_D, (hh + 1) * DN_D)
        heads.append(dict(
            q=l2norm(qc[:, cs]) * (DN_D ** -0.5), k=l2norm(kc[:, cs]), v=vc[:, cs],
            gc=_dot_hi(g_ref[hh], upper_f),
            beta=b_ref[hh], s=s_ref[hh], outs=[]))

    work = []
    for c in range(nc):
        sl = slice(c * chunk, (c + 1) * chunk)
        for hd in heads:
            q, k, v = hd['q'][sl], hd['k'][sl], hd['v'][sl]
            gc_row = hd['gc'][c:c + 1, :]
            gc = _row_to_col(gc_row, chunk)
            beta = _row_to_col(hd['beta'][c:c + 1, :], chunk)
            decay = jnp.where(tri, jnp.exp(jnp.where(tri, gc - gc_row, 0.0)), 0.0)
            qk_b = jnp.concatenate([q, k], axis=0).astype(BF16)
            work.append(dict(hd=hd, q=q, k=k, v=v, gc=gc, beta=beta, decay=decay, qk_b=qk_b,
                             gc_last=gc_row[:, chunk - 1:chunk]))
    for wk in work:
        wk['gram'] = _dot_nt(wk['qk_b'], wk['qk_b'][chunk:, :])
    for wk in work:
        lower = jnp.where(strict, wk['beta'] * wk['gram'][chunk:, :] * wk['decay'], 0.0)
        wk['inv'] = eye_f - lower
        wk['power'] = _split_bf16(lower)
    for wk in work:
        wk['power'] = _split_bf16(_dot_split(wk['power'], wk['power']))
    span = 2
    while span < chunk:
        for wk in work:
            wk['inv'] = wk['inv'] + _dot_split(_split_bf16(wk['inv']), wk['power'])
        span *= 2
        if span < chunk:
            for wk in work:
                wk['power'] = _split_bf16(_dot_split(wk['power'], wk['power']))
    for wk in work:
        egc = jnp.exp(wk['gc'])
        k, beta = wk['k'], wk['beta']
        uw = _dot_split(_split_bf16(wk['inv']),
                        _split_bf16(jnp.concatenate([wk['v'] * beta, k * (beta * egc)], axis=1)))
        wk['u'] = uw[:, :DN_D]
        wk['lhs1'] = jnp.concatenate([uw[:, DN_D:], wk['q'] * egc], axis=0).astype(BF16)
        qk = jnp.where(tri, wk['gram'][:chunk, :] * wk['decay'], 0.0)
        k_dec = k * jnp.exp(wk['gc_last'] - wk['gc'])
        wk['lhs2'] = jnp.concatenate([qk, k_dec.T], axis=0).astype(BF16)
        wk['g_last'] = jnp.exp(wk['gc_last'])
    for wk in work:
        hd = wk['hd']
        s = hd['s']
        ws = jnp.dot(wk['lhs1'], s.astype(BF16), preferred_element_type=F32)
        v_new = wk['u'] - ws[:chunk, :]
        upd = jnp.dot(wk['lhs2'], v_new.astype(BF16), preferred_element_type=F32)
        hd['outs'].append(ws[chunk:, :] + upd[:chunk, :])
        hd['s'] = s * wk['g_last'] + upd[chunk:, :]

    ng = ng_ref[...]
    for hh, hd in enumerate(heads):
        cs = slice(hh * DN_D, (hh + 1) * DN_D)
        s_ref[hh] = hd['s']
        s_out_ref[hh] = hd['s']
        o = jnp.concatenate(hd['outs'], axis=0)
        o = o * lax.rsqrt(jnp.mean(o * o, axis=-1, keepdims=True) + EPS) * ng
        z = z_ref[:, cs]
        o_ref[:, cs] = (o * (z * _sigmoid(z))).astype(o_ref.dtype)


def dn_prompt(p, conv_w, g_chunks, beta_chunks, norm_g, batch, seq, tc, hg):
    nt = seq // tc
    nc = tc // DN_CHUNK
    ng = DN_HEADS // hg
    w = hg * DN_D

    def col(off):
        return pl.BlockSpec((tc, w), lambda b, h, t: (b * nt + t, off * ng + h))

    def wcol(off):
        return pl.BlockSpec((DN_CONV, w), lambda b, h, t: (0, off * ng + h))

    gspec = pl.BlockSpec((hg, nc, DN_CHUNK), lambda b, h, t: (b * ng + h, t, 0))
    return pl.pallas_call(
        functools.partial(_dn_prompt_kernel, tc=tc, chunk=DN_CHUNK, hg=hg),
        grid=(batch, ng, nt),
        in_specs=[col(0), col(1), col(2), col(3), wcol(0), wcol(1), wcol(2), gspec, gspec,
                  pl.BlockSpec((1, DN_D), lambda b, h, t: (0, 0))],
        out_specs=[pl.BlockSpec((tc, w), lambda b, h, t: (b * nt + t, h)),
                   pl.BlockSpec((hg, DN_D, DN_D), lambda b, h, t: (b * ng + h, 0, 0))],
        out_shape=[jax.ShapeDtypeStruct((batch * seq, DN_QK), BF16),
                   jax.ShapeDtypeStruct((batch * DN_HEADS, DN_D, DN_D), F32)],
        scratch_shapes=[pltpu.VMEM((hg, DN_D, DN_D), F32)] + [pltpu.VMEM((tc + 8, w), F32)] * 3,
        compiler_params=_cparams("parallel", "parallel", "arbitrary"),
        name="dn_prompt",
    )(p, p, p, p, conv_w, conv_w, conv_w, g_chunks, beta_chunks, norm_g.reshape(1, DN_D))


def _dn_sample_kernel(q_ref, k_ref, v_ref, z_ref, cq_ref, ck_ref, cv_ref, wq_ref, wk_ref, wv_ref,
                      g_ref, b_ref, ng_ref, s_ref, o_ref, s_out_ref, *, bb):
    def conv_silu(x_ref, c_ref, w_ref):
        w = w_ref[...]
        out = c_ref[:, 0, :] * w[0:1, :]
        out = out + c_ref[:, 1, :] * w[1:2, :]
        out = out + c_ref[:, 2, :] * w[2:3, :]
        out = out + x_ref[...] * w[3:4, :]
        return out * _sigmoid(out)

    def l2norm(x):
        return x * lax.rsqrt(jnp.sum(x * x, axis=-1, keepdims=True) + EPS)

    q = l2norm(conv_silu(q_ref, cq_ref, wq_ref)) * (DN_D ** -0.5)
    k = l2norm(conv_silu(k_ref, ck_ref, wk_ref))
    v = conv_silu(v_ref, cv_ref, wv_ref)
    q_t = q.T
    k_t = k.T
    decay = jnp.exp(g_ref[0])
    beta = b_ref[0]
    rows = []
    for b in range(bb):
        s = s_ref[b, 0] * decay[b:b + 1, :]
        k_col = k_t[:, b:b + 1]
        kv = jnp.sum(k_col * s, axis=0, keepdims=True)
        s = s + k_col * ((v[b:b + 1, :] - kv) * beta[b:b + 1, :])
        s_out_ref[b, 0] = s
        rows.append(jnp.sum(q_t[:, b:b + 1] * s, axis=0, keepdims=True))
    o = jnp.concatenate(rows, axis=0)
    o = o * lax.rsqrt(jnp.mean(o * o, axis=-1, keepdims=True) + EPS) * ng_ref[...]
    z = z_ref[...]
    o_ref[...] = (o * (z * _sigmoid(z))).astype(o_ref.dtype)


def dn_sample(p, state_conv, conv_w, g_t, beta_t, norm_g, state, bb):
    nb = p.shape[0]
    hb = DN_HEADS

    def col(off):
        return pl.BlockSpec((bb, DN_D), lambda i, h: (i, off + h))

    def ccol(off):
        return pl.BlockSpec((bb, DN_CONV - 1, DN_D), lambda i, h: (i, 0, off + h))

    def wcol(off):
        return pl.BlockSpec((DN_CONV, DN_D), lambda i, h: (0, off + h))

    gspec = pl.BlockSpec((1, bb, 1), lambda i, h: (h, i, 0))
    sspec = pl.BlockSpec((bb, 1, DN_D, DN_D), lambda i, h: (i, h, 0, 0))
    return pl.pallas_call(
        functools.partial(_dn_sample_kernel, bb=bb),
        grid=(nb // bb, DN_HEADS),
        in_specs=[col(0), col(hb), col(2 * hb), col(3 * hb), ccol(0), ccol(hb), ccol(2 * hb),
                  wcol(0), wcol(hb), wcol(2 * hb), gspec, gspec,
                  pl.BlockSpec((1, DN_D), lambda i, h: (0, 0)), sspec],
        out_specs=[pl.BlockSpec((bb, DN_D), lambda i, h: (i, h)), sspec],
        out_shape=[jax.ShapeDtypeStruct((nb, DN_QK), BF16), jax.ShapeDtypeStruct(state.shape, F32)],
        compiler_params=_cparams("parallel", "parallel"),
        name="dn_sample",
    )(p, p, p, p, state_conv, state_conv, state_conv, conv_w, conv_w, conv_w, g_t, beta_t,
      norm_g.reshape(1, DN_D), state)


def _fox_sample_kernel(pt_ref, q_ref, kn_ref, vn_ref, lfn_ref, *refs, pp, scale):
    k_refs = refs[0:pp]
    v_refs = refs[pp:2 * pp]
    lf_refs = refs[2 * pp:3 * pp]
    o_ref = refs[3 * pp]
    m_ref, l_ref, acc_ref, carry_ref = refs[3 * pp + 1:]
    j = pl.program_id(1)
    qs = q_ref[0] * scale

    @pl.when(j == 0)
    def _():
        m_ref[...] = jnp.broadcast_to(jnp.sum(qs * kn_ref[0], axis=1, keepdims=True), m_ref.shape)
        l_ref[...] = jnp.ones_like(l_ref)
        acc_ref[...] = vn_ref[0]
        carry_ref[...] = lfn_ref[0]

    ra = lax.broadcasted_iota(jnp.int32, (PAGE_SIZE, PAGE_SIZE), 0)
    rb = lax.broadcasted_iota(jnp.int32, (PAGE_SIZE, PAGE_SIZE), 1)
    later_f = (ra > rb).astype(F32)
    shape3 = (PAGE_SIZE, FOX_HEADS, FOX_HD)
    diag3 = lax.broadcasted_iota(jnp.int32, shape3, 0) == lax.broadcasted_iota(jnp.int32, shape3, 2)
    carry = carry_ref[...]
    logits = []
    for i in range(pp):
        lf = lf_refs[i][0]
        bias = _dot_hi(lf, later_f) + carry
        carry = carry + jnp.sum(lf, axis=1, keepdims=True)
        s3 = jnp.sum(k_refs[i][0] * qs[None] + jnp.where(diag3, bias[None], 0.0), axis=2, keepdims=True)
        logits.append(jnp.broadcast_to(s3, shape3))
    carry_ref[...] = carry
    m_old = m_ref[...]
    m_new = m_old
    for s3 in logits:
        m_new = jnp.maximum(m_new, jnp.max(s3, axis=0))
    alpha = jnp.exp(m_old - m_new)
    l_new = l_ref[...] * alpha
    acc = acc_ref[...] * alpha
    for i, s3 in enumerate(logits):
        p3 = jnp.exp(s3 - m_new[None])
        l_new = l_new + jnp.sum(p3, axis=0)
        acc = acc + jnp.sum(p3 * v_refs[i][0], axis=0)
    l_ref[...] = l_new
    acc_ref[...] = acc
    m_ref[...] = m_new

    @pl.when(j == pl.num_programs(1) - 1)
    def _():
        o_ref[0] = (acc_ref[...] / l_ref[...]).astype(o_ref.dtype)


def fox_sample(q, k_new, v_new, lf_new, k_pool, v_pool, lf_pool_t, page_table, pp):
    nb, n_pages = page_table.shape
    assert n_pages % pp == 0
    steps = n_pages // pp

    def page_idx4(i):
        return lambda b, j, pt: (pt[b, n_pages - 1 - (j * pp + i)], 0, 0, 0)

    def page_idx3(i):
        return lambda b, j, pt: (pt[b, n_pages - 1 - (j * pp + i)], 0, 0)

    tok = lambda b, j, pt: (b, 0, 0)
    in_specs = [pl.BlockSpec((1, FOX_HEADS, FOX_HD), tok)] * 3 + [pl.BlockSpec((1, FOX_HEADS, 1), tok)]
    in_specs += [pl.BlockSpec((1, PAGE_SIZE, FOX_HEADS, FOX_HD), page_idx4(i)) for i in range(pp)]
    in_specs += [pl.BlockSpec((1, PAGE_SIZE, FOX_HEADS, FOX_HD), page_idx4(i)) for i in range(pp)]
    in_specs += [pl.BlockSpec((1, FOX_HEADS, PAGE_SIZE), page_idx3(i)) for i in range(pp)]
    grid_spec = pltpu.PrefetchScalarGridSpec(
        num_scalar_prefetch=1,
        grid=(nb, steps),
        in_specs=in_specs,
        out_specs=pl.BlockSpec((1, FOX_HEADS, FOX_HD), tok),
        scratch_shapes=[pltpu.VMEM((FOX_HEADS, FOX_HD), F32), pltpu.VMEM((FOX_HEADS, FOX_HD), F32),
                        pltpu.VMEM((FOX_HEADS, FOX_HD), F32), pltpu.VMEM((FOX_HEADS, 1), F32)],
    )
    return pl.pallas_call(
        functools.partial(_fox_sample_kernel, pp=pp, scale=FOX_HD ** -0.5),
        grid_spec=grid_spec,
        out_shape=jax.ShapeDtypeStruct((nb, FOX_HEADS, FOX_HD), BF16),
        compiler_params=_cparams("parallel", "arbitrary"),
        name="fox_sample",
    )(page_table, q, k_new, v_new, lf_new, *([k_pool] * pp), *([v_pool] * pp), *([lf_pool_t] * pp))


def _mem_prompt_kernel(q_ref, k_ref, v_ref, g_ref, o_ref, *, scale):
    g = g_ref[...]
    for h in range(MEM_HEADS):
        cs = slice(h * MEM_HD, (h + 1) * MEM_HD)
        q = q_ref[:, cs]
        qn = (q * lax.rsqrt(jnp.mean(q * q, axis=-1, keepdims=True) + EPS) * g).astype(BF16)
        s = _dot_nt(qn, k_ref[:, cs]) * scale
        p = jnp.exp(s - jnp.max(s, axis=1, keepdims=True))
        p = p / jnp.sum(p, axis=1, keepdims=True)
        o_ref[:, cs] = jnp.dot(p.astype(BF16), v_ref[:, cs], preferred_element_type=F32).astype(o_ref.dtype)


def mem_attend_prompt(p, mk, mv, qn_g, batch, seq, tq):
    nq = seq // tq
    kv = pl.BlockSpec((MEM_TOKENS, MEM_W), lambda b, i: (b, 0))
    return pl.pallas_call(
        functools.partial(_mem_prompt_kernel, scale=MEM_HD ** -0.5),
        grid=(batch, nq),
        in_specs=[pl.BlockSpec((tq, MEM_W), lambda b, i: (b * nq + i, 0)), kv, kv,
                  pl.BlockSpec((1, MEM_HD), lambda b, i: (0, 0))],
        out_specs=pl.BlockSpec((tq, MEM_W), lambda b, i: (b * nq + i, 0)),
        out_shape=jax.ShapeDtypeStruct((batch * seq, MEM_W), BF16),
        compiler_params=_cparams("parallel", "parallel"),
        name="mem_prompt",
    )(p, mk, mv, qn_g.reshape(1, MEM_HD))


def _mem_sample_kernel(q_ref, k_ref, v_ref, g_ref, o_ref, *, scale):
    q = q_ref[0]
    qn = q * lax.rsqrt(jnp.mean(q * q, axis=-1, keepdims=True) + EPS) * g_ref[...] * scale
    s3 = jnp.sum(k_ref[0] * qn[None], axis=2, keepdims=True)
    p3 = jnp.exp(s3 - jnp.max(s3, axis=0)[None])
    o = jnp.sum(p3 * v_ref[0], axis=0) / jnp.sum(p3, axis=0)
    o_ref[0] = o.astype(o_ref.dtype)


def mem_sample(q, mk, mv, qn_g):
    nb = q.shape[0]
    tok = pl.BlockSpec((1, MEM_HEADS, MEM_HD), lambda b: (b, 0, 0))
    kv = pl.BlockSpec((1, MEM_TOKENS, MEM_HEADS, MEM_HD), lambda b: (b, 0, 0, 0))
    return pl.pallas_call(
        functools.partial(_mem_sample_kernel, scale=MEM_HD ** -0.5),
        grid=(nb,),
        in_specs=[tok, kv, kv, pl.BlockSpec((1, MEM_HD), lambda b: (0, 0))],
        out_specs=tok,
        out_shape=jax.ShapeDtypeStruct((nb, MEM_HEADS, MEM_HD), BF16),
        compiler_params=_cparams("parallel"),
        name="mem_sample",
    )(q, mk, mv, qn_g.reshape(1, MEM_HD))


PEER_NO_RANK = float(PEER_NKEYS)


def _top_values(s, count, with_rank=False):
    rows = s.shape[0]
    idx = lax.broadcasted_iota(jnp.int32, s.shape, 0).astype(F32)
    rank = jnp.full(s.shape, PEER_NO_RANK, F32) if with_rank else None
    vals = []
    for r in range(count):
        m = jnp.max(s, axis=0, keepdims=True)
        picked = idx == jnp.min(jnp.where(s == m, idx, float(rows)), axis=0, keepdims=True)
        s = jnp.where(picked, NEG_INF, s)
        if with_rank:
            rank = jnp.where(picked, float(r), rank)
        vals.append(m)
    return jnp.concatenate(vals, axis=0), rank


def _peer_stats_kernel(hn_ref, wq_ref, keys_ref, cnt_ref, g1_ref, r2_ref, e2_ref, st_ref, top_ref, rank_ref,
                       cnt_scr, g1_scr):
    nk = PEER_NKEYS
    kk = PEER_TOPK
    q = jnp.dot(hn_ref[...], wq_ref[...], preferred_element_type=F32)
    st_ref[...] = _dot_nt(keys_ref[...], q.astype(BF16))

    def half_body(c, carry):
        r0 = pl.multiple_of(c * nk, nk)
        t0 = pl.multiple_of(c * kk, kk)
        vals, rank = _top_values(st_ref[pl.ds(r0, nk), :], kk, with_rank=True)
        top_ref[pl.ds(t0, kk), :] = vals
        rank_ref[pl.ds(r0, nk), :] = rank
        return carry

    lax.fori_loop(0, 2 * PEER_HEADS, half_body, 0)

    def head_body(h, carry):
        t0 = pl.multiple_of(h * 2 * kk, 2 * kk)
        v1 = top_ref[pl.ds(t0, kk), :]
        v2 = top_ref[pl.ds(t0 + kk, kk), :]
        cand = jnp.concatenate([v1[0:1, :] + v2] + [v1[a:a + 1, :] + v2[0:8, :] for a in range(1, 8)]
                               + [v1[8:kk, :] + v2[0:1, :]], axis=0)
        cv, _ = _top_values(cand, kk)
        z = jnp.sum(jnp.exp(cv - cv[0:1, :]), axis=0, keepdims=True)
        tau = cv[kk - 1:kk, :]
        r0 = pl.multiple_of(h * 2 * nk, 2 * nk)
        s1 = st_ref[pl.ds(r0, nk), :]
        s2 = st_ref[pl.ds(r0 + nk, nk), :]
        rank1 = rank_ref[pl.ds(r0, nk), :]
        cnt = jnp.zeros_like(s1)
        for a in range(kk):
            n_hit = jnp.sum(((v1[a:a + 1, :] + v2) >= tau).astype(F32), axis=0, keepdims=True)
            cnt = jnp.where(rank1 == float(a), n_hit, cnt)
        cnt_scr[h] = cnt
        g1_scr[h] = jnp.exp(s1 - v1[0:1, :]) / z
        r2_ref[h] = rank_ref[pl.ds(r0 + nk, nk), :].astype(BF16)
        e2_ref[h] = jnp.exp(s2 - v2[0:1, :]).astype(BF16)
        return carry

    lax.fori_loop(0, PEER_HEADS, head_body, 0)
    for h in range(PEER_HEADS):
        cnt_ref[:, h, :] = cnt_scr[h]
        g1_ref[:, h, :] = g1_scr[h]


def peer_stats(hn, w_q, keys_t, mt):
    m = hn.shape[0]
    nrow = PEER_HEADS * 2 * PEER_NKEYS
    hspec = pl.BlockSpec((PEER_HEADS, PEER_NKEYS, mt), lambda i: (0, 0, i))
    kspec = pl.BlockSpec((PEER_NKEYS, PEER_HEADS, mt), lambda i: (0, 0, i))
    f32_shape = jax.ShapeDtypeStruct((PEER_NKEYS, PEER_HEADS, m), F32)
    bf16_shape = jax.ShapeDtypeStruct((PEER_HEADS, PEER_NKEYS, m), BF16)
    return pl.pallas_call(
        _peer_stats_kernel,
        grid=(m // mt,),
        in_specs=[pl.BlockSpec((mt, D_MODEL), lambda i: (i, 0)),
                  pl.BlockSpec(w_q.shape, lambda i: (0, 0)),
                  pl.BlockSpec(keys_t.shape, lambda i: (0, 0))],
        out_specs=[kspec, kspec, hspec, hspec],
        out_shape=[f32_shape, f32_shape, bf16_shape, bf16_shape],
        scratch_shapes=[pltpu.VMEM((nrow, mt), F32), pltpu.VMEM((2 * PEER_HEADS * PEER_TOPK, mt), F32),
                        pltpu.VMEM((nrow, mt), F32), pltpu.VMEM((PEER_HEADS, PEER_NKEYS, mt), F32),
                        pltpu.VMEM((PEER_HEADS, PEER_NKEYS, mt), F32)],
        compiler_params=_cparams("parallel"),
        name="peer_stats",
    )(hn, w_q, keys_t)


def _peer_gates(cnt_ref, g1_ref, r2_ref, e2_ref, gate_ref, n_a):
    mt = gate_ref.shape[1]
    rows = 16
    heads = range(PEER_HEADS)
    for a in range(n_a):
        for c in range(mt // LANE):
            cs = slice(c * LANE, (c + 1) * LANE)
            cnt = [jnp.broadcast_to(cnt_ref[a, h:h + 1, cs], (rows, LANE)).astype(BF16) for h in heads]
            g1 = [jnp.broadcast_to(g1_ref[a, h:h + 1, cs], (rows, LANE)).astype(BF16) for h in heads]
            for r in range(PEER_NKEYS // rows):
                rs = slice(r * rows, (r + 1) * rows)
                w = None
                for h in heads:
                    term = jnp.where(r2_ref[h, rs, cs] < cnt[h], e2_ref[h, rs, cs], 0.0) * g1[h]
                    w = term if w is None else w + term
                gate_ref[a * PEER_NKEYS + r * rows:a * PEER_NKEYS + (r + 1) * rows, cs] = w


def _peer_mix_kernel(hn_ref, res_ref, u_ref, v_ref, cnt_ref, g1_ref, cntn_ref, g1n_ref, r2_ref, e2_ref,
                     o_ref, gate_a_ref, gate_b_ref, *, et):
    e = pl.program_id(1)
    n_a = et // PEER_NKEYS

    @pl.when(e == 0)
    def _():
        o_ref[...] = res_ref[...]
        _peer_gates(cnt_ref, g1_ref, r2_ref, e2_ref, gate_a_ref, n_a)

    def step(gate_cur_ref, gate_next_ref):
        _peer_gates(cntn_ref, g1n_ref, r2_ref, e2_ref, gate_next_ref, n_a)
        act_t = _dot_nt(u_ref[...], hn_ref[...])
        gel_t = 0.5 * act_t * (1.0 + lax.erf(act_t * (0.5 ** 0.5)))
        hmat = (gate_cur_ref[...].astype(F32) * gel_t).astype(BF16).T
        o_ref[...] += jnp.dot(hmat, v_ref[...], preferred_element_type=F32)

    @pl.when(e % 2 == 0)
    def _():
        step(gate_a_ref, gate_b_ref)

    @pl.when(e % 2 == 1)
    def _():
        step(gate_b_ref, gate_a_ref)


def peer_mix(hn, res, u_tab, v_tab, cnt_t, g1_t, r2, e2, mt, et):
    m = hn.shape[0]
    n_exp = u_tab.shape[0]
    n_e = n_exp // et
    a_per = et // PEER_NKEYS
    once = pl.Buffered(1)
    sel1 = pl.BlockSpec((a_per, PEER_HEADS, mt), lambda i, e: (e, 0, i))
    sel1_next = pl.BlockSpec((a_per, PEER_HEADS, mt), lambda i, e: (jnp.minimum(e + 1, n_e - 1), 0, i))
    sel2 = pl.BlockSpec((PEER_HEADS, PEER_NKEYS, mt), lambda i, e: (0, 0, i), pipeline_mode=once)
    tab = pl.BlockSpec((et, D_MODEL), lambda i, e: (e, 0))
    return pl.pallas_call(
        functools.partial(_peer_mix_kernel, et=et),
        grid=(m // mt, n_e),
        in_specs=[pl.BlockSpec((mt, D_MODEL), lambda i, e: (i, 0), pipeline_mode=once),
                  pl.BlockSpec((mt, D_MODEL), lambda i, e: (i, 0), pipeline_mode=once), tab, tab,
                  sel1, sel1, sel1_next, sel1_next, sel2, sel2],
        out_specs=pl.BlockSpec((mt, D_MODEL), lambda i, e: (i, 0)),
        out_shape=jax.ShapeDtypeStruct((m, D_MODEL), F32),
        scratch_shapes=[pltpu.VMEM((et, mt), BF16), pltpu.VMEM((et, mt), BF16)],
        compiler_params=_cparams("parallel", "arbitrary"),
        name="peer_mix",
    )(hn, res, u_tab, v_tab, cnt_t, g1_t, cnt_t, g1_t, r2, e2)


def _prep_weights(w_in3, layer, w_mem_kv, w_out, peer_w_q, peer_sub_keys, peer_u, peer_v):
    w_dn, w_fox, w_mq, w_small = split_w_in(w_in3, layer, 128)
    half = PEER_DKEY // 2
    pairs = 2 * PEER_HEADS
    sk = peer_sub_keys.reshape(pairs, PEER_NKEYS, half)
    eye = jnp.eye(pairs, dtype=F32)
    keys_t = (eye[:, None, :, None] * sk[:, :, None, :]).reshape(pairs * PEER_NKEYS, pairs * half).astype(BF16)
    return dict(w_dn=w_dn, w_fox=w_fox, w_mq=w_mq, w_small=w_small, w_mem_kv=w_mem_kv.astype(BF16), w_out=w_out.astype(BF16),
                peer_w_q=peer_w_q.astype(BF16), keys_t=keys_t, peer_u=peer_u.astype(BF16),
                peer_v=peer_v.astype(BF16))


def _tile(m, pref):
    return pref if m % pref == 0 else m


def _project(x2, ln_g, wts):
    m = x2.shape[0]
    xn = rmsnorm_rows(x2, ln_g, _tile(m, 256))
    tm = _tile(m, 1024)
    p_dn = matmul([xn], wts['w_dn'], tm, 512)
    p_fox = matmul([xn], wts['w_fox'], tm, 512)
    p_mq = matmul([xn], wts['w_mq'], tm, 512)
    ps = matmul([xn], wts['w_small'], tm, SMALL_COLS)
    return p_dn, p_fox, p_mq, ps


def _channel_mix(x2, o_dn, o_fox, o_mem, wts, ln_ffn_g):
    m = x2.shape[0]
    h = matmul([o_dn, o_fox, o_mem], wts['w_out'], _tile(m, 1024), 512, residual=x2)
    hn = rmsnorm_rows(h, ln_ffn_g, _tile(m, 256))
    cnt_t, g1_t, r2, e2 = peer_stats(hn, wts['peer_w_q'], wts['keys_t'], _tile(m, 512))
    return peer_mix(hn, h, wts['peer_u'], wts['peer_v'], cnt_t, g1_t, r2, e2, _tile(m, 512), 512)


def kernel(x_prompt, x_sample, cache_fox_k, cache_fox_v, cache_fox_logf, state_delta, state_conv, cache_mem_k, cache_mem_v, page_table, mem_prompt, ln_mix_g, w_in, conv_w, dn_a_log, dn_dt_bias, dn_norm_g, fox_f_bias, fox_qn_g, fox_kn_g, ln_mem_g, w_mem_kv, mem_qn_g, mem_kn_g, w_out, ln_ffn_g, peer_w_q, peer_sub_keys, peer_u, peer_v):
    depth = w_in.shape[0]
    assert depth == 1
    l = 0
    batch, seq, _ = x_prompt.shape
    nb = x_sample.shape[0]
    m_p = batch * seq
    wts = _prep_weights(w_in, l, w_mem_kv[l], w_out[l], peer_w_q[l], peer_sub_keys[l], peer_u[l], peer_v[l])

    x2 = x_prompt.reshape(m_p, D_MODEL)
    p, p_fox, p_mq, ps = _project(x2, ln_mix_g[l], wts)
    gt = gates(ps, dn_a_log[l], dn_dt_bias[l], fox_f_bias[l], 1024)
    beta = gt[:, :DN_HEADS]
    gdec = gt[:, DN_HEADS:2 * DN_HEADS]
    logf = gt[:, 2 * DN_HEADS:2 * DN_HEADS + FOX_HEADS]

    def to_chunks(a):
        return a.reshape(batch, seq, DN_HEADS).transpose(0, 2, 1).reshape(batch * DN_HEADS, seq // DN_CHUNK, DN_CHUNK)

    o_dn, dn_state = dn_prompt(p, conv_w[l], to_chunks(gdec), to_chunks(beta), dn_norm_g[l], batch, seq, 512, 8)

    fqn = head_rmsnorm(p_fox, 0, fox_qn_g[l], FOX_HEADS, FOX_HD, 512, out_dtype=BF16)
    fkn = head_rmsnorm(p_fox, 1, fox_kn_g[l], FOX_HEADS, FOX_HD, 512)
    fv = p_fox[:, 2 * FOX_W:]
    lf_rows = logf.reshape(batch, seq, FOX_HEADS).transpose(0, 2, 1).reshape(batch * FOX_HEADS * (seq // LANE), LANE)
    c_row = cumsum_time(lf_rows, seq // LANE).reshape(batch * FOX_HEADS, 1, seq)
    o_fox = fox_prompt(fqn, fkn.astype(BF16), fv.astype(BF16), c_row, batch, seq, 512, 2)

    mem2 = mem_prompt.reshape(batch * MEM_TOKENS, D_MODEL)
    memn = rmsnorm_rows(mem2, ln_mem_g[l], 256)
    mkv = matmul([memn], wts['w_mem_kv'], batch * MEM_TOKENS, 512)
    mk = head_rmsnorm(mkv, 0, mem_kn_g[l], MEM_HEADS, MEM_HD, 256)
    mv = mkv[:, MEM_W:]
    o_mem = mem_attend_prompt(p_mq, mk.astype(BF16), mv.astype(BF16), mem_qn_g[l], batch, seq, 512)

    y_p = _channel_mix(x2, o_dn, o_fox, o_mem, wts, ln_ffn_g[l])

    xs = x_sample.reshape(nb, D_MODEL)
    sp, sp_fox, sp_mq, sps = _project(xs, ln_mix_g[l], wts)
    sgt = gates(sps, dn_a_log[l], dn_dt_bias[l], fox_f_bias[l], nb)
    s_beta = sgt[:, :DN_HEADS].T.reshape(DN_HEADS, nb, 1)
    s_g = sgt[:, DN_HEADS:2 * DN_HEADS].T.reshape(DN_HEADS, nb, 1)
    s_logf = sgt[:, 2 * DN_HEADS:2 * DN_HEADS + FOX_HEADS]
    so_dn, s_state = dn_sample(sp, state_conv[l], conv_w[l], s_g, s_beta, dn_norm_g[l], state_delta[l], 32)
    conv_s = jnp.concatenate([state_conv[l][:, 1:, :], sp[:, None, :CONV_CH]], axis=1)

    sfq = head_rmsnorm(sp_fox, 0, fox_qn_g[l], FOX_HEADS, FOX_HD, nb)
    sfk = head_rmsnorm(sp_fox, 1, fox_kn_g[l], FOX_HEADS, FOX_HD, nb)
    sfv = sp_fox[:, 2 * FOX_W:]
    so_fox = fox_sample(sfq.reshape(nb, FOX_HEADS, FOX_HD), sfk.reshape(nb, FOX_HEADS, FOX_HD),
                        sfv.reshape(nb, FOX_HEADS, FOX_HD), s_logf.reshape(nb, FOX_HEADS, 1),
                        cache_fox_k[l], cache_fox_v[l], jnp.swapaxes(cache_fox_logf[l], 1, 2),
                        page_table, 8).reshape(nb, FOX_W)
    smq = sp_mq.reshape(nb, MEM_HEADS, MEM_HD)
    so_mem = mem_sample(smq, cache_mem_k[l], cache_mem_v[l], mem_qn_g[l]).reshape(nb, MEM_W)
    y_s = _channel_mix(xs, so_dn, so_fox, so_mem, wts, ln_ffn_g[l])

    return (
        y_p.reshape(batch, seq, D_MODEL),
        y_s.reshape(nb, 1, D_MODEL),
        fkn.reshape(1, batch, seq, FOX_HEADS, FOX_HD),
        fv.reshape(1, batch, seq, FOX_HEADS, FOX_HD),
        logf.reshape(1, batch, seq, FOX_HEADS),
        dn_state.reshape(1, batch, DN_HEADS, DN_D, DN_D),
        p.reshape(batch, seq, P_DN_COLS)[:, seq - (DN_CONV - 1):, :CONV_CH][None],
        mk.reshape(1, batch, MEM_TOKENS, MEM_HEADS, MEM_HD),
        mv.reshape(1, batch, MEM_TOKENS, MEM_HEADS, MEM_HD),
        sfk.reshape(1, nb, 1, FOX_HEADS, FOX_HD),
        sfv.reshape(1, nb, 1, FOX_HEADS, FOX_HD),
        s_logf.reshape(1, nb, 1, FOX_HEADS),
        s_state[None],
        conv_s[None],
    )
```

```python
import functools

import jax
import jax.numpy as jnp
from jax import lax
from jax.experimental import pallas as pl
from jax.experimental.pallas import tpu as pltpu

F32 = jnp.float32
BF16 = jnp.bfloat16
HIGHEST = lax.Precision.HIGHEST
EPS = 1e-6
NEG_INF = float("-inf")

D_MODEL = 4096
DN_HEADS = 16
DN_D = 128
DN_CONV = 4
DN_CHUNK = 64
FOX_HEADS = 8
FOX_HD = 128
MEM_TOKENS = 256
MEM_HEADS = 4
MEM_HD = 256
PEER_HEADS = 8
PEER_NKEYS = 128
PEER_DKEY = 128
PEER_TOPK = 16
PAGE_SIZE = 128

DN_QK = DN_HEADS * DN_D
CONV_CH = 3 * DN_QK
FOX_W = FOX_HEADS * FOX_HD
MEM_W = MEM_HEADS * MEM_HD
_OFF_Z = CONV_CH
_OFF_B = _OFF_Z + DN_QK
_OFF_A = _OFF_B + DN_HEADS
_OFF_FQ = _OFF_A + DN_HEADS
_OFF_FK = _OFF_FQ + FOX_W
_OFF_FV = _OFF_FK + FOX_W
_OFF_FF = _OFF_FV + FOX_W
_OFF_MQ = _OFF_FF + FOX_HEADS
_IN_COLS = _OFF_MQ + MEM_W
P_DN_COLS = CONV_CH + DN_QK
LANE = 128
SMALL_COLS = LANE

VMEM_LIMIT_BYTES = 56 * 1024 * 1024


def _cparams(*sem):
    return pltpu.CompilerParams(dimension_semantics=sem, vmem_limit_bytes=VMEM_LIMIT_BYTES)


def _sigmoid(x):
    return 1.0 / (1.0 + jnp.exp(-x))


def _softplus(x):
    return jnp.maximum(x, 0.0) + jnp.log1p(jnp.exp(-jnp.abs(x)))


def _dot_hi(a, b):
    return jnp.dot(a, b, precision=HIGHEST, preferred_element_type=F32)


def _split_bf16(x):
    hi = x.astype(BF16)
    return hi, (x - hi.astype(F32)).astype(BF16)


def _dot_split(a, b):
    (ah, al), (bh, bl) = a, b
    small = jnp.dot(ah, bl, preferred_element_type=F32) + jnp.dot(al, bh, preferred_element_type=F32)
    return small + jnp.dot(ah, bh, preferred_element_type=F32)


def _dot_nt(a, b):
    return lax.dot_general(a, b, (((1,), (1,)), ((), ())), preferred_element_type=F32)


def _row_to_col(row, n):
    a = lax.broadcasted_iota(jnp.int32, (n, n), 0)
    b = lax.broadcasted_iota(jnp.int32, (n, n), 1)
    return jnp.sum(jnp.where(a == b, row, 0.0), axis=1, keepdims=True)


def _rmsnorm_kernel(x_ref, g_ref, o_ref):
    x = x_ref[...]
    y = x * lax.rsqrt(jnp.mean(x * x, axis=-1, keepdims=True) + EPS) * g_ref[...]
    o_ref[...] = y.astype(o_ref.dtype)


def rmsnorm_rows(x, g, tm, out_dtype=BF16):
    m, k = x.shape
    return pl.pallas_call(
        _rmsnorm_kernel,
        grid=(m // tm,),
        in_specs=[pl.BlockSpec((tm, k), lambda i: (i, 0)), pl.BlockSpec((1, k), lambda i: (0, 0))],
        out_specs=pl.BlockSpec((tm, k), lambda i: (i, 0)),
        out_shape=jax.ShapeDtypeStruct((m, k), out_dtype),
        compiler_params=_cparams("parallel"),
        name="rmsnorm_rows",
    )(x, g.reshape(1, k))


def _head_rmsnorm_kernel(x_ref, g_ref, o_ref, *, heads, hd):
    g = g_ref[...]
    for h in range(heads):
        x = x_ref[:, h * hd:(h + 1) * hd]
        y = x * lax.rsqrt(jnp.mean(x * x, axis=-1, keepdims=True) + EPS) * g
        o_ref[:, h * hd:(h + 1) * hd] = y.astype(o_ref.dtype)


def head_rmsnorm(x, col_block, g, heads, hd, tm, out_dtype=F32):
    m = x.shape[0]
    w = heads * hd
    return pl.pallas_call(
        functools.partial(_head_rmsnorm_kernel, heads=heads, hd=hd),
        grid=(m // tm,),
        in_specs=[pl.BlockSpec((tm, w), lambda i: (i, col_block)), pl.BlockSpec((1, hd), lambda i: (0, 0))],
        out_specs=pl.BlockSpec((tm, w), lambda i: (i, 0)),
        out_shape=jax.ShapeDtypeStruct((m, w), out_dtype),
        compiler_params=_cparams("parallel"),
        name="head_rmsnorm",
    )(x, g.reshape(1, hd))


def _matmul_kernel(*refs, ksizes, has_res):
    n_a = len(ksizes)
    w_ref = refs[n_a]
    o_ref = refs[-1]
    acc = None
    off = 0
    for a_ref, ks in zip(refs[:n_a], ksizes):
        part = jnp.dot(a_ref[...], w_ref[off:off + ks, :], preferred_element_type=F32)
        acc = part if acc is None else acc + part
        off += ks
    if has_res:
        acc = acc + refs[n_a + 1][...]
    o_ref[...] = acc


def matmul(a_list, w, tm, tn, residual=None):
    m = a_list[0].shape[0]
    k, n = w.shape
    ksizes = tuple(a.shape[1] for a in a_list)
    assert sum(ksizes) == k and m % tm == 0 and n % tn == 0
    in_specs = [pl.BlockSpec((tm, ks), lambda i, j: (i, 0)) for ks in ksizes]
    in_specs.append(pl.BlockSpec((k, tn), lambda i, j: (0, j)))
    args = list(a_list) + [w]
    if residual is not None:
        in_specs.append(pl.BlockSpec((tm, tn), lambda i, j: (i, j)))
        args.append(residual)
    return pl.pallas_call(
        functools.partial(_matmul_kernel, ksizes=ksizes, has_res=residual is not None),
        grid=(m // tm, n // tn),
        in_specs=in_specs,
        out_specs=pl.BlockSpec((tm, tn), lambda i, j: (i, j)),
        out_shape=jax.ShapeDtypeStruct((m, n), F32),
        compiler_params=_cparams("parallel", "parallel"),
        name="matmul",
    )(*args)


def _matmul_nt_kernel(a_ref, w_ref, o_ref):
    o_ref[...] = _dot_nt(a_ref[...], w_ref[...])


def matmul_nt(a, w_t, tm, tn):
    m, k = a.shape
    n = w_t.shape[0]
    assert w_t.shape[1] == k and m % tm == 0 and n % tn == 0
    return pl.pallas_call(
        _matmul_nt_kernel,
        grid=(m // tm, n // tn),
        in_specs=[pl.BlockSpec((tm, k), lambda i, j: (i, 0)), pl.BlockSpec((tn, k), lambda i, j: (j, 0))],
        out_specs=pl.BlockSpec((tm, tn), lambda i, j: (i, j)),
        out_shape=jax.ShapeDtypeStruct((m, n), F32),
        compiler_params=_cparams("parallel", "parallel"),
        name="matmul_nt",
    )(a, w_t)


def _gates_kernel(p_ref, alog_ref, dtb_ref, fb_ref, o_ref):
    x = p_ref[...]
    lane = lax.broadcasted_iota(jnp.int32, x.shape, 1)
    beta = _sigmoid(x)
    g = -jnp.exp(alog_ref[...]) * _softplus(x + dtb_ref[...])
    logf = -_softplus(-(x + fb_ref[...]))
    out = jnp.where(lane < DN_HEADS, beta,
                    jnp.where(lane < 2 * DN_HEADS, g,
                              jnp.where(lane < 2 * DN_HEADS + FOX_HEADS, logf, 0.0)))
    o_ref[...] = out


def gates(p_small, a_log, dt_bias, f_bias, tm):
    m = p_small.shape[0]

    def pad(v, off):
        return jnp.zeros((1, SMALL_COLS), F32).at[0, off:off + v.shape[0]].set(v)

    row = pl.BlockSpec((1, SMALL_COLS), lambda i: (0, 0))
    return pl.pallas_call(
        _gates_kernel,
        grid=(m // tm,),
        in_specs=[pl.BlockSpec((tm, SMALL_COLS), lambda i: (i, 0)), row, row, row],
        out_specs=pl.BlockSpec((tm, SMALL_COLS), lambda i: (i, 0)),
        out_shape=jax.ShapeDtypeStruct((m, SMALL_COLS), F32),
        compiler_params=_cparams("parallel"),
        name="gates",
    )(p_small, pad(a_log, DN_HEADS), pad(dt_bias, DN_HEADS), pad(f_bias, 2 * DN_HEADS))


def _cumsum_kernel(x_ref, o_ref, *, blocks_per_group):
    x = x_ref[...]
    r, n = x.shape
    a = lax.broadcasted_iota(jnp.int32, (n, n), 0)
    b = lax.broadcasted_iota(jnp.int32, (n, n), 1)
    local = _dot_hi(x, (a <= b).astype(F32))
    tot = jnp.broadcast_to(local[:, n - 1:n], (r, n))
    ra = lax.broadcasted_iota(jnp.int32, (r, r), 0)
    rb = lax.broadcasted_iota(jnp.int32, (r, r), 1)
    earlier = jnp.logical_and(rb < ra, (ra // blocks_per_group) == (rb // blocks_per_group)).astype(F32)
    o_ref[...] = local + _dot_hi(earlier, tot)


def cumsum_time(x, blocks_per_group):
    return pl.pallas_call(
        functools.partial(_cumsum_kernel, blocks_per_group=blocks_per_group),
        out_shape=jax.ShapeDtypeStruct(x.shape, F32),
        compiler_params=pltpu.CompilerParams(vmem_limit_bytes=VMEM_LIMIT_BYTES),
        name="cumsum_time",
    )(x)


def _fox_prompt_kernel(q_ref, k_ref, v_ref, c_ref, o_ref, *, blk, hg, scale):
    i = pl.program_id(2)
    q0 = pl.multiple_of(i * blk, blk)
    hd = FOX_HD
    qs = [q_ref[:, h * hd:(h + 1) * hd] for h in range(hg)]
    cqs = [_row_to_col(c_ref[h, :, pl.ds(q0, blk)], blk) for h in range(hg)]

    def scores(h, k0):
        k = k_ref[pl.ds(k0, blk), h * hd:(h + 1) * hd]
        return _dot_nt(qs[h], k) * scale + (cqs[h] - c_ref[h, :, pl.ds(k0, blk)])

    def update(h, state, s, k0):
        m, l, acc = state
        m_new = jnp.maximum(m, jnp.max(s, axis=1, keepdims=True))
        alpha = jnp.exp(m - m_new)
        p = jnp.exp(s - m_new)
        l = l * alpha + jnp.sum(p, axis=1, keepdims=True)
        v = v_ref[pl.ds(k0, blk), h * hd:(h + 1) * hd]
        acc = acc * alpha + jnp.dot(p.astype(BF16), v, preferred_element_type=F32)
        return m_new, l, acc

    def body(j, carry):
        k0 = pl.multiple_of(j * blk, blk)
        ss = [scores(h, k0) for h in range(hg)]
        return tuple(update(h, carry[h], ss[h], k0) for h in range(hg))

    init = tuple((jnp.full((blk, 1), NEG_INF, F32), jnp.zeros((blk, 1), F32), jnp.zeros((blk, hd), F32))
                 for _ in range(hg))
    carry = lax.fori_loop(0, i, body, init)
    causal = (lax.broadcasted_iota(jnp.int32, (blk, blk), 1) <= lax.broadcasted_iota(jnp.int32, (blk, blk), 0))
    for h in range(hg):
        s = jnp.where(causal, scores(h, q0), NEG_INF)
        _, l, acc = update(h, carry[h], s, q0)
        o_ref[:, h * hd:(h + 1) * hd] = (acc / l).astype(o_ref.dtype)


def fox_prompt(qn, kn, v, c_row, batch, seq, blk, hg):
    nq = seq // blk
    ng = FOX_HEADS // hg
    w = hg * FOX_HD
    kv_spec = pl.BlockSpec((seq, w), lambda b, h, i: (b, h))
    return pl.pallas_call(
        functools.partial(_fox_prompt_kernel, blk=blk, hg=hg, scale=FOX_HD ** -0.5),
        grid=(batch, ng, nq),
        in_specs=[pl.BlockSpec((blk, w), lambda b, h, i: (b * nq + i, h)), kv_spec, kv_spec,
                  pl.BlockSpec((hg, 1, seq), lambda b, h, i: (b * ng + h, 0, 0))],
        out_specs=pl.BlockSpec((blk, w), lambda b, h, i: (b * nq + i, h)),
        out_shape=jax.ShapeDtypeStruct((batch * seq, FOX_W), BF16),
        compiler_params=_cparams("parallel", "parallel", "arbitrary"),
        name="fox_prompt",
    )(qn, kn, v, c_row)


def _dn_prompt_kernel(q_ref, k_ref, v_ref, z_ref, wq_ref, wk_ref, wv_ref, g_ref, b_ref, ng_ref,
                      o_ref, s_out_ref, s_ref, xq_ref, xk_ref, xv_ref, *, tc, chunk, hg):
    t = pl.program_id(2)
    nc = tc // chunk
    pad = 8

    @pl.when(t == 0)
    def _():
        s_ref[...] = jnp.zeros_like(s_ref)
        for buf in (xq_ref, xk_ref, xv_ref):
            buf[0:pad, :] = jnp.zeros((pad, hg * DN_D), F32)

    def conv_silu(x_ref, w_ref, buf_ref):
        u = x_ref[...]
        buf_ref[pad:pad + tc, :] = u
        w = w_ref[...]
        out = buf_ref[pad - 3:pad - 3 + tc, :] * w[0:1, :]
        out = out + buf_ref[pad - 2:pad - 2 + tc, :] * w[1:2, :]
        out = out + buf_ref[pad - 1:pad - 1 + tc, :] * w[2:3, :]
        out = out + u * w[3:4, :]
        buf_ref[0:pad, :] = u[tc - pad:tc, :]
        return out * _sigmoid(out)

    def l2norm(x):
        return x * lax.rsqrt(jnp.sum(x * x, axis=-1, keepdims=True) + EPS)

    qc = conv_silu(q_ref, wq_ref, xq_ref)
    kc = conv_silu(k_ref, wk_ref, xk_ref)
    vc = conv_silu(v_ref, wv_ref, xv_ref)

    ra = lax.broadcasted_iota(jnp.int32, (chunk, chunk), 0)
    rb = lax.broadcasted_iota(jnp.int32, (chunk, chunk), 1)
    tri = ra >= rb
    strict = ra > rb
    eye_f = (ra == rb).astype(F32)
    upper_f = (ra <= rb).astype(F32)

    heads = []
    for hh in range(hg):
        cs = slice(hh * DN_D, (hh + 1) * DN_D)
        heads.append(dict(
            q=l2norm(qc[:, cs]) * (DN_D ** -0.5), k=l2norm(kc[:, cs]), v=vc[:, cs],
            gc=_dot_hi(g_ref[hh], upper_f),
            beta=b_ref[hh], s=s_ref[hh], outs=[]))

    work = []
    for c in range(nc):
        sl = slice(c * chunk, (c + 1) * chunk)
        for hd in heads:
            q, k, v = hd['q'][sl], hd['k'][sl], hd['v'][sl]
            gc_row = hd['gc'][c:c + 1, :]
            gc = _row_to_col(gc_row, chunk)
            beta = _row_to_col(hd['beta'][c:c + 1, :], chunk)
            decay = jnp.where(tri, jnp.exp(jnp.where(tri, gc - gc_row, 0.0)), 0.0)
            qk_b = jnp.concatenate([q, k], axis=0).astype(BF16)
            work.append(dict(hd=hd, q=q, k=k, v=v, gc=gc, beta=beta, decay=decay, qk_b=qk_b,
                             gc_last=gc_row[:, chunk - 1:chunk]))
    for wk in work:
        wk['gram'] = _dot_nt(wk['qk_b'], wk['qk_b'][chunk:, :])
    for wk in work:
        lower = jnp.where(strict, wk['beta'] * wk['gram'][chunk:, :] * wk['decay'], 0.0)
        wk['inv'] = eye_f - lower
        wk['power'] = _split_bf16(lower)
    for wk in work:
        wk['power'] = _split_bf16(_dot_split(wk['power'], wk['power']))
    span = 2
    while span < chunk:
        for wk in work:
            wk['inv'] = wk['inv'] + _dot_split(_split_bf16(wk['inv']), wk['power'])
        span *= 2
        if span < chunk:
            for wk in work:
                wk['power'] = _split_bf16(_dot_split(wk['power'], wk['power']))
    for wk in work:
        egc = jnp.exp(wk['gc'])
        k, beta = wk['k'], wk['beta']
        uw = _dot_split(_split_bf16(wk['inv']),
                        _split_bf16(jnp.concatenate([wk['v'] * beta, k * (beta * egc)], axis=1)))
        wk['u'] = uw[:, :DN_D]
        wk['lhs1'] = jnp.concatenate([uw[:, DN_D:], wk['q'] * egc], axis=0).astype(BF16)
        qk = jnp.where(tri, wk['gram'][:chunk, :] * wk['decay'], 0.0)
        k_dec = k * jnp.exp(wk['gc_last'] - wk['gc'])
        wk['lhs2'] = jnp.concatenate([qk, k_dec.T], axis=0).astype(BF16)
        wk['g_last'] = jnp.exp(wk['gc_last'])
    for wk in work:
        hd = wk['hd']
        s = hd['s']
        ws = jnp.dot(wk['lhs1'], s.astype(BF16), preferred_element_type=F32)
        v_new = wk['u'] - ws[:chunk, :]
        upd = jnp.dot(wk['lhs2'], v_new.astype(BF16), preferred_element_type=F32)
        hd['outs'].append(ws[chunk:, :] + upd[:chunk, :])
        hd['s'] = s * wk['g_last'] + upd[chunk:, :]

    ng = ng_ref[...]
    for hh, hd in enumerate(heads):
        cs = slice(hh * DN_D, (hh + 1) * DN_D)
        s_ref[hh] = hd['s']
        s_out_ref[hh] = hd['s']
        o = jnp.concatenate(hd['outs'], axis=0)
        o = o * lax.rsqrt(jnp.mean(o * o, axis=-1, keepdims=True) + EPS) * ng
        z = z_ref[:, cs]
        o_ref[:, cs] = (o * (z * _sigmoid(z))).astype(o_ref.dtype)


def dn_prompt(p, conv_w, g_chunks, beta_chunks, norm_g, batch, seq, tc, hg):
    nt = seq // tc
    nc = tc // DN_CHUNK
    ng = DN_HEADS // hg
    w = hg * DN_D

    def col(off):
        return pl.BlockSpec((tc, w), lambda b, h, t: (b * nt + t, off * ng + h))

    def wcol(off):
        return pl.BlockSpec((DN_CONV, w), lambda b, h, t: (0, off * ng + h))

    gspec = pl.BlockSpec((hg, nc, DN_CHUNK), lambda b, h, t: (b * ng + h, t, 0))
    return pl.pallas_call(
        functools.partial(_dn_prompt_kernel, tc=tc, chunk=DN_CHUNK, hg=hg),
        grid=(batch, ng, nt),
        in_specs=[col(0), col(1), col(2), col(3), wcol(0), wcol(1), wcol(2), gspec, gspec,
                  pl.BlockSpec((1, DN_D), lambda b, h, t: (0, 0))],
        out_specs=[pl.BlockSpec((tc, w), lambda b, h, t: (b * nt + t, h)),
                   pl.BlockSpec((hg, DN_D, DN_D), lambda b, h, t: (b * ng + h, 0, 0))],
        out_shape=[jax.ShapeDtypeStruct((batch * seq, DN_QK), BF16),
                   jax.ShapeDtypeStruct((batch * DN_HEADS, DN_D, DN_D), F32)],
        scratch_shapes=[pltpu.VMEM((hg, DN_D, DN_D), F32)] + [pltpu.VMEM((tc + 8, w), F32)] * 3,
        compiler_params=_cparams("parallel", "parallel", "arbitrary"),
        name="dn_prompt",
    )(p, p, p, p, conv_w, conv_w, conv_w, g_chunks, beta_chunks, norm_g.reshape(1, DN_D))


def _dn_sample_kernel(q_ref, k_ref, v_ref, z_ref, cq_ref, ck_ref, cv_ref, wq_ref, wk_ref, wv_ref,
                      g_ref, b_ref, ng_ref, s_ref, o_ref, s_out_ref, *, bb):
    def conv_silu(x_ref, c_ref, w_ref):
        w = w_ref[...]
        out = c_ref[:, 0, :] * w[0:1, :]
        out = out + c_ref[:, 1, :] * w[1:2, :]
        out = out + c_ref[:, 2, :] * w[2:3, :]
        out = out + x_ref[...] * w[3:4, :]
        return out * _sigmoid(out)

    def l2norm(x):
        return x * lax.rsqrt(jnp.sum(x * x, axis=-1, keepdims=True) + EPS)

    q = l2norm(conv_silu(q_ref, cq_ref, wq_ref)) * (DN_D ** -0.5)
    k = l2norm(conv_silu(k_ref, ck_ref, wk_ref))
    v = conv_silu(v_ref, cv_ref, wv_ref)
    q_t = q.T
    k_t = k.T
    decay = jnp.exp(g_ref[0])
    beta = b_ref[0]
    rows = []
    for b in range(bb):
        s = s_ref[b, 0] * decay[b:b + 1, :]
        k_col = k_t[:, b:b + 1]
        kv = jnp.sum(k_col * s, axis=0, keepdims=True)
        s = s + k_col * ((v[b:b + 1, :] - kv) * beta[b:b + 1, :])
        s_out_ref[b, 0] = s
        rows.append(jnp.sum(q_t[:, b:b + 1] * s, axis=0, keepdims=True))
    o = jnp.concatenate(rows, axis=0)
    o = o * lax.rsqrt(jnp.mean(o * o, axis=-1, keepdims=True) + EPS) * ng_ref[...]
    z = z_ref[...]
    o_ref[...] = (o * (z * _sigmoid(z))).astype(o_ref.dtype)


def dn_sample(p, state_conv, conv_w, g_t, beta_t, norm_g, state, bb):
    nb = p.shape[0]
    hb = DN_HEADS

    def col(off):
        return pl.BlockSpec((bb, DN_D), lambda i, h: (i, off + h))

    def ccol(off):
        return pl.BlockSpec((bb, DN_CONV - 1, DN_D), lambda i, h: (i, 0, off + h))

    def wcol(off):
        return pl.BlockSpec((DN_CONV, DN_D), lambda i, h: (0, off + h))

    gspec = pl.BlockSpec((1, bb, 1), lambda i, h: (h, i, 0))
    sspec = pl.BlockSpec((bb, 1, DN_D, DN_D), lambda i, h: (i, h, 0, 0))
    return pl.pallas_call(
        functools.partial(_dn_sample_kernel, bb=bb),
        grid=(nb // bb, DN_HEADS),
        in_specs=[col(0), col(hb), col(2 * hb), col(3 * hb), ccol(0), ccol(hb), ccol(2 * hb),
                  wcol(0), wcol(hb), wcol(2 * hb), gspec, gspec,
                  pl.BlockSpec((1, DN_D), lambda i, h: (0, 0)), sspec],
        out_specs=[pl.BlockSpec((bb, DN_D), lambda i, h: (i, h)), sspec],
        out_shape=[jax.ShapeDtypeStruct((nb, DN_QK), BF16), jax.ShapeDtypeStruct(state.shape, F32)],
        compiler_params=_cparams("parallel", "parallel"),
        name="dn_sample",
    )(p, p, p, p, state_conv, state_conv, state_conv, conv_w, conv_w, conv_w, g_t, beta_t,
      norm_g.reshape(1, DN_D), state)


def _fox_sample_kernel(pt_ref, q_ref, kn_ref, vn_ref, lfn_ref, *refs, pp, scale):
    k_refs = refs[0:pp]
    v_refs = refs[pp:2 * pp]
    lf_refs = refs[2 * pp:3 * pp]
    o_ref = refs[3 * pp]
    m_ref, l_ref, acc_ref, carry_ref = refs[3 * pp + 1:]
    j = pl.program_id(1)
    qs = q_ref[0] * scale

    @pl.when(j == 0)
    def _():
        m_ref[...] = jnp.broadcast_to(jnp.sum(qs * kn_ref[0], axis=1, keepdims=True), m_ref.shape)
        l_ref[...] = jnp.ones_like(l_ref)
        acc_ref[...] = vn_ref[0]
        carry_ref[...] = lfn_ref[0]

    ra = lax.broadcasted_iota(jnp.int32, (PAGE_SIZE, PAGE_SIZE), 0)
    rb = lax.broadcasted_iota(jnp.int32, (PAGE_SIZE, PAGE_SIZE), 1)
    later_f = (ra > rb).astype(F32)
    shape3 = (PAGE_SIZE, FOX_HEADS, FOX_HD)
    diag3 = lax.broadcasted_iota(jnp.int32, shape3, 0) == lax.broadcasted_iota(jnp.int32, shape3, 2)
    carry = carry_ref[...]
    logits = []
    for i in range(pp):
        lf = lf_refs[i][0]
        bias = _dot_hi(lf, later_f) + carry
        carry = carry + jnp.sum(lf, axis=1, keepdims=True)
        s3 = jnp.sum(k_refs[i][0] * qs[None] + jnp.where(diag3, bias[None], 0.0), axis=2, keepdims=True)
        logits.append(jnp.broadcast_to(s3, shape3))
    carry_ref[...] = carry
    m_old = m_ref[...]
    m_new = m_old
    for s3 in logits:
        m_new = jnp.maximum(m_new, jnp.max(s3, axis=0))
    alpha = jnp.exp(m_old - m_new)
    l_new = l_ref[...] * alpha
    acc = acc_ref[...] * alpha
    for i, s3 in enumerate(logits):
        p3 = jnp.exp(s3 - m_new[None])
        l_new = l_new + jnp.sum(p3, axis=0)
        acc = acc + jnp.sum(p3 * v_refs[i][0], axis=0)
    l_ref[...] = l_new
    acc_ref[...] = acc
    m_ref[...] = m_new

    @pl.when(j == pl.num_programs(1) - 1)
    def _():
        o_ref[0] = (acc_ref[...] / l_ref[...]).astype(o_ref.dtype)


def fox_sample(q, k_new, v_new, lf_new, k_pool, v_pool, lf_pool_t, page_table, pp):
    nb, n_pages = page_table.shape
    assert n_pages % pp == 0
    steps = n_pages // pp

    def page_idx4(i):
        return lambda b, j, pt: (pt[b, n_pages - 1 - (j * pp + i)], 0, 0, 0)

    def page_idx3(i):
        return lambda b, j, pt: (pt[b, n_pages - 1 - (j * pp + i)], 0, 0)

    tok = lambda b, j, pt: (b, 0, 0)
    in_specs = [pl.BlockSpec((1, FOX_HEADS, FOX_HD), tok)] * 3 + [pl.BlockSpec((1, FOX_HEADS, 1), tok)]
    in_specs += [pl.BlockSpec((1, PAGE_SIZE, FOX_HEADS, FOX_HD), page_idx4(i)) for i in range(pp)]
    in_specs += [pl.BlockSpec((1, PAGE_SIZE, FOX_HEADS, FOX_HD), page_idx4(i)) for i in range(pp)]
    in_specs += [pl.BlockSpec((1, FOX_HEADS, PAGE_SIZE), page_idx3(i)) for i in range(pp)]
    grid_spec = pltpu.PrefetchScalarGridSpec(
        num_scalar_prefetch=1,
        grid=(nb, steps),
        in_specs=in_specs,
        out_specs=pl.BlockSpec((1, FOX_HEADS, FOX_HD), tok),
        scratch_shapes=[pltpu.VMEM((FOX_HEADS, FOX_HD), F32), pltpu.VMEM((FOX_HEADS, FOX_HD), F32),
                        pltpu.VMEM((FOX_HEADS, FOX_HD), F32), pltpu.VMEM((FOX_HEADS, 1), F32)],
    )
    return pl.pallas_call(
        functools.partial(_fox_sample_kernel, pp=pp, scale=FOX_HD ** -0.5),
        grid_spec=grid_spec,
        out_shape=jax.ShapeDtypeStruct((nb, FOX_HEADS, FOX_HD), BF16),
        compiler_params=_cparams("parallel", "arbitrary"),
        name="fox_sample",
    )(page_table, q, k_new, v_new, lf_new, *([k_pool] * pp), *([v_pool] * pp), *([lf_pool_t] * pp))


def _mem_prompt_kernel(q_ref, k_ref, v_ref, g_ref, o_ref, *, scale):
    g = g_ref[...]
    for h in range(MEM_HEADS):
        cs = slice(h * MEM_HD, (h + 1) * MEM_HD)
        q = q_ref[:, cs]
        qn = (q * lax.rsqrt(jnp.mean(q * q, axis=-1, keepdims=True) + EPS) * g).astype(BF16)
        s = _dot_nt(qn, k_ref[:, cs]) * scale
        p = jnp.exp(s - jnp.max(s, axis=1, keepdims=True))
        p = p / jnp.sum(p, axis=1, keepdims=True)
        o_ref[:, cs] = jnp.dot(p.astype(BF16), v_ref[:, cs], preferred_element_type=F32).astype(o_ref.dtype)


def mem_attend_prompt(p, mk, mv, qn_g, batch, seq, tq):
    nq = seq // tq
    kv = pl.BlockSpec((MEM_TOKENS, MEM_W), lambda b, i: (b, 0))
    return pl.pallas_call(
        functools.partial(_mem_prompt_kernel, scale=MEM_HD ** -0.5),
        grid=(batch, nq),
        in_specs=[pl.BlockSpec((tq, MEM_W), lambda b, i: (b * nq + i, 0)), kv, kv,
                  pl.BlockSpec((1, MEM_HD), lambda b, i: (0, 0))],
        out_specs=pl.BlockSpec((tq, MEM_W), lambda b, i: (b * nq + i, 0)),
        out_shape=jax.ShapeDtypeStruct((batch * seq, MEM_W), BF16),
        compiler_params=_cparams("parallel", "parallel"),
        name="mem_prompt",
    )(p, mk, mv, qn_g.reshape(1, MEM_HD))


def _mem_sample_kernel(q_ref, k_ref, v_ref, g_ref, o_ref, *, scale):
    q = q_ref[0]
    qn = q * lax.rsqrt(jnp.mean(q * q, axis=-1, keepdims=True) + EPS) * g_ref[...] * scale
    s3 = jnp.sum(k_ref[0] * qn[None], axis=2, keepdims=True)
    p3 = jnp.exp(s3 - jnp.max(s3, axis=0)[None])
    o = jnp.sum(p3 * v_ref[0], axis=0) / jnp.sum(p3, axis=0)
    o_ref[0] = o.astype(o_ref.dtype)


def mem_sample(q, mk, mv, qn_g):
    nb = q.shape[0]
    tok = pl.BlockSpec((1, MEM_HEADS, MEM_HD), lambda b: (b, 0, 0))
    kv = pl.BlockSpec((1, MEM_TOKENS, MEM_HEADS, MEM_HD), lambda b: (b, 0, 0, 0))
    return pl.pallas_call(
        functools.partial(_mem_sample_kernel, scale=MEM_HD ** -0.5),
        grid=(nb,),
        in_specs=[tok, kv, kv, pl.BlockSpec((1, MEM_HD), lambda b: (0, 0))],
        out_specs=tok,
        out_shape=jax.ShapeDtypeStruct((nb, MEM_HEADS, MEM_HD), BF16),
        compiler_params=_cparams("parallel"),
        name="mem_sample",
    )(q, mk, mv, qn_g.reshape(1, MEM_HD))


PEER_NO_RANK = float(PEER_NKEYS)


def _top_values(s, count, with_rank=False):
    rows = s.shape[0]
    idx = lax.broadcasted_iota(jnp.int32, s.shape, 0).astype(F32)
    rank = jnp.full(s.shape, PEER_NO_RANK, F32) if with_rank else None
    vals = []
    for r in range(count):
        m = jnp.max(s, axis=0, keepdims=True)
        picked = idx == jnp.min(jnp.where(s == m, idx, float(rows)), axis=0, keepdims=True)
        s = jnp.where(picked, NEG_INF, s)
        if with_rank:
            rank = jnp.where(picked, float(r), rank)
        vals.append(m)
    return jnp.concatenate(vals, axis=0), rank


def _peer_stats_kernel(hn_ref, wq_ref, keys_ref, cnt_ref, g1_ref, r2_ref, e2_ref, st_ref, top_ref, rank_ref,
                       cnt_scr, g1_scr):
    nk = PEER_NKEYS
    kk = PEER_TOPK
    q = jnp.dot(hn_ref[...], wq_ref[...], preferred_element_type=F32)
    st_ref[...] = _dot_nt(keys_ref[...], q.astype(BF16))

    def half_body(c, carry):
        r0 = pl.multiple_of(c * nk, nk)
        t0 = pl.multiple_of(c * kk, kk)
        vals, rank = _top_values(st_ref[pl.ds(r0, nk), :], kk, with_rank=True)
        top_ref[pl.ds(t0, kk), :] = vals
        rank_ref[pl.ds(r0, nk), :] = rank
        return carry

    lax.fori_loop(0, 2 * PEER_HEADS, half_body, 0)

    def head_body(h, carry):
        t0 = pl.multiple_of(h * 2 * kk, 2 * kk)
        v1 = top_ref[pl.ds(t0, kk), :]
        v2 = top_ref[pl.ds(t0 + kk, kk), :]
        cand = jnp.concatenate([v1[0:1, :] + v2] + [v1[a:a + 1, :] + v2[0:8, :] for a in range(1, 8)]
                               + [v1[8:kk, :] + v2[0:1, :]], axis=0)
        cv, _ = _top_values(cand, kk)
        z = jnp.sum(jnp.exp(cv - cv[0:1, :]), axis=0, keepdims=True)
        tau = cv[kk - 1:kk, :]
        r0 = pl.multiple_of(h * 2 * nk, 2 * nk)
        s1 = st_ref[pl.ds(r0, nk), :]
        s2 = st_ref[pl.ds(r0 + nk, nk), :]
        rank1 = rank_ref[pl.ds(r0, nk), :]
        cnt = jnp.zeros_like(s1)
        for a in range(kk):
            n_hit = jnp.sum(((v1[a:a + 1, :] + v2) >= tau).astype(F32), axis=0, keepdims=True)
            cnt = jnp.where(rank1 == float(a), n_hit, cnt)
        cnt_scr[h] = cnt
        g1_scr[h] = jnp.exp(s1 - v1[0:1, :]) / z
        r2_ref[h] = rank_ref[pl.ds(r0 + nk, nk), :].astype(BF16)
        e2_ref[h] = jnp.exp(s2 - v2[0:1, :]).astype(BF16)
        return carry

    lax.fori_loop(0, PEER_HEADS, head_body, 0)
    for h in range(PEER_HEADS):
        cnt_ref[:, h, :] = cnt_scr[h]
        g1_ref[:, h, :] = g1_scr[h]


def peer_stats(hn, w_q, keys_t, mt):
    m = hn.shape[0]
    nrow = PEER_HEADS * 2 * PEER_NKEYS
    hspec = pl.BlockSpec((PEER_HEADS, PEER_NKEYS, mt), lambda i: (0, 0, i))
    kspec = pl.BlockSpec((PEER_NKEYS, PEER_HEADS, mt), lambda i: (0, 0, i))
    f32_shape = jax.ShapeDtypeStruct((PEER_NKEYS, PEER_HEADS, m), F32)
    bf16_shape = jax.ShapeDtypeStruct((PEER_HEADS, PEER_NKEYS, m), BF16)
    return pl.pallas_call(
        _peer_stats_kernel,
        grid=(m // mt,),
        in_specs=[pl.BlockSpec((mt, D_MODEL), lambda i: (i, 0)),
                  pl.BlockSpec(w_q.shape, lambda i: (0, 0)),
                  pl.BlockSpec(keys_t.shape, lambda i: (0, 0))],
        out_specs=[kspec, kspec, hspec, hspec],
        out_shape=[f32_shape, f32_shape, bf16_shape, bf16_shape],
        scratch_shapes=[pltpu.VMEM((nrow, mt), F32), pltpu.VMEM((2 * PEER_HEADS * PEER_TOPK, mt), F32),
                        pltpu.VMEM((nrow, mt), F32), pltpu.VMEM((PEER_HEADS, PEER_NKEYS, mt), F32),
                        pltpu.VMEM((PEER_HEADS, PEER_NKEYS, mt), F32)],
        compiler_params=_cparams("parallel"),
        name="peer_stats",
    )(hn, w_q, keys_t)


def _peer_gates(cnt_ref, g1_ref, r2_ref, e2_ref, gate_ref, n_a):
    mt = gate_ref.shape[1]
    rows = 16
    heads = range(PEER_HEADS)
    for a in range(n_a):
        for c in range(mt // LANE):
            cs = slice(c * LANE, (c + 1) * LANE)
            cnt = [jnp.broadcast_to(cnt_ref[a, h:h + 1, cs], (rows, LANE)).astype(BF16) for h in heads]
            g1 = [jnp.broadcast_to(g1_ref[a, h:h + 1, cs], (rows, LANE)).astype(BF16) for h in heads]
            for r in range(PEER_NKEYS // rows):
                rs = slice(r * rows, (r + 1) * rows)
                w = None
                for h in heads:
                    term = jnp.where(r2_ref[h, rs, cs] < cnt[h], e2_ref[h, rs, cs], 0.0) * g1[h]
                    w = term if w is None else w + term
                gate_ref[a * PEER_NKEYS + r * rows:a * PEER_NKEYS + (r + 1) * rows, cs] = w


def _peer_mix_kernel(hn_ref, res_ref, u_ref, v_ref, cnt_ref, g1_ref, cntn_ref, g1n_ref, r2_ref, e2_ref,
                     o_ref, gate_a_ref, gate_b_ref, *, et):
    e = pl.program_id(1)
    n_a = et // PEER_NKEYS

    @pl.when(e == 0)
    def _():
        o_ref[...] = res_ref[...]
        _peer_gates(cnt_ref, g1_ref, r2_ref, e2_ref, gate_a_ref, n_a)

    def step(gate_cur_ref, gate_next_ref):
        _peer_gates(cntn_ref, g1n_ref, r2_ref, e2_ref, gate_next_ref, n_a)
        act_t = _dot_nt(u_ref[...], hn_ref[...])
        gel_t = 0.5 * act_t * (1.0 + lax.erf(act_t * (0.5 ** 0.5)))
        hmat = (gate_cur_ref[...].astype(F32) * gel_t).astype(BF16).T
        o_ref[...] += jnp.dot(hmat, v_ref[...], preferred_element_type=F32)

    @pl.when(e % 2 == 0)
    def _():
        step(gate_a_ref, gate_b_ref)

    @pl.when(e % 2 == 1)
    def _():
        step(gate_b_ref, gate_a_ref)


def peer_mix(hn, res, u_tab, v_tab, cnt_t, g1_t, r2, e2, mt, et):
    m = hn.shape[0]
    n_exp = u_tab.shape[0]
    n_e = n_exp // et
    a_per = et // PEER_NKEYS
    once = pl.Buffered(1)
    sel1 = pl.BlockSpec((a_per, PEER_HEADS, mt), lambda i, e: (e, 0, i))
    sel1_next = pl.BlockSpec((a_per, PEER_HEADS, mt), lambda i, e: (jnp.minimum(e + 1, n_e - 1), 0, i))
    sel2 = pl.BlockSpec((PEER_HEADS, PEER_NKEYS, mt), lambda i, e: (0, 0, i), pipeline_mode=once)
    tab = pl.BlockSpec((et, D_MODEL), lambda i, e: (e, 0))
    return pl.pallas_call(
        functools.partial(_peer_mix_kernel, et=et),
        grid=(m // mt, n_e),
        in_specs=[pl.BlockSpec((mt, D_MODEL), lambda i, e: (i, 0), pipeline_mode=once),
                  pl.BlockSpec((mt, D_MODEL), lambda i, e: (i, 0), pipeline_mode=once), tab, tab,
                  sel1, sel1, sel1_next, sel1_next, sel2, sel2],
        out_specs=pl.BlockSpec((mt, D_MODEL), lambda i, e: (i, 0)),
        out_shape=jax.ShapeDtypeStruct((m, D_MODEL), F32),
        scratch_shapes=[pltpu.VMEM((et, mt), BF16), pltpu.VMEM((et, mt), BF16)],
        compiler_params=_cparams("parallel", "arbitrary"),
        name="peer_mix",
    )(hn, res, u_tab, v_tab, cnt_t, g1_t, cnt_t, g1_t, r2, e2)


def _prep_weights(w_in3, layer, w_mem_kv, w_out, peer_w_q, peer_sub_keys, peer_u, peer_v):
    w_t = jnp.swapaxes(w_in3[layer], 0, 1)
    w_dn = w_t[:_OFF_B].astype(BF16)
    w_fox = w_t[_OFF_FQ:_OFF_FF].astype(BF16)
    w_mq = w_t[_OFF_MQ:].astype(BF16)
    w_small = jnp.concatenate(
        [w_t[_OFF_B:_OFF_FQ], w_t[_OFF_FF:_OFF_MQ],
         jnp.zeros((SMALL_COLS - 2 * DN_HEADS - FOX_HEADS, D_MODEL), F32)], axis=0).astype(BF16)
    half = PEER_DKEY // 2
    pairs = 2 * PEER_HEADS
    sk = peer_sub_keys.reshape(pairs, PEER_NKEYS, half)
    eye = jnp.eye(pairs, dtype=F32)
    keys_t = (eye[:, None, :, None] * sk[:, :, None, :]).reshape(pairs * PEER_NKEYS, pairs * half).astype(BF16)
    return dict(w_dn=w_dn, w_fox=w_fox, w_mq=w_mq, w_small=w_small, w_mem_kv=w_mem_kv.astype(BF16), w_out=w_out.astype(BF16),
                peer_w_q=peer_w_q.astype(BF16), keys_t=keys_t, peer_u=peer_u.astype(BF16),
                peer_v=peer_v.astype(BF16))


def _tile(m, pref):
    return pref if m % pref == 0 else m


def _project(x2, ln_g, wts):
    m = x2.shape[0]
    xn = rmsnorm_rows(x2, ln_g, _tile(m, 256))
    tm = _tile(m, 1024)
    p_dn = matmul_nt(xn, wts['w_dn'], tm, 512)
    p_fox = matmul_nt(xn, wts['w_fox'], tm, 512)
    p_mq = matmul_nt(xn, wts['w_mq'], tm, 512)
    ps = matmul_nt(xn, wts['w_small'], tm, SMALL_COLS)
    return p_dn, p_fox, p_mq, ps


def _channel_mix(x2, o_dn, o_fox, o_mem, wts, ln_ffn_g):
    m = x2.shape[0]
    h = matmul([o_dn, o_fox, o_mem], wts['w_out'], _tile(m, 1024), 512, residual=x2)
    hn = rmsnorm_rows(h, ln_ffn_g, _tile(m, 256))
    cnt_t, g1_t, r2, e2 = peer_stats(hn, wts['peer_w_q'], wts['keys_t'], _tile(m, 512))
    return peer_mix(hn, h, wts['peer_u'], wts['peer_v'], cnt_t, g1_t, r2, e2, _tile(m, 512), 512)


def kernel(x_prompt, x_sample, cache_fox_k, cache_fox_v, cache_fox_logf, state_delta, state_conv, cache_mem_k, cache_mem_v, page_table, mem_prompt, ln_mix_g, w_in, conv_w, dn_a_log, dn_dt_bias, dn_norm_g, fox_f_bias, fox_qn_g, fox_kn_g, ln_mem_g, w_mem_kv, mem_qn_g, mem_kn_g, w_out, ln_ffn_g, peer_w_q, peer_sub_keys, peer_u, peer_v):
    depth = w_in.shape[0]
    assert depth == 1
    l = 0
    batch, seq, _ = x_prompt.shape
    nb = x_sample.shape[0]
    m_p = batch * seq
    wts = _prep_weights(w_in, l, w_mem_kv[l], w_out[l], peer_w_q[l], peer_sub_keys[l], peer_u[l], peer_v[l])

    x2 = x_prompt.reshape(m_p, D_MODEL)
    p, p_fox, p_mq, ps = _project(x2, ln_mix_g[l], wts)
    gt = gates(ps, dn_a_log[l], dn_dt_bias[l], fox_f_bias[l], 1024)
    beta = gt[:, :DN_HEADS]
    gdec = gt[:, DN_HEADS:2 * DN_HEADS]
    logf = gt[:, 2 * DN_HEADS:2 * DN_HEADS + FOX_HEADS]

    def to_chunks(a):
        return a.reshape(batch, seq, DN_HEADS).transpose(0, 2, 1).reshape(batch * DN_HEADS, seq // DN_CHUNK, DN_CHUNK)

    o_dn, dn_state = dn_prompt(p, conv_w[l], to_chunks(gdec), to_chunks(beta), dn_norm_g[l], batch, seq, 512, 8)

    fqn = head_rmsnorm(p_fox, 0, fox_qn_g[l], FOX_HEADS, FOX_HD, 512, out_dtype=BF16)
    fkn = head_rmsnorm(p_fox, 1, fox_kn_g[l], FOX_HEADS, FOX_HD, 512)
    fv = p_fox[:, 2 * FOX_W:]
    lf_rows = logf.reshape(batch, seq, FOX_HEADS).transpose(0, 2, 1).reshape(batch * FOX_HEADS * (seq // LANE), LANE)
    c_row = cumsum_time(lf_rows, seq // LANE).reshape(batch * FOX_HEADS, 1, seq)
    o_fox = fox_prompt(fqn, fkn.astype(BF16), fv.astype(BF16), c_row, batch, seq, 512, 2)

    mem2 = mem_prompt.reshape(batch * MEM_TOKENS, D_MODEL)
    memn = rmsnorm_rows(mem2, ln_mem_g[l], 256)
    mkv = matmul([memn], wts['w_mem_kv'], batch * MEM_TOKENS, 512)
    mk = head_rmsnorm(mkv, 0, mem_kn_g[l], MEM_HEADS, MEM_HD, 256)
    mv = mkv[:, MEM_W:]
    o_mem = mem_attend_prompt(p_mq, mk.astype(BF16), mv.astype(BF16), mem_qn_g[l], batch, seq, 512)

    y_p = _channel_mix(x2, o_dn, o_fox, o_mem, wts, ln_ffn_g[l])

    xs = x_sample.reshape(nb, D_MODEL)
    sp, sp_fox, sp_mq, sps = _project(xs, ln_mix_g[l], wts)
    sgt = gates(sps, dn_a_log[l], dn_dt_bias[l], fox_f_bias[l], nb)
    s_beta = sgt[:, :DN_HEADS].T.reshape(DN_HEADS, nb, 1)
    s_g = sgt[:, DN_HEADS:2 * DN_HEADS].T.reshape(DN_HEADS, nb, 1)
    s_logf = sgt[:, 2 * DN_HEADS:2 * DN_HEADS + FOX_HEADS]
    so_dn, s_state = dn_sample(sp, state_conv[l], conv_w[l], s_g, s_beta, dn_norm_g[l], state_delta[l], 32)
    conv_s = jnp.concatenate([state_conv[l][:, 1:, :], sp[:, None, :CONV_CH]], axis=1)

    sfq = head_rmsnorm(sp_fox, 0, fox_qn_g[l], FOX_HEADS, FOX_HD, nb)
    sfk = head_rmsnorm(sp_fox, 1, fox_kn_g[l], FOX_HEADS, FOX_HD, nb)
    sfv = sp_fox[:, 2 * FOX_W:]
    so_fox = fox_sample(sfq.reshape(nb, FOX_HEADS, FOX_HD), sfk.reshape(nb, FOX_HEADS, FOX_HD),
                        sfv.reshape(nb, FOX_HEADS, FOX_HD), s_logf.reshape(nb, FOX_HEADS, 1),
                        cache_fox_k[l], cache_fox_v[l], jnp.swapaxes(cache_fox_logf[l], 1, 2),
                        page_table, 8).reshape(nb, FOX_W)
    smq = sp_mq.reshape(nb, MEM_HEADS, MEM_HD)
    so_mem = mem_sample(smq, cache_mem_k[l], cache_mem_v[l], mem_qn_g[l]).reshape(nb, MEM_W)
    y_s = _channel_mix(xs, so_dn, so_fox, so_mem, wts, ln_ffn_g[l])

    return (
        y_p.reshape(batch, seq, D_MODEL),
        y_s.reshape(nb, 1, D_MODEL),
        fkn.reshape(1, batch, seq, FOX_HEADS, FOX_HD),
        fv.reshape(1, batch, seq, FOX_HEADS, FOX_HD),
        logf.reshape(1, batch, seq, FOX_HEADS),
        dn_state.reshape(1, batch, DN_HEADS, DN_D, DN_D),
        p.reshape(batch, seq, P_DN_COLS)[:, seq - (DN_CONV - 1):, :CONV_CH][None],
        mk.reshape(1, batch, MEM_TOKENS, MEM_HEADS, MEM_HD),
        mv.reshape(1, batch, MEM_TOKENS, MEM_HEADS, MEM_HD),
        sfk.reshape(1, nb, 1, FOX_HEADS, FOX_HD),
        sfv.reshape(1, nb, 1, FOX_HEADS, FOX_HD),
        s_logf.reshape(1, nb, 1, FOX_HEADS),
        s_state[None],
        conv_s[None],
    )
```

```python
import functools

import jax
import jax.numpy as jnp
from jax import lax
from jax.experimental import pallas as pl
from jax.experimental.pallas import tpu as pltpu

F32 = jnp.float32
BF16 = jnp.bfloat16
HIGHEST = lax.Precision.HIGHEST
EPS = 1e-6
NEG_INF = float("-inf")

D_MODEL = 4096
DN_HEADS = 16
DN_D = 128
DN_CONV = 4
DN_CHUNK = 64
FOX_HEADS = 8
FOX_HD = 128
MEM_TOKENS = 256
MEM_HEADS = 4
MEM_HD = 256
PEER_HEADS = 8
PEER_NKEYS = 128
PEER_DKEY = 128
PEER_TOPK = 16
PAGE_SIZE = 128

DN_QK = DN_HEADS * DN_D
CONV_CH = 3 * DN_QK
FOX_W = FOX_HEADS * FOX_HD
MEM_W = MEM_HEADS * MEM_HD
_OFF_Z = CONV_CH
_OFF_B = _OFF_Z + DN_QK
_OFF_A = _OFF_B + DN_HEADS
_OFF_FQ = _OFF_A + DN_HEADS
_OFF_FK = _OFF_FQ + FOX_W
_OFF_FV = _OFF_FK + FOX_W
_OFF_FF = _OFF_FV + FOX_W
_OFF_MQ = _OFF_FF + FOX_HEADS
_IN_COLS = _OFF_MQ + MEM_W
P_DN_COLS = CONV_CH + DN_QK
LANE = 128
SMALL_COLS = LANE

VMEM_LIMIT_BYTES = 56 * 1024 * 1024


def _cparams(*sem):
    return pltpu.CompilerParams(dimension_semantics=sem, vmem_limit_bytes=VMEM_LIMIT_BYTES)


def _sigmoid(x):
    return 1.0 / (1.0 + jnp.exp(-x))


def _softplus(x):
    return jnp.maximum(x, 0.0) + jnp.log1p(jnp.exp(-jnp.abs(x)))


def _dot_hi(a, b):
    return jnp.dot(a, b, precision=HIGHEST, preferred_element_type=F32)


def _split_bf16(x):
    hi = x.astype(BF16)
    return hi, (x - hi.astype(F32)).astype(BF16)


def _dot_split(a, b):
    (ah, al), (bh, bl) = a, b
    small = jnp.dot(ah, bl, preferred_element_type=F32) + jnp.dot(al, bh, preferred_element_type=F32)
    return small + jnp.dot(ah, bh, preferred_element_type=F32)


def _dot_nt(a, b):
    return lax.dot_general(a, b, (((1,), (1,)), ((), ())), preferred_element_type=F32)


def _row_to_col(row, n):
    a = lax.broadcasted_iota(jnp.int32, (n, n), 0)
    b = lax.broadcasted_iota(jnp.int32, (n, n), 1)
    return jnp.sum(jnp.where(a == b, row, 0.0), axis=1, keepdims=True)


def _rmsnorm_kernel(x_ref, g_ref, o_ref):
    x = x_ref[...]
    y = x * lax.rsqrt(jnp.mean(x * x, axis=-1, keepdims=True) + EPS) * g_ref[...]
    o_ref[...] = y.astype(o_ref.dtype)


def rmsnorm_rows(x, g, tm, out_dtype=BF16):
    m, k = x.shape
    return pl.pallas_call(
        _rmsnorm_kernel,
        grid=(m // tm,),
        in_specs=[pl.BlockSpec((tm, k), lambda i: (i, 0)), pl.BlockSpec((1, k), lambda i: (0, 0))],
        out_specs=pl.BlockSpec((tm, k), lambda i: (i, 0)),
        out_shape=jax.ShapeDtypeStruct((m, k), out_dtype),
        compiler_params=_cparams("parallel"),
        name="rmsnorm_rows",
    )(x, g.reshape(1, k))


def _head_rmsnorm_kernel(x_ref, g_ref, o_ref, *, heads, hd):
    g = g_ref[...]
    for h in range(heads):
        x = x_ref[:, h * hd:(h + 1) * hd]
        y = x * lax.rsqrt(jnp.mean(x * x, axis=-1, keepdims=True) + EPS) * g
        o_ref[:, h * hd:(h + 1) * hd] = y.astype(o_ref.dtype)


def head_rmsnorm(x, col_block, g, heads, hd, tm, out_dtype=F32):
    m = x.shape[0]
    w = heads * hd
    return pl.pallas_call(
        functools.partial(_head_rmsnorm_kernel, heads=heads, hd=hd),
        grid=(m // tm,),
        in_specs=[pl.BlockSpec((tm, w), lambda i: (i, col_block)), pl.BlockSpec((1, hd), lambda i: (0, 0))],
        out_specs=pl.BlockSpec((tm, w), lambda i: (i, 0)),
        out_shape=jax.ShapeDtypeStruct((m, w), out_dtype),
        compiler_params=_cparams("parallel"),
        name="head_rmsnorm",
    )(x, g.reshape(1, hd))


def _matmul_kernel(*refs, ksizes, has_res):
    n_a = len(ksizes)
    w_ref = refs[n_a]
    o_ref = refs[-1]
    acc = None
    off = 0
    for a_ref, ks in zip(refs[:n_a], ksizes):
        part = jnp.dot(a_ref[...], w_ref[off:off + ks, :], preferred_element_type=F32)
        acc = part if acc is None else acc + part
        off += ks
    if has_res:
        acc = acc + refs[n_a + 1][...]
    o_ref[...] = acc


def matmul(a_list, w, tm, tn, residual=None):
    m = a_list[0].shape[0]
    k, n = w.shape
    ksizes = tuple(a.shape[1] for a in a_list)
    assert sum(ksizes) == k and m % tm == 0 and n % tn == 0
    in_specs = [pl.BlockSpec((tm, ks), lambda i, j: (i, 0)) for ks in ksizes]
    in_specs.append(pl.BlockSpec((k, tn), lambda i, j: (0, j)))
    args = list(a_list) + [w]
    if residual is not None:
        in_specs.append(pl.BlockSpec((tm, tn), lambda i, j: (i, j)))
        args.append(residual)
    return pl.pallas_call(
        functools.partial(_matmul_kernel, ksizes=ksizes, has_res=residual is not None),
        grid=(m // tm, n // tn),
        in_specs=in_specs,
        out_specs=pl.BlockSpec((tm, tn), lambda i, j: (i, j)),
        out_shape=jax.ShapeDtypeStruct((m, n), F32),
        compiler_params=_cparams("parallel", "parallel"),
        name="matmul",
    )(*args)


def _matmul_nt_kernel(a_ref, w_ref, o_ref):
    o_ref[...] = _dot_nt(a_ref[...], w_ref[...])


def matmul_nt(a, w_t, tm, tn):
    m, k = a.shape
    n = w_t.shape[0]
    assert w_t.shape[1] == k and m % tm == 0 and n % tn == 0
    return pl.pallas_call(
        _matmul_nt_kernel,
        grid=(m // tm, n // tn),
        in_specs=[pl.BlockSpec((tm, k), lambda i, j: (i, 0)), pl.BlockSpec((tn, k), lambda i, j: (j, 0))],
        out_specs=pl.BlockSpec((tm, tn), lambda i, j: (i, j)),
        out_shape=jax.ShapeDtypeStruct((m, n), F32),
        compiler_params=_cparams("parallel", "parallel"),
        name="matmul_nt",
    )(a, w_t)


def _gates_kernel(p_ref, alog_ref, dtb_ref, fb_ref, o_ref):
    x = p_ref[...]
    lane = lax.broadcasted_iota(jnp.int32, x.shape, 1)
    beta = _sigmoid(x)
    g = -jnp.exp(alog_ref[...]) * _softplus(x + dtb_ref[...])
    logf = -_softplus(-(x + fb_ref[...]))
    out = jnp.where(lane < DN_HEADS, beta,
                    jnp.where(lane < 2 * DN_HEADS, g,
                              jnp.where(lane < 2 * DN_HEADS + FOX_HEADS, logf, 0.0)))
    o_ref[...] = out


def gates(p_small, a_log, dt_bias, f_bias, tm):
    m = p_small.shape[0]

    def pad(v, off):
        return jnp.zeros((1, SMALL_COLS), F32).at[0, off:off + v.shape[0]].set(v)

    row = pl.BlockSpec((1, SMALL_COLS), lambda i: (0, 0))
    return pl.pallas_call(
        _gates_kernel,
        grid=(m // tm,),
        in_specs=[pl.BlockSpec((tm, SMALL_COLS), lambda i: (i, 0)), row, row, row],
        out_specs=pl.BlockSpec((tm, SMALL_COLS), lambda i: (i, 0)),
        out_shape=jax.ShapeDtypeStruct((m, SMALL_COLS), F32),
        compiler_params=_cparams("parallel"),
        name="gates",
    )(p_small, pad(a_log, DN_HEADS), pad(dt_bias, DN_HEADS), pad(f_bias, 2 * DN_HEADS))


def _cumsum_kernel(x_ref, o_ref, *, blocks_per_group):
    x = x_ref[...]
    r, n = x.shape
    a = lax.broadcasted_iota(jnp.int32, (n, n), 0)
    b = lax.broadcasted_iota(jnp.int32, (n, n), 1)
    local = _dot_hi(x, (a <= b).astype(F32))
    tot = jnp.broadcast_to(local[:, n - 1:n], (r, n))
    ra = lax.broadcasted_iota(jnp.int32, (r, r), 0)
    rb = lax.broadcasted_iota(jnp.int32, (r, r), 1)
    earlier = jnp.logical_and(rb < ra, (ra // blocks_per_group) == (rb // blocks_per_group)).astype(F32)
    o_ref[...] = local + _dot_hi(earlier, tot)


def cumsum_time(x, blocks_per_group):
    return pl.pallas_call(
        functools.partial(_cumsum_kernel, blocks_per_group=blocks_per_group),
        out_shape=jax.ShapeDtypeStruct(x.shape, F32),
        compiler_params=pltpu.CompilerParams(vmem_limit_bytes=VMEM_LIMIT_BYTES),
        name="cumsum_time",
    )(x)


def _fox_prompt_kernel(q_ref, k_ref, v_ref, c_ref, o_ref, *, blk, hg, scale):
    i = pl.program_id(2)
    q0 = pl.multiple_of(i * blk, blk)
    hd = FOX_HD
    qs = [q_ref[:, h * hd:(h + 1) * hd] for h in range(hg)]
    cqs = [_row_to_col(c_ref[h, :, pl.ds(q0, blk)], blk) for h in range(hg)]

    def scores(h, k0):
        k = k_ref[pl.ds(k0, blk), h * hd:(h + 1) * hd]
        return _dot_nt(qs[h], k) * scale + (cqs[h] - c_ref[h, :, pl.ds(k0, blk)])

    def update(h, state, s, k0):
        m, l, acc = state
        m_new = jnp.maximum(m, jnp.max(s, axis=1, keepdims=True))
        alpha = jnp.exp(m - m_new)
        p = jnp.exp(s - m_new)
        l = l * alpha + jnp.sum(p, axis=1, keepdims=True)
        v = v_ref[pl.ds(k0, blk), h * hd:(h + 1) * hd]
        acc = acc * alpha + jnp.dot(p.astype(BF16), v, preferred_element_type=F32)
        return m_new, l, acc

    def body(j, carry):
        k0 = pl.multiple_of(j * blk, blk)
        ss = [scores(h, k0) for h in range(hg)]
        return tuple(update(h, carry[h], ss[h], k0) for h in range(hg))

    init = tuple((jnp.full((blk, 1), NEG_INF, F32), jnp.zeros((blk, 1), F32), jnp.zeros((blk, hd), F32))
                 for _ in range(hg))
    carry = lax.fori_loop(0, i, body, init)
    causal = (lax.broadcasted_iota(jnp.int32, (blk, blk), 1) <= lax.broadcasted_iota(jnp.int32, (blk, blk), 0))
    for h in range(hg):
        s = jnp.where(causal, scores(h, q0), NEG_INF)
        _, l, acc = update(h, carry[h], s, q0)
        o_ref[:, h * hd:(h + 1) * hd] = (acc / l).astype(o_ref.dtype)


def fox_prompt(qn, kn, v, c_row, batch, seq, blk, hg):
    nq = seq // blk
    ng = FOX_HEADS // hg
    w = hg * FOX_HD
    kv_spec = pl.BlockSpec((seq, w), lambda b, h, i: (b, h))
    return pl.pallas_call(
        functools.partial(_fox_prompt_kernel, blk=blk, hg=hg, scale=FOX_HD ** -0.5),
        grid=(batch, ng, nq),
        in_specs=[pl.BlockSpec((blk, w), lambda b, h, i: (b * nq + i, h)), kv_spec, kv_spec,
                  pl.BlockSpec((hg, 1, seq), lambda b, h, i: (b * ng + h, 0, 0))],
        out_specs=pl.BlockSpec((blk, w), lambda b, h, i: (b * nq + i, h)),
        out_shape=jax.ShapeDtypeStruct((batch * seq, FOX_W), BF16),
        compiler_params=_cparams("parallel", "parallel", "arbitrary"),
        name="fox_prompt",
    )(qn, kn, v, c_row)


def _dn_prompt_kernel(q_ref, k_ref, v_ref, z_ref, wq_ref, wk_ref, wv_ref, g_ref, b_ref, ng_ref,
                      o_ref, s_out_ref, s_ref, xq_ref, xk_ref, xv_ref, *, tc, chunk, hg):
    t = pl.program_id(2)
    nc = tc // chunk
    pad = 8

    @pl.when(t == 0)
    def _():
        s_ref[...] = jnp.zeros_like(s_ref)
        for buf in (xq_ref, xk_ref, xv_ref):
            buf[0:pad, :] = jnp.zeros((pad, hg * DN_D), F32)

    def conv_silu(x_ref, w_ref, buf_ref):
        u = x_ref[...]
        buf_ref[pad:pad + tc, :] = u
        w = w_ref[...]
        out = buf_ref[pad - 3:pad - 3 + tc, :] * w[0:1, :]
        out = out + buf_ref[pad - 2:pad - 2 + tc, :] * w[1:2, :]
        out = out + buf_ref[pad - 1:pad - 1 + tc, :] * w[2:3, :]
        out = out + u * w[3:4, :]
        buf_ref[0:pad, :] = u[tc - pad:tc, :]
        return out * _sigmoid(out)

    def l2norm(x):
        return x * lax.rsqrt(jnp.sum(x * x, axis=-1, keepdims=True) + EPS)

    qc = conv_silu(q_ref, wq_ref, xq_ref)
    kc = conv_silu(k_ref, wk_ref, xk_ref)
    vc = conv_silu(v_ref, wv_ref, xv_ref)

    ra = lax.broadcasted_iota(jnp.int32, (chunk, chunk), 0)
    rb = lax.broadcasted_iota(jnp.int32, (chunk, chunk), 1)
    tri = ra >= rb
    strict = ra > rb
    eye_f = (ra == rb).astype(F32)
    upper_f = (ra <= rb).astype(F32)

    heads = []
    for hh in range(hg):
        cs = slice(hh * DN_D, (hh + 1) * DN_D)
        heads.append(dict(
            q=l2norm(qc[:, cs]) * (DN_D ** -0.5), k=l2norm(kc[:, cs]), v=vc[:, cs],
            gc=_dot_hi(g_ref[hh], upper_f),
            beta=b_ref[hh], s=s_ref[hh], outs=[]))

    work = []
    for c in range(nc):
        sl = slice(c * chunk, (c + 1) * chunk)
        for hd in heads:
            q, k, v = hd['q'][sl], hd['k'][sl], hd['v'][sl]
            gc_row = hd['gc'][c:c + 1, :]
            gc = _row_to_col(gc_row, chunk)
            beta = _row_to_col(hd['beta'][c:c + 1, :], chunk)
            decay = jnp.where(tri, jnp.exp(jnp.where(tri, gc - gc_row, 0.0)), 0.0)
            qk_b = jnp.concatenate([q, k], axis=0).astype(BF16)
            work.append(dict(hd=hd, q=q, k=k, v=v, gc=gc, beta=beta, decay=decay, qk_b=qk_b,
                             gc_last=gc_row[:, chunk - 1:chunk]))
    for wk in work:
        wk['gram'] = _dot_nt(wk['qk_b'], wk['qk_b'][chunk:, :])
    for wk in work:
        lower = jnp.where(strict, wk['beta'] * wk['gram'][chunk:, :] * wk['decay'], 0.0)
        wk['inv'] = eye_f - lower
        wk['power'] = _split_bf16(lower)
    for wk in work:
        wk['power'] = _split_bf16(_dot_split(wk['power'], wk['power']))
    span = 2
    while span < chunk:
        for wk in work:
            wk['inv'] = wk['inv'] + _dot_split(_split_bf16(wk['inv']), wk['power'])
        span *= 2
        if span < chunk:
            for wk in work:
                wk['power'] = _split_bf16(_dot_split(wk['power'], wk['power']))
    for wk in work:
        egc = jnp.exp(wk['gc'])
        k, beta = wk['k'], wk['beta']
        uw = _dot_split(_split_bf16(wk['inv']),
                        _split_bf16(jnp.concatenate([wk['v'] * beta, k * (beta * egc)], axis=1)))
        wk['u'] = uw[:, :DN_D]
        wk['lhs1'] = jnp.concatenate([uw[:, DN_D:], wk['q'] * egc], axis=0).astype(BF16)
        qk = jnp.where(tri, wk['gram'][:chunk, :] * wk['decay'], 0.0)
        k_dec = k * jnp.exp(wk['gc_last'] - wk['gc'])
        wk['lhs2'] = jnp.concatenate([qk, k_dec.T], axis=0).astype(BF16)
        wk['g_last'] = jnp.exp(wk['gc_last'])
    for wk in work:
        hd = wk['hd']
        s = hd['s']
        ws = jnp.dot(wk['lhs1'], s.astype(BF16), preferred_element_type=F32)
        v_new = wk['u'] - ws[:chunk, :]
        upd = jnp.dot(wk['lhs2'], v_new.astype(BF16), preferred_element_type=F32)
        hd['outs'].append(ws[chunk:, :] + upd[:chunk, :])
        hd['s'] = s * wk['g_last'] + upd[chunk:, :]

    ng = ng_ref[...]
    for hh, hd in enumerate(heads):
        cs = slice(hh * DN_D, (hh + 1) * DN_D)
        s_ref[hh] = hd['s']
        s_out_ref[hh] = hd['s']
        o = jnp.concatenate(hd['outs'], axis=0)
        o = o * lax.rsqrt(jnp.mean(o * o, axis=-1, keepdims=True) + EPS) * ng
        z = z_ref[:, cs]
        o_ref[:, cs] = (o * (z * _sigmoid(z))).astype(o_ref.dtype)


def dn_prompt(p, conv_w, g_chunks, beta_chunks, norm_g, batch, seq, tc, hg):
    nt = seq // tc
    nc = tc // DN_CHUNK
    ng = DN_HEADS // hg
    w = hg * DN_D

    def col(off):
        return pl.BlockSpec((tc, w), lambda b, h, t: (b * nt + t, off * ng + h))

    def wcol(off):
        return pl.BlockSpec((DN_CONV, w), lambda b, h, t: (0, off * ng + h))

    gspec = pl.BlockSpec((hg, nc, DN_CHUNK), lambda b, h, t: (b * ng + h, t, 0))
    return pl.pallas_call(
        functools.partial(_dn_prompt_kernel, tc=tc, chunk=DN_CHUNK, hg=hg),
        grid=(batch, ng, nt),
        in_specs=[col(0), col(1), col(2), col(3), wcol(0), wcol(1), wcol(2), gspec, gspec,
                  pl.BlockSpec((1, DN_D), lambda b, h, t: (0, 0))],
        out_specs=[pl.BlockSpec((tc, w), lambda b, h, t: (b * nt + t, h)),
                   pl.BlockSpec((hg, DN_D, DN_D), lambda b, h, t: (b * ng + h, 0, 0))],
        out_shape=[jax.ShapeDtypeStruct((batch * seq, DN_QK), BF16),
                   jax.ShapeDtypeStruct((batch * DN_HEADS, DN_D, DN_D), F32)],
        scratch_shapes=[pltpu.VMEM((hg, DN_D, DN_D), F32)] + [pltpu.VMEM((tc + 8, w), F32)] * 3,
        compiler_params=_cparams("parallel", "parallel", "arbitrary"),
        name="dn_prompt",
    )(p, p, p, p, conv_w, conv_w, conv_w, g_chunks, beta_chunks, norm_g.reshape(1, DN_D))


def _dn_sample_kernel(q_ref, k_ref, v_ref, z_ref, cq_ref, ck_ref, cv_ref, wq_ref, wk_ref, wv_ref,
                      g_ref, b_ref, ng_ref, s_ref, o_ref, s_out_ref, *, bb):
    def conv_silu(x_ref, c_ref, w_ref):
        w = w_ref[...]
        out = c_ref[:, 0, :] * w[0:1, :]
        out = out + c_ref[:, 1, :] * w[1:2, :]
        out = out + c_ref[:, 2, :] * w[2:3, :]
        out = out + x_ref[...] * w[3:4, :]
        return out * _sigmoid(out)

    def l2norm(x):
        return x * lax.rsqrt(jnp.sum(x * x, axis=-1, keepdims=True) + EPS)

    q = l2norm(conv_silu(q_ref, cq_ref, wq_ref)) * (DN_D ** -0.5)
    k = l2norm(conv_silu(k_ref, ck_ref, wk_ref))
    v = conv_silu(v_ref, cv_ref, wv_ref)
    q_t = q.T
    k_t = k.T
    decay = jnp.exp(g_ref[0])
    beta = b_ref[0]
    rows = []
    for b in range(bb):
        s = s_ref[b, 0] * decay[b:b + 1, :]
        k_col = k_t[:, b:b + 1]
        kv = jnp.sum(k_col * s, axis=0, keepdims=True)
        s = s + k_col * ((v[b:b + 1, :] - kv) * beta[b:b + 1, :])
        s_out_ref[b, 0] = s
        rows.append(jnp.sum(q_t[:, b:b + 1] * s, axis=0, keepdims=True))
    o = jnp.concatenate(rows, axis=0)
    o = o * lax.rsqrt(jnp.mean(o * o, axis=-1, keepdims=True) + EPS) * ng_ref[...]
    z = z_ref[...]
    o_ref[...] = (o * (z * _sigmoid(z))).astype(o_ref.dtype)


def dn_sample(p, state_conv, conv_w, g_t, beta_t, norm_g, state, bb):
    nb = p.shape[0]
    hb = DN_HEADS

    def col(off):
        return pl.BlockSpec((bb, DN_D), lambda i, h: (i, off + h))

    def ccol(off):
        return pl.BlockSpec((bb, DN_CONV - 1, DN_D), lambda i, h: (i, 0, off + h))

    def wcol(off):
        return pl.BlockSpec((DN_CONV, DN_D), lambda i, h: (0, off + h))

    gspec = pl.BlockSpec((1, bb, 1), lambda i, h: (h, i, 0))
    sspec = pl.BlockSpec((bb, 1, DN_D, DN_D), lambda i, h: (i, h, 0, 0))
    return pl.pallas_call(
        functools.partial(_dn_sample_kernel, bb=bb),
        grid=(nb // bb, DN_HEADS),
        in_specs=[col(0), col(hb), col(2 * hb), col(3 * hb), ccol(0), ccol(hb), ccol(2 * hb),
                  wcol(0), wcol(hb), wcol(2 * hb), gspec, gspec,
                  pl.BlockSpec((1, DN_D), lambda i, h: (0, 0)), sspec],
        out_specs=[pl.BlockSpec((bb, DN_D), lambda i, h: (i, h)), sspec],
        out_shape=[jax.ShapeDtypeStruct((nb, DN_QK), BF16), jax.ShapeDtypeStruct(state.shape, F32)],
        compiler_params=_cparams("parallel", "parallel"),
        name="dn_sample",
    )(p, p, p, p, state_conv, state_conv, state_conv, conv_w, conv_w, conv_w, g_t, beta_t,
      norm_g.reshape(1, DN_D), state)


def _fox_sample_kernel(pt_ref, q_ref, kn_ref, vn_ref, lfn_ref, *refs, pp, scale):
    k_refs = refs[0:pp]
    v_refs = refs[pp:2 * pp]
    lf_refs = refs[2 * pp:3 * pp]
    o_ref = refs[3 * pp]
    m_ref, l_ref, acc_ref, carry_ref = refs[3 * pp + 1:]
    j = pl.program_id(1)
    qs = q_ref[0] * scale

    @pl.when(j == 0)
    def _():
        m_ref[...] = jnp.broadcast_to(jnp.sum(qs * kn_ref[0], axis=1, keepdims=True), m_ref.shape)
        l_ref[...] = jnp.ones_like(l_ref)
        acc_ref[...] = vn_ref[0]
        carry_ref[...] = lfn_ref[0]

    ra = lax.broadcasted_iota(jnp.int32, (PAGE_SIZE, PAGE_SIZE), 0)
    rb = lax.broadcasted_iota(jnp.int32, (PAGE_SIZE, PAGE_SIZE), 1)
    later_f = (ra > rb).astype(F32)
    shape3 = (PAGE_SIZE, FOX_HEADS, FOX_HD)
    diag3 = lax.broadcasted_iota(jnp.int32, shape3, 0) == lax.broadcasted_iota(jnp.int32, shape3, 2)
    carry = carry_ref[...]
    logits = []
    for i in range(pp):
        lf = lf_refs[i][0]
        bias = _dot_hi(lf, later_f) + carry
        carry = carry + jnp.sum(lf, axis=1, keepdims=True)
        s3 = jnp.sum(k_refs[i][0] * qs[None] + jnp.where(diag3, bias[None], 0.0), axis=2, keepdims=True)
        logits.append(jnp.broadcast_to(s3, shape3))
    carry_ref[...] = carry
    m_new, l_new, acc = m_ref[...], l_ref[...], acc_ref[...]
    half = max(pp // 4, 1)
    for first in range(0, pp, half):
        group = range(first, min(first + half, pp))
        m_old = m_new
        for i in group:
            m_new = jnp.maximum(m_new, jnp.max(logits[i], axis=0))
        alpha = jnp.exp(m_old - m_new)
        l_new = l_new * alpha
        acc = acc * alpha
        for i in group:
            p3 = jnp.exp(logits[i] - m_new[None])
            l_new = l_new + jnp.sum(p3, axis=0)
            acc = acc + jnp.sum(p3 * v_refs[i][0], axis=0)
    l_ref[...] = l_new
    acc_ref[...] = acc
    m_ref[...] = m_new

    @pl.when(j == pl.num_programs(1) - 1)
    def _():
        o_ref[0] = (acc_ref[...] / l_ref[...]).astype(o_ref.dtype)


def fox_sample(q, k_new, v_new, lf_new, k_pool, v_pool, lf_pool_t, page_table, pp):
    nb, n_pages = page_table.shape
    assert n_pages % pp == 0
    steps = n_pages // pp

    def page_idx4(i):
        return lambda b, j, pt: (pt[b, n_pages - 1 - (j * pp + i)], 0, 0, 0)

    def page_idx3(i):
        return lambda b, j, pt: (pt[b, n_pages - 1 - (j * pp + i)], 0, 0)

    tok = lambda b, j, pt: (b, 0, 0)
    in_specs = [pl.BlockSpec((1, FOX_HEADS, FOX_HD), tok)] * 3 + [pl.BlockSpec((1, FOX_HEADS, 1), tok)]
    in_specs += [pl.BlockSpec((1, PAGE_SIZE, FOX_HEADS, FOX_HD), page_idx4(i)) for i in range(pp)]
    in_specs += [pl.BlockSpec((1, PAGE_SIZE, FOX_HEADS, FOX_HD), page_idx4(i)) for i in range(pp)]
    in_specs += [pl.BlockSpec((1, FOX_HEADS, PAGE_SIZE), page_idx3(i)) for i in range(pp)]
    grid_spec = pltpu.PrefetchScalarGridSpec(
        num_scalar_prefetch=1,
        grid=(nb, steps),
        in_specs=in_specs,
        out_specs=pl.BlockSpec((1, FOX_HEADS, FOX_HD), tok),
        scratch_shapes=[pltpu.VMEM((FOX_HEADS, FOX_HD), F32), pltpu.VMEM((FOX_HEADS, FOX_HD), F32),
                        pltpu.VMEM((FOX_HEADS, FOX_HD), F32), pltpu.VMEM((FOX_HEADS, 1), F32)],
    )
    return pl.pallas_call(
        functools.partial(_fox_sample_kernel, pp=pp, scale=FOX_HD ** -0.5),
        grid_spec=grid_spec,
        out_shape=jax.ShapeDtypeStruct((nb, FOX_HEADS, FOX_HD), BF16),
        compiler_params=_cparams("parallel", "arbitrary"),
        name="fox_sample",
    )(page_table, q, k_new, v_new, lf_new, *([k_pool] * pp), *([v_pool] * pp), *([lf_pool_t] * pp))


def _mem_prompt_kernel(q_ref, k_ref, v_ref, g_ref, o_ref, *, scale):
    g = g_ref[...]
    for h in range(MEM_HEADS):
        cs = slice(h * MEM_HD, (h + 1) * MEM_HD)
        q = q_ref[:, cs]
        qn = (q * lax.rsqrt(jnp.mean(q * q, axis=-1, keepdims=True) + EPS) * g).astype(BF16)
        s = _dot_nt(qn, k_ref[:, cs]) * scale
        p = jnp.exp(s - jnp.max(s, axis=1, keepdims=True))
        p = p / jnp.sum(p, axis=1, keepdims=True)
        o_ref[:, cs] = jnp.dot(p.astype(BF16), v_ref[:, cs], preferred_element_type=F32).astype(o_ref.dtype)


def mem_attend_prompt(p, mk, mv, qn_g, batch, seq, tq):
    nq = seq // tq
    kv = pl.BlockSpec((MEM_TOKENS, MEM_W), lambda b, i: (b, 0))
    return pl.pallas_call(
        functools.partial(_mem_prompt_kernel, scale=MEM_HD ** -0.5),
        grid=(batch, nq),
        in_specs=[pl.BlockSpec((tq, MEM_W), lambda b, i: (b * nq + i, 0)), kv, kv,
                  pl.BlockSpec((1, MEM_HD), lambda b, i: (0, 0))],
        out_specs=pl.BlockSpec((tq, MEM_W), lambda b, i: (b * nq + i, 0)),
        out_shape=jax.ShapeDtypeStruct((batch * seq, MEM_W), BF16),
        compiler_params=_cparams("parallel", "parallel"),
        name="mem_prompt",
    )(p, mk, mv, qn_g.reshape(1, MEM_HD))


def _mem_sample_kernel(q_ref, k_ref, v_ref, g_ref, o_ref, *, scale):
    q = q_ref[0]
    qn = q * lax.rsqrt(jnp.mean(q * q, axis=-1, keepdims=True) + EPS) * g_ref[...] * scale
    s3 = jnp.sum(k_ref[0] * qn[None], axis=2, keepdims=True)
    p3 = jnp.exp(s3 - jnp.max(s3, axis=0)[None])
    o = jnp.sum(p3 * v_ref[0], axis=0) / jnp.sum(p3, axis=0)
    o_ref[0] = o.astype(o_ref.dtype)


def mem_sample(q, mk, mv, qn_g):
    nb = q.shape[0]
    tok = pl.BlockSpec((1, MEM_HEADS, MEM_HD), lambda b: (b, 0, 0))
    kv = pl.BlockSpec((1, MEM_TOKENS, MEM_HEADS, MEM_HD), lambda b: (b, 0, 0, 0))
    return pl.pallas_call(
        functools.partial(_mem_sample_kernel, scale=MEM_HD ** -0.5),
        grid=(nb,),
        in_specs=[tok, kv, kv, pl.BlockSpec((1, MEM_HD), lambda b: (0, 0))],
        out_specs=tok,
        out_shape=jax.ShapeDtypeStruct((nb, MEM_HEADS, MEM_HD), BF16),
        compiler_params=_cparams("parallel"),
        name="mem_sample",
    )(q, mk, mv, qn_g.reshape(1, MEM_HD))


PEER_NO_RANK = float(PEER_NKEYS)


def _top_values(s, count, with_rank=False):
    rows = s.shape[0]
    idx = lax.broadcasted_iota(jnp.int32, s.shape, 0).astype(F32)
    rank = jnp.full(s.shape, PEER_NO_RANK, F32) if with_rank else None
    vals = []
    for r in range(count):
        m = jnp.max(s, axis=0, keepdims=True)
        picked = idx == jnp.min(jnp.where(s == m, idx, float(rows)), axis=0, keepdims=True)
        s = jnp.where(picked, NEG_INF, s)
        if with_rank:
            rank = jnp.where(picked, float(r), rank)
        vals.append(m)
    return jnp.concatenate(vals, axis=0), rank


def _peer_stats_kernel(hn_ref, wq_ref, keys_ref, cnt_ref, g1_ref, r2_ref, e2_ref, st_ref, top_ref, rank_ref,
                       cnt_scr, g1_scr):
    nk = PEER_NKEYS
    kk = PEER_TOPK
    q = jnp.dot(hn_ref[...], wq_ref[...], preferred_element_type=F32)
    st_ref[...] = _dot_nt(keys_ref[...], q.astype(BF16))

    def half_body(c, carry):
        r0 = pl.multiple_of(c * nk, nk)
        t0 = pl.multiple_of(c * kk, kk)
        vals, rank = _top_values(st_ref[pl.ds(r0, nk), :], kk, with_rank=True)
        top_ref[pl.ds(t0, kk), :] = vals
        rank_ref[pl.ds(r0, nk), :] = rank
        return carry

    lax.fori_loop(0, 2 * PEER_HEADS, half_body, 0)

    def head_body(h, carry):
        t0 = pl.multiple_of(h * 2 * kk, 2 * kk)
        v1 = top_ref[pl.ds(t0, kk), :]
        v2 = top_ref[pl.ds(t0 + kk, kk), :]
        cand = jnp.concatenate([v1[0:1, :] + v2] + [v1[a:a + 1, :] + v2[0:8, :] for a in range(1, 8)]
                               + [v1[8:kk, :] + v2[0:1, :]], axis=0)
        cv, _ = _top_values(cand, kk)
        z = jnp.sum(jnp.exp(cv - cv[0:1, :]), axis=0, keepdims=True)
        tau = cv[kk - 1:kk, :]
        r0 = pl.multiple_of(h * 2 * nk, 2 * nk)
        s1 = st_ref[pl.ds(r0, nk), :]
        s2 = st_ref[pl.ds(r0 + nk, nk), :]
        rank1 = rank_ref[pl.ds(r0, nk), :]
        cnt = jnp.zeros_like(s1)
        for a in range(kk):
            n_hit = jnp.sum(((v1[a:a + 1, :] + v2) >= tau).astype(F32), axis=0, keepdims=True)
            cnt = jnp.where(rank1 == float(a), n_hit, cnt)
        cnt_scr[h] = cnt
        g1_scr[h] = jnp.exp(s1 - v1[0:1, :]) / z
        r2_ref[h] = rank_ref[pl.ds(r0 + nk, nk), :].astype(BF16)
        e2_ref[h] = jnp.exp(s2 - v2[0:1, :]).astype(BF16)
        return carry

    lax.fori_loop(0, PEER_HEADS, head_body, 0)
    for h in range(PEER_HEADS):
        cnt_ref[:, h, :] = cnt_scr[h]
        g1_ref[:, h, :] = g1_scr[h]


def peer_stats(hn, w_q, keys_t, mt):
    m = hn.shape[0]
    nrow = PEER_HEADS * 2 * PEER_NKEYS
    hspec = pl.BlockSpec((PEER_HEADS, PEER_NKEYS, mt), lambda i: (0, 0, i))
    kspec = pl.BlockSpec((PEER_NKEYS, PEER_HEADS, mt), lambda i: (0, 0, i))
    f32_shape = jax.ShapeDtypeStruct((PEER_NKEYS, PEER_HEADS, m), F32)
    bf16_shape = jax.ShapeDtypeStruct((PEER_HEADS, PEER_NKEYS, m), BF16)
    return pl.pallas_call(
        _peer_stats_kernel,
        grid=(m // mt,),
        in_specs=[pl.BlockSpec((mt, D_MODEL), lambda i: (i, 0)),
                  pl.BlockSpec(w_q.shape, lambda i: (0, 0)),
                  pl.BlockSpec(keys_t.shape, lambda i: (0, 0))],
        out_specs=[kspec, kspec, hspec, hspec],
        out_shape=[f32_shape, f32_shape, bf16_shape, bf16_shape],
        scratch_shapes=[pltpu.VMEM((nrow, mt), F32), pltpu.VMEM((2 * PEER_HEADS * PEER_TOPK, mt), F32),
                        pltpu.VMEM((nrow, mt), F32), pltpu.VMEM((PEER_HEADS, PEER_NKEYS, mt), F32),
                        pltpu.VMEM((PEER_HEADS, PEER_NKEYS, mt), F32)],
        compiler_params=_cparams("parallel"),
        name="peer_stats",
    )(hn, w_q, keys_t)


def _peer_gates(cnt_ref, g1_ref, r2_ref, e2_ref, gate_ref, n_a):
    mt = gate_ref.shape[1]
    rows = 16
    heads = range(PEER_HEADS)
    for a in range(n_a):
        for c in range(mt // LANE):
            cs = slice(c * LANE, (c + 1) * LANE)
            cnt = [jnp.broadcast_to(cnt_ref[a, h:h + 1, cs], (rows, LANE)).astype(BF16) for h in heads]
            g1 = [jnp.broadcast_to(g1_ref[a, h:h + 1, cs], (rows, LANE)).astype(BF16) for h in heads]
            for r in range(PEER_NKEYS // rows):
                rs = slice(r * rows, (r + 1) * rows)
                w = None
                for h in heads:
                    term = jnp.where(r2_ref[h, rs, cs] < cnt[h], e2_ref[h, rs, cs], 0.0) * g1[h]
                    w = term if w is None else w + term
                gate_ref[a * PEER_NKEYS + r * rows:a * PEER_NKEYS + (r + 1) * rows, cs] = w


def _peer_mix_kernel(hn_ref, res_ref, u_ref, v_ref, cnt_ref, g1_ref, cntn_ref, g1n_ref, r2_ref, e2_ref,
                     o_ref, gate_a_ref, gate_b_ref, *, et):
    e = pl.program_id(1)
    n_a = et // PEER_NKEYS

    @pl.when(e == 0)
    def _():
        o_ref[...] = res_ref[...]
        _peer_gates(cnt_ref, g1_ref, r2_ref, e2_ref, gate_a_ref, n_a)

    def step(gate_cur_ref, gate_next_ref):
        _peer_gates(cntn_ref, g1n_ref, r2_ref, e2_ref, gate_next_ref, n_a)
        act_t = _dot_nt(u_ref[...], hn_ref[...])
        gel_t = 0.5 * act_t * (1.0 + lax.erf(act_t * (0.5 ** 0.5)))
        hmat = (gate_cur_ref[...].astype(F32) * gel_t).astype(BF16).T
        o_ref[...] += jnp.dot(hmat, v_ref[...], preferred_element_type=F32)

    @pl.when(e % 2 == 0)
    def _():
        step(gate_a_ref, gate_b_ref)

    @pl.when(e % 2 == 1)
    def _():
        step(gate_b_ref, gate_a_ref)


def peer_mix(hn, res, u_tab, v_tab, cnt_t, g1_t, r2, e2, mt, et):
    m = hn.shape[0]
    n_exp = u_tab.shape[0]
    n_e = n_exp // et
    a_per = et // PEER_NKEYS
    once = pl.Buffered(1)
    sel1 = pl.BlockSpec((a_per, PEER_HEADS, mt), lambda i, e: (e, 0, i))
    sel1_next = pl.BlockSpec((a_per, PEER_HEADS, mt), lambda i, e: (jnp.minimum(e + 1, n_e - 1), 0, i))
    sel2 = pl.BlockSpec((PEER_HEADS, PEER_NKEYS, mt), lambda i, e: (0, 0, i), pipeline_mode=once)
    tab = pl.BlockSpec((et, D_MODEL), lambda i, e: (e, 0))
    return pl.pallas_call(
        functools.partial(_peer_mix_kernel, et=et),
        grid=(m // mt, n_e),
        in_specs=[pl.BlockSpec((mt, D_MODEL), lambda i, e: (i, 0), pipeline_mode=once),
                  pl.BlockSpec((mt, D_MODEL), lambda i, e: (i, 0), pipeline_mode=once), tab, tab,
                  sel1, sel1, sel1_next, sel1_next, sel2, sel2],
        out_specs=pl.BlockSpec((mt, D_MODEL), lambda i, e: (i, 0)),
        out_shape=jax.ShapeDtypeStruct((m, D_MODEL), F32),
        scratch_shapes=[pltpu.VMEM((et, mt), BF16), pltpu.VMEM((et, mt), BF16)],
        compiler_params=_cparams("parallel", "arbitrary"),
        name="peer_mix",
    )(hn, res, u_tab, v_tab, cnt_t, g1_t, cnt_t, g1_t, r2, e2)


def _prep_weights(w_in3, layer, w_mem_kv, w_out, peer_w_q, peer_sub_keys, peer_u, peer_v):
    w_t = jnp.swapaxes(w_in3[layer], 0, 1)
    w_dn = w_t[:_OFF_B].astype(BF16)
    w_fox = w_t[_OFF_FQ:_OFF_FF].astype(BF16)
    w_mq = w_t[_OFF_MQ:].astype(BF16)
    w_small = jnp.concatenate(
        [w_t[_OFF_B:_OFF_FQ], w_t[_OFF_FF:_OFF_MQ],
         jnp.zeros((SMALL_COLS - 2 * DN_HEADS - FOX_HEADS, D_MODEL), F32)], axis=0).astype(BF16)
    half = PEER_DKEY // 2
    pairs = 2 * PEER_HEADS
    sk = peer_sub_keys.reshape(pairs, PEER_NKEYS, half)
    eye = jnp.eye(pairs, dtype=F32)
    keys_t = (eye[:, None, :, None] * sk[:, :, None, :]).reshape(pairs * PEER_NKEYS, pairs * half).astype(BF16)
    return dict(w_dn=w_dn, w_fox=w_fox, w_mq=w_mq, w_small=w_small, w_mem_kv=w_mem_kv.astype(BF16), w_out=w_out.astype(BF16),
                peer_w_q=peer_w_q.astype(BF16), keys_t=keys_t, peer_u=peer_u.astype(BF16),
                peer_v=peer_v.astype(BF16))


def _tile(m, pref):
    return pref if m % pref == 0 else m


def _project(x2, ln_g, wts):
    m = x2.shape[0]
    xn = rmsnorm_rows(x2, ln_g, _tile(m, 256))
    tm = _tile(m, 1024)
    p_dn = matmul_nt(xn, wts['w_dn'], tm, 512)
    p_fox = matmul_nt(xn, wts['w_fox'], tm, 512)
    p_mq = matmul_nt(xn, wts['w_mq'], tm, 512)
    ps = matmul_nt(xn, wts['w_small'], tm, SMALL_COLS)
    return p_dn, p_fox, p_mq, ps


def _channel_mix(x2, o_dn, o_fox, o_mem, wts, ln_ffn_g):
    m = x2.shape[0]
    h = matmul([o_dn, o_fox, o_mem], wts['w_out'], _tile(m, 1024), 512, residual=x2)
    hn = rmsnorm_rows(h, ln_ffn_g, _tile(m, 256))
    cnt_t, g1_t, r2, e2 = peer_stats(hn, wts['peer_w_q'], wts['keys_t'], _tile(m, 512))
    return peer_mix(hn, h, wts['peer_u'], wts['peer_v'], cnt_t, g1_t, r2, e2, _tile(m, 512), 512)


def kernel(x_prompt, x_sample, cache_fox_k, cache_fox_v, cache_fox_logf, state_delta, state_conv, cache_mem_k, cache_mem_v, page_table, mem_prompt, ln_mix_g, w_in, conv_w, dn_a_log, dn_dt_bias, dn_norm_g, fox_f_bias, fox_qn_g, fox_kn_g, ln_mem_g, w_mem_kv, mem_qn_g, mem_kn_g, w_out, ln_ffn_g, peer_w_q, peer_sub_keys, peer_u, peer_v):
    depth = w_in.shape[0]
    assert depth == 1
    l = 0
    batch, seq, _ = x_prompt.shape
    nb = x_sample.shape[0]
    m_p = batch * seq
    wts = _prep_weights(w_in, l, w_mem_kv[l], w_out[l], peer_w_q[l], peer_sub_keys[l], peer_u[l], peer_v[l])

    x2 = x_prompt.reshape(m_p, D_MODEL)
    p, p_fox, p_mq, ps = _project(x2, ln_mix_g[l], wts)
    gt = gates(ps, dn_a_log[l], dn_dt_bias[l], fox_f_bias[l], 1024)
    beta = gt[:, :DN_HEADS]
    gdec = gt[:, DN_HEADS:2 * DN_HEADS]
    logf = gt[:, 2 * DN_HEADS:2 * DN_HEADS + FOX_HEADS]

    def to_chunks(a):
        return a.reshape(batch, seq, DN_HEADS).transpose(0, 2, 1).reshape(batch * DN_HEADS, seq // DN_CHUNK, DN_CHUNK)

    o_dn, dn_state = dn_prompt(p, conv_w[l], to_chunks(gdec), to_chunks(beta), dn_norm_g[l], batch, seq, 512, 8)

    fqn = head_rmsnorm(p_fox, 0, fox_qn_g[l], FOX_HEADS, FOX_HD, 512, out_dtype=BF16)
    fkn = head_rmsnorm(p_fox, 1, fox_kn_g[l], FOX_HEADS, FOX_HD, 512)
    fv = p_fox[:, 2 * FOX_W:]
    lf_rows = logf.reshape(batch, seq, FOX_HEADS).transpose(0, 2, 1).reshape(batch * FOX_HEADS * (seq // LANE), LANE)
    c_row = cumsum_time(lf_rows, seq // LANE).reshape(batch * FOX_HEADS, 1, seq)
    o_fox = fox_prompt(fqn, fkn.astype(BF16), fv.astype(BF16), c_row, batch, seq, 512, 2)

    mem2 = mem_prompt.reshape(batch * MEM_TOKENS, D_MODEL)
    memn = rmsnorm_rows(mem2, ln_mem_g[l], 256)
    mkv = matmul([memn], wts['w_mem_kv'], batch * MEM_TOKENS, 512)
    mk = head_rmsnorm(mkv, 0, mem_kn_g[l], MEM_HEADS, MEM_HD, 256)
    mv = mkv[:, MEM_W:]
    o_mem = mem_attend_prompt(p_mq, mk.astype(BF16), mv.astype(BF16), mem_qn_g[l], batch, seq, 512)

    y_p = _channel_mix(x2, o_dn, o_fox, o_mem, wts, ln_ffn_g[l])

    xs = x_sample.reshape(nb, D_MODEL)
    sp, sp_fox, sp_mq, sps = _project(xs, ln_mix_g[l], wts)
    sgt = gates(sps, dn_a_log[l], dn_dt_bias[l], fox_f_bias[l], nb)
    s_beta = sgt[:, :DN_HEADS].T.reshape(DN_HEADS, nb, 1)
    s_g = sgt[:, DN_HEADS:2 * DN_HEADS].T.reshape(DN_HEADS, nb, 1)
    s_logf = sgt[:, 2 * DN_HEADS:2 * DN_HEADS + FOX_HEADS]
    so_dn, s_state = dn_sample(sp, state_conv[l], conv_w[l], s_g, s_beta, dn_norm_g[l], state_delta[l], 32)
    conv_s = jnp.concatenate([state_conv[l][:, 1:, :], sp[:, None, :CONV_CH]], axis=1)

    sfq = head_rmsnorm(sp_fox, 0, fox_qn_g[l], FOX_HEADS, FOX_HD, nb)
    sfk = head_rmsnorm(sp_fox, 1, fox_kn_g[l], FOX_HEADS, FOX_HD, nb)
    sfv = sp_fox[:, 2 * FOX_W:]
    so_fox = fox_sample(sfq.reshape(nb, FOX_HEADS, FOX_HD), sfk.reshape(nb, FOX_HEADS, FOX_HD),
                        sfv.reshape(nb, FOX_HEADS, FOX_HD), s_logf.reshape(nb, FOX_HEADS, 1),
                        cache_fox_k[l], cache_fox_v[l], jnp.swapaxes(cache_fox_logf[l], 1, 2),
                        page_table, 8).reshape(nb, FOX_W)
    smq = sp_mq.reshape(nb, MEM_HEADS, MEM_HD)
    so_mem = mem_sample(smq, cache_mem_k[l], cache_mem_v[l], mem_qn_g[l]).reshape(nb, MEM_W)
    y_s = _channel_mix(xs, so_dn, so_fox, so_mem, wts, ln_ffn_g[l])

    return (
        y_p.reshape(batch, seq, D_MODEL),
        y_s.reshape(nb, 1, D_MODEL),
        fkn.reshape(1, batch, seq, FOX_HEADS, FOX_HD),
        fv.reshape(1, batch, seq, FOX_HEADS, FOX_HD),
        logf.reshape(1, batch, seq, FOX_HEADS),
        dn_state.reshape(1, batch, DN_HEADS, DN_D, DN_D),
        p.reshape(batch, seq, P_DN_COLS)[:, seq - (DN_CONV - 1):, :CONV_CH][None],
        mk.reshape(1, batch, MEM_TOKENS, MEM_HEADS, MEM_HD),
        mv.reshape(1, batch, MEM_TOKENS, MEM_HEADS, MEM_HD),
        sfk.reshape(1, nb, 1, FOX_HEADS, FOX_HD),
        sfv.reshape(1, nb, 1, FOX_HEADS, FOX_HD),
        s_logf.reshape(1, nb, 1, FOX_HEADS),
        s_state[None],
        conv_s[None],
    )
```
